```python
import math
import jax, jax.numpy as jnp
from jax import lax
import numpy as np

D_MODEL = 4096
BATCH = 4
SEQ = 2048
DEPTH = 2

HEAD_DIM = 128
N_BRANCH = 4
BRANCH_WIDTH = D_MODEL // N_BRANCH
A_HEADS = BRANCH_WIDTH // HEAD_DIM
A_LATENT = D_MODEL // 8
IDX_HEADS = 8
IDX_DIM = 64
IDX_TOPK = 256
A_BLOCK = 128
B_GROUPS = ((128, 1), (512, 4), (2048, 16))
B_HEADS = BRANCH_WIDTH // HEAD_DIM
B_BLOCK = 128
C_CHUNK = 128
C_GROUPS = 8
C_GROUP_WIDTH = BRANCH_WIDTH // C_GROUPS
D_CONV_WIDTH = 3
REL_BUCKETS = 32
REL_MAX_DIST = 2048
REL_HEADS = A_HEADS + len(B_GROUPS) * B_HEADS
FFN_DIM = 11008
N_EXPERTS = 8
TOP_K = 2
EXPERT_DIM = 3072
N_DENSE = (DEPTH + 1) // 2
N_MOE = DEPTH // 2
N_MOD = 6
EPS = 1e-6

IN_SPLITS = (
    A_HEADS * HEAD_DIM,
    A_LATENT,
    IDX_HEADS * IDX_DIM,
    IDX_DIM,
    IDX_HEADS,
    3 * len(B_GROUPS) * B_HEADS * HEAD_DIM,
    2 * BRANCH_WIDTH,
    3 * BRANCH_WIDTH,
    N_BRANCH * D_MODEL,
)
IN_WIDTH = sum(IN_SPLITS)

kernel_name = 'hybrid_dsa_dilated_gmlp_shortconv_moe_block'


def rms_norm(x, g):
    xf = x.astype(jnp.float32)
    y = xf * lax.rsqrt(jnp.mean(xf * xf, axis=-1, keepdims=True) + EPS)
    return y.astype(x.dtype) * g


def layer_norm(x, g, b):
    xf = x.astype(jnp.float32)
    mu = jnp.mean(xf, axis=-1, keepdims=True)
    var = jnp.mean(jnp.square(xf - mu), axis=-1, keepdims=True)
    return ((xf - mu) * lax.rsqrt(var + EPS)).astype(x.dtype) * g + b


def rel_bucket(dist):
    max_exact = REL_BUCKETS // 2
    d = jnp.maximum(dist, 0)
    df = jnp.maximum(d, max_exact).astype(jnp.float32)
    large = max_exact + (jnp.log(df / max_exact) / math.log(REL_MAX_DIST / max_exact)
                         * (REL_BUCKETS - max_exact)).astype(jnp.int32)
    large = jnp.minimum(large, REL_BUCKETS - 1)
    return jnp.where(d < max_exact, d, large)


def dsa_attention(q, lat, iq, ik, iw, w_uk, w_uv, bias_tab):
    bsz, seq = q.shape[:2]
    topk = min(IDX_TOPK, seq // 4)
    nb = seq // A_BLOCK
    q_abs = jnp.einsum('bshd,lhd->bshl', q, w_uk)
    key_pos = jnp.arange(seq)
    scale = HEAD_DIM ** -0.5

    def blocks(t):
        return jnp.moveaxis(t.reshape(bsz, nb, A_BLOCK, *t.shape[2:]), 1, 0)

    def one_block(args):
        qa, iqb, iwb, start = args
        qpos = start + jnp.arange(A_BLOCK)
        idx_logits = jnp.einsum('bqhd,bkd->bqhk', iqb, ik).astype(jnp.float32)
        score = jnp.einsum('bqh,bqhk->bqk', iwb.astype(jnp.float32), jax.nn.relu(idx_logits))
        causal = key_pos[None, :] <= qpos[:, None]
        score = jnp.where(causal[None], score, -jnp.inf)
        _, sel = lax.top_k(score, topk)
        valid = sel <= qpos[None, :, None]
        lat_sel = jax.vmap(lambda l, i: l[i])(lat, sel)
        logits = jnp.einsum('bqhl,bqkl->bqhk', qa, lat_sel).astype(jnp.float32) * scale
        bias = bias_tab[rel_bucket(qpos[None, :, None] - sel)]
        logits = logits + jnp.moveaxis(bias, -1, 2).astype(jnp.float32)
        logits = jnp.where(valid[:, :, None, :], logits, -jnp.inf)
        p = jax.nn.softmax(logits, axis=-1).astype(lat.dtype)
        o_lat = jnp.einsum('bqhk,bqkl->bqhl', p, lat_sel)
        return jnp.einsum('bqhl,lhd->bqhd', o_lat, w_uv)

    starts = jnp.arange(nb, dtype=jnp.int32) * A_BLOCK
    out = lax.map(one_block, (blocks(q_abs), blocks(iq), blocks(iw), starts))
    return jnp.moveaxis(out, 0, 1).reshape(bsz, seq, A_HEADS * HEAD_DIM)


def dilated_window_attention(q, k, v, window, dil, bias_tab):
    bsz, seq, nh, hd = q.shape
    span = window // dil
    n_sub = seq // dil
    nb = -(-n_sub // B_BLOCK)
    n_pad = nb * B_BLOCK

    def to_sub(t):
        t = t.reshape(bsz, n_sub, dil, nh, hd).transpose(0, 2, 1, 3, 4)
        return jnp.pad(t, ((0, 0), (0, 0), (0, n_pad - n_sub), (0, 0), (0, 0)))

    def band_keys(t):
        t = jnp.pad(to_sub(t), ((0, 0), (0, 0), (B_BLOCK, 0), (0, 0), (0, 0)))
        t = t.reshape(bsz, dil, nb + 1, B_BLOCK, nh, hd)
        return jnp.concatenate([t[:, :, :-1], t[:, :, 1:]], axis=3)

    qb = to_sub(q).reshape(bsz, dil, nb, B_BLOCK, nh, hd)
    kb, vb = band_keys(k), band_keys(v)
    qi = jnp.arange(B_BLOCK)[:, None]
    kj = jnp.arange(2 * B_BLOCK)[None, :]
    sub_dist = qi + B_BLOCK - kj
    key_sub_pos = jnp.arange(nb)[:, None, None] * B_BLOCK - B_BLOCK + kj[None]
    valid = ((sub_dist >= 0) & (sub_dist <= span))[None] & (key_sub_pos >= 0)
    bias = jnp.moveaxis(bias_tab[rel_bucket(sub_dist * dil)], -1, 0)
    logits = jnp.einsum('brnqhd,brnkhd->brnhqk', qb, kb).astype(jnp.float32) * hd ** -0.5
    logits = logits + bias.astype(jnp.float32)
    logits = jnp.where(valid[None, None, :, None], logits, -jnp.inf)
    m = jnp.max(logits, axis=-1, keepdims=True)
    e = jnp.exp(logits - m)
    den = jnp.sum(e, axis=-1, keepdims=True)
    o = jnp.einsum('brnhqk,brnkhd->brnqhd', (e / den).astype(v.dtype), vb)
    lse = jnp.swapaxes((m + jnp.log(den))[..., 0], -1, -2)

    def from_sub(t):
        t = t.reshape(bsz, dil, n_pad, *t.shape[4:])[:, :, :n_sub]
        return jnp.moveaxis(t, 1, 2).reshape(bsz, seq, *t.shape[3:])

    return from_sub(o), from_sub(lse)


def dilated_mixture(qkv, bias_tab):
    bsz, seq = qkv.shape[:2]
    outs, lses = [], []
    for g, (window, dil) in enumerate(B_GROUPS):
        o, l = dilated_window_attention(qkv[:, :, 0, g], qkv[:, :, 1, g], qkv[:, :, 2, g],
                                        window, dil, bias_tab[:, g * B_HEADS:(g + 1) * B_HEADS])
        outs.append(o)
        lses.append(l)
    wts = jax.nn.softmax(jnp.stack(lses).astype(jnp.float32), axis=0)
    o = jnp.sum(wts[..., None].astype(qkv.dtype) * jnp.stack(outs), axis=0)
    return o.reshape(bsz, seq, B_HEADS * HEAD_DIM)


def spatial_gating(uv, ln_g, ln_b, w_s, b_s):
    bsz, seq = uv.shape[:2]
    u, v = jnp.split(jax.nn.gelu(uv), 2, axis=-1)
    v = layer_norm(v, ln_g, ln_b).reshape(bsz, seq // C_CHUNK, C_CHUNK, C_GROUPS, C_GROUP_WIDTH)
    w = w_s * jnp.tril(jnp.ones((C_CHUNK, C_CHUNK), w_s.dtype))
    mixed = jnp.einsum('gts,bnsgc->bntgc', w, v) + b_s.T[:, :, None]
    return u * mixed.reshape(bsz, seq, BRANCH_WIDTH)


def short_conv(hbc, conv_w):
    h, b_gate, c_gate = jnp.split(hbc, 3, axis=-1)
    z = lax.conv_general_dilated(c_gate * h, conv_w[:, None, :], window_strides=(1,),
                                 padding=[(D_CONV_WIDTH - 1, 0)],
                                 dimension_numbers=('NWC', 'WIO', 'NWC'),
                                 feature_group_count=BRANCH_WIDTH)
    return b_gate * z


def hybrid_mixer(h, w_in, lat_g, w_uk, w_uv, c_ln_g, c_ln_b, c_w_s, c_b_s, d_conv,
                 w_branch, w_out, rel_bias):
    bsz, seq, _ = h.shape
    proj = h @ w_in
    a_q, a_lat, i_q, i_k, i_w, b_qkv, c_uv, d_hbc, gates = jnp.split(
        proj, np.cumsum(IN_SPLITS)[:-1].tolist(), axis=-1)
    o_a = dsa_attention(a_q.reshape(bsz, seq, A_HEADS, HEAD_DIM), rms_norm(a_lat, lat_g),
                        i_q.reshape(bsz, seq, IDX_HEADS, IDX_DIM), i_k, i_w, w_uk, w_uv,
                        rel_bias[:, :A_HEADS])
    o_b = dilated_mixture(b_qkv.reshape(bsz, seq, 3, len(B_GROUPS), B_HEADS, HEAD_DIM),
                          rel_bias[:, A_HEADS:])
    o_c = spatial_gating(c_uv, c_ln_g, c_ln_b, c_w_s, c_b_s)
    o_d = short_conv(d_hbc, d_conv)
    branches = jnp.stack([o_a, o_b, o_c, o_d], axis=2)
    projected = jnp.einsum('bsnw,nwd->bsnd', branches, w_branch)
    g = jax.nn.sigmoid(gates.reshape(bsz, seq, N_BRANCH, D_MODEL))
    return jnp.sum(g * projected, axis=2) @ w_out


def swiglu(h, wg, wu, wd):
    return (jax.nn.silu(h @ wg) * (h @ wu)) @ wd


def moe_swiglu(h, router, wg, wu, wd):
    logits = (h @ router).astype(jnp.float32)
    top_val, top_idx = lax.top_k(logits, TOP_K)
    gate = jax.nn.softmax(top_val, axis=-1)
    combine = jnp.sum(jax.nn.one_hot(top_idx, N_EXPERTS, dtype=jnp.float32) * gate[..., None],
                      axis=-2).astype(h.dtype)
    out = jnp.zeros_like(h)
    for e in range(N_EXPERTS):
        out = out + combine[..., e:e + 1] * swiglu(h, wg[e], wu[e], wd[e])
    return out


def setup_inputs(seed: int = 0) -> dict:
    key = jax.random.key(seed)
    ks = iter(jax.random.split(key, 32))

    def nrm(shape, scale=1.0):
        return jax.random.normal(next(ks), shape, jnp.float32) * scale

    def gain(shape):
        return 1.0 + nrm(shape, 0.05)

    W = BRANCH_WIDTH
    return {
        'x': nrm((BATCH, SEQ, D_MODEL)),
        'c': nrm((BATCH, D_MODEL)),
        'w_ada': nrm((D_MODEL, N_MOD * D_MODEL), D_MODEL ** -0.5),
        'b_ada': nrm((N_MOD * D_MODEL,), 0.01),
        'ada_table': nrm((DEPTH, N_MOD, D_MODEL), 0.1),
        'rel_bias': nrm((REL_BUCKETS, REL_HEADS), 0.2),
        'norm_pre_mix': gain((DEPTH, D_MODEL)),
        'norm_post_mix': gain((DEPTH, D_MODEL)),
        'norm_pre_ffn': gain((DEPTH, D_MODEL)),
        'norm_post_ffn': gain((DEPTH, D_MODEL)),
        'w_in': nrm((DEPTH, D_MODEL, IN_WIDTH), D_MODEL ** -0.5),
        'a_lat_norm': gain((DEPTH, A_LATENT)),
        'a_w_uk': nrm((DEPTH, A_LATENT, A_HEADS, HEAD_DIM), A_LATENT ** -0.5),
        'a_w_uv': nrm((DEPTH, A_LATENT, A_HEADS, HEAD_DIM), A_LATENT ** -0.5),
        'c_ln_g': gain((DEPTH, W)),
        'c_ln_b': nrm((DEPTH, W), 0.01),
        'c_w_s': nrm((DEPTH, C_GROUPS, C_CHUNK, C_CHUNK), C_CHUNK ** -0.5),
        'c_b_s': gain((DEPTH, C_GROUPS, C_CHUNK)),
        'd_conv': nrm((DEPTH, D_CONV_WIDTH, W), D_CONV_WIDTH ** -0.5),
        'w_branch': nrm((DEPTH, N_BRANCH, W, D_MODEL), W ** -0.5),
        'w_out': nrm((DEPTH, D_MODEL, D_MODEL), D_MODEL ** -0.5),
        'ffn_w_gate': nrm((N_DENSE, D_MODEL, FFN_DIM), D_MODEL ** -0.5),
        'ffn_w_up': nrm((N_DENSE, D_MODEL, FFN_DIM), D_MODEL ** -0.5),
        'ffn_w_down': nrm((N_DENSE, FFN_DIM, D_MODEL), FFN_DIM ** -0.5),
        'moe_router': nrm((N_MOE, D_MODEL, N_EXPERTS), D_MODEL ** -0.5),
        'moe_w_gate': nrm((N_MOE, N_EXPERTS, D_MODEL, EXPERT_DIM), D_MODEL ** -0.5),
        'moe_w_up': nrm((N_MOE, N_EXPERTS, D_MODEL, EXPERT_DIM), D_MODEL ** -0.5),
        'moe_w_down': nrm((N_MOE, N_EXPERTS, EXPERT_DIM, D_MODEL), EXPERT_DIM ** -0.5),
    }


def reference(x, c, w_ada, b_ada, ada_table, rel_bias, norm_pre_mix, norm_post_mix,
              norm_pre_ffn, norm_post_ffn, w_in, a_lat_norm, a_w_uk, a_w_uv, c_ln_g, c_ln_b,
              c_w_s, c_b_s, d_conv, w_branch, w_out, ffn_w_gate, ffn_w_up, ffn_w_down,
              moe_router, moe_w_gate, moe_w_up, moe_w_down):
    bsz = x.shape[0]
    mod_shared = (jax.nn.silu(c) @ w_ada + b_ada).reshape(bsz, N_MOD, D_MODEL)
    for layer in range(DEPTH):
        mod = mod_shared + ada_table[layer]
        shift_m, scale_m, gate_m, shift_f, scale_f, gate_f = [mod[:, i, None, :] for i in range(N_MOD)]
        h = rms_norm(x, norm_pre_mix[layer]) * (1 + scale_m) + shift_m
        y = hybrid_mixer(h, w_in[layer], a_lat_norm[layer], a_w_uk[layer], a_w_uv[layer],
                         c_ln_g[layer], c_ln_b[layer], c_w_s[layer], c_b_s[layer], d_conv[layer],
                         w_branch[layer], w_out[layer], rel_bias)
        x = x + gate_m * rms_norm(y, norm_post_mix[layer])
        h = rms_norm(x, norm_pre_ffn[layer]) * (1 + scale_f) + shift_f
        j = layer // 2
        if layer % 2 == 0:
            y = swiglu(h, ffn_w_gate[j], ffn_w_up[j], ffn_w_down[j])
        else:
            y = moe_swiglu(h, moe_router[j], moe_w_gate[j], moe_w_up[j], moe_w_down[j])
        x = x + gate_f * rms_norm(y, norm_post_ffn[layer])
    return x
```

```python
import functools
import math

import jax
import jax.numpy as jnp
from jax import lax
from jax.experimental import pallas as pl
from jax.experimental.pallas import tpu as pltpu

F32 = jnp.float32
BF16 = jnp.bfloat16

LANE = 128
V7X_VMEM_BYTES = 64 * 1024 * 1024
VMEM_CAP = V7X_VMEM_BYTES - 8 * 1024 * 1024

HEAD_DIM = 128
N_BRANCH = 4
IDX_HEADS = 8
IDX_DIM = 64
IDX_TOPK = 256
A_BLOCK = 128
B_GROUPS = ((128, 1), (512, 4), (2048, 16))
B_BLOCK = 128
C_CHUNK = 128
C_GROUPS = 8
D_CONV_WIDTH = 3
REL_BUCKETS = 32
REL_MAX_DIST = 2048
TOP_K = 2
N_MOD = 6
EPS = 1e-6
NEG_INF = float("-inf")


def _params(semantics, vmem_bytes):
    limit = int(min(VMEM_CAP, max(vmem_bytes * 5 // 4 + (4 << 20), 16 << 20)))
    return pltpu.CompilerParams(dimension_semantics=semantics, vmem_limit_bytes=limit)


def _pick_tile(n, candidates):
    for c in candidates:
        if n % c == 0:
            return c
    raise ValueError(f"no tile in {candidates} divides {n}")


def _cast_rows(src_fn, dst_ref, n_rows, chunk):
    def body(r, carry):
        rows = pl.ds(pl.multiple_of(r * chunk, chunk), chunk)
        dst_ref[rows, :] = src_fn(rows).astype(BF16)
        return carry
    lax.fori_loop(0, n_rows // chunk, body, 0)


def _mm_body(shift, tn, chunk, *refs):
    if shift:
        lhs_ref, w_ref, wx_ref, o_ref, wb_ref = refs
    else:
        lhs_ref, w_ref, o_ref, wb_ref = refs

    @pl.when(pl.program_id(1) == 0)
    def _():
        if shift:
            def src(rows):
                w = jnp.concatenate([w_ref[rows, :], wx_ref[rows, :]], axis=1)
                return w[:, shift:shift + tn]
        else:
            def src(rows):
                return w_ref[rows, :]
        _cast_rows(src, wb_ref, w_ref.shape[0], chunk)

    o_ref[...] = jnp.dot(lhs_ref[...], wb_ref[...],
                         preferred_element_type=F32).astype(o_ref.dtype)


def _matmul(lhs, w, *, col0, n_out, shift=0, tn=512, tm=512, out_dtype=F32, name="mm"):
    m, k = lhs.shape
    tn = min(tn, n_out)
    tm = min(tm, m)
    assert n_out % tn == 0 and m % tm == 0 and col0 % tn == 0 and tn % LANE == 0
    chunk = _pick_tile(k, (256, 128, 64, 8))
    in_specs = [
        pl.BlockSpec((tm, k), lambda j, i: (i, 0)),
        pl.BlockSpec((k, tn), lambda j, i: (0, col0 // tn + j)),
    ]
    args = [lhs, w]
    if shift:
        in_specs.append(pl.BlockSpec((k, LANE), lambda j, i: (0, (col0 + (j + 1) * tn) // LANE)))
        args.append(w)
    osz = jnp.dtype(out_dtype).itemsize
    vmem = 2 * tm * k * 2 + 2 * k * tn * 4 + (2 * k * LANE * 4 if shift else 0) + k * tn * 2 \
        + 2 * tm * tn * osz + tm * tn * 4 + chunk * (tn + LANE) * 8
    return pl.pallas_call(
        functools.partial(_mm_body, shift, tn, chunk),
        out_shape=jax.ShapeDtypeStruct((m, n_out), out_dtype),
        grid=(n_out // tn, m // tm),
        in_specs=in_specs,
        out_specs=pl.BlockSpec((tm, tn), lambda j, i: (i, j)),
        scratch_shapes=[pltpu.VMEM((k, tn), BF16)],
        compiler_params=_params(("arbitrary", "arbitrary"), vmem),
        name=name,
    )(*args)


def _ada_body(c_ref, w_ref, b_ref, o_ref):
    c = c_ref[...]
    a = (c * jax.nn.sigmoid(c)).astype(BF16)
    o_ref[...] = jnp.dot(a, w_ref[...].astype(BF16), preferred_element_type=F32) + b_ref[...]


def _ada(c_pad, w_ada, b_ada, tn=512):
    m, k = c_pad.shape
    n = w_ada.shape[1]
    tn = _pick_tile(n, (tn, 256, 128))
    vmem = 2 * k * tn * 4 + k * tn * 2 + 4 * m * k * 4
    return pl.pallas_call(
        _ada_body,
        out_shape=jax.ShapeDtypeStruct((m, n), F32),
        grid=(n // tn,),
        in_specs=[pl.BlockSpec((m, k), lambda j: (0, 0)),
                  pl.BlockSpec((k, tn), lambda j: (0, j)),
                  pl.BlockSpec((1, tn), lambda j: (0, j))],
        out_specs=pl.BlockSpec((m, tn), lambda j: (0, j)),
        compiler_params=_params(("arbitrary",), vmem),
        name="ada",
    )(c_pad, w_ada, b_ada.reshape(1, n))


def _rms(x, g):
    return x * lax.rsqrt(jnp.mean(x * x, axis=-1, keepdims=True) + EPS) * g


def _route(h, router_ref, n_experts):
    rows = h.shape[0]
    lane = lax.broadcasted_iota(jnp.int32, (rows, LANE), 1)
    logits = jnp.full((rows, LANE), NEG_INF, F32)
    for e in range(n_experts):
        le = jnp.sum(h * router_ref[e:e + 1, :], axis=-1, keepdims=True)
        logits = jnp.where(lane == e, le, logits)
    m1 = jnp.max(logits, axis=-1, keepdims=True)
    i1 = jnp.min(jnp.where(logits == m1, lane, LANE), axis=-1, keepdims=True)
    rest = jnp.where(lane == i1, NEG_INF, logits)
    m2 = jnp.max(rest, axis=-1, keepdims=True)
    i2 = jnp.min(jnp.where(rest == m2, lane, LANE), axis=-1, keepdims=True)
    e2 = jnp.exp(m2 - m1)
    den = 1.0 + e2
    return jnp.where(lane == i1, 1.0 / den, jnp.where(lane == i2, e2 / den, 0.0))


def _pre_body(x_ref, mod_ref, g_ref, h_ref):
    h = _rms(x_ref[...], g_ref[...]) * (1.0 + mod_ref[1:2, :]) + mod_ref[0:1, :]
    h_ref[...] = h.astype(h_ref.dtype)


def _prenorm(x3, mod, g, tr=256):
    b, s, d = x3.shape
    tr = min(tr, s)
    vmem = 2 * tr * d * 4 + 2 * tr * d * 2 + 4 * tr * d * 4
    return pl.pallas_call(
        _pre_body,
        out_shape=jax.ShapeDtypeStruct((b, s, d), BF16),
        grid=(b, s // tr),
        in_specs=[pl.BlockSpec((None, tr, d), lambda bi, i: (bi, i, 0)),
                  pl.BlockSpec((None, N_MOD, d), lambda bi, i: (bi, 0, 0)),
                  pl.BlockSpec((1, d), lambda bi, i: (0, 0))],
        out_specs=pl.BlockSpec((None, tr, d), lambda bi, i: (bi, i, 0)),
        compiler_params=_params(("arbitrary", "arbitrary"), vmem),
        name="prenorm",
    )(x3, mod, g.reshape(1, d))


def _post_body(gate_row, nxt, n_experts, *refs):
    x_ref, y_ref, mod_ref, gpost_ref = refs[:4]
    pos = 4
    if nxt is not None:
        modn_ref, gpre_ref = refs[pos:pos + 2]
        pos += 2
    if n_experts:
        router_ref = refs[pos]
        pos += 1
    xo_ref = refs[pos]
    pos += 1
    xn = x_ref[...] + mod_ref[gate_row:gate_row + 1, :] * _rms(y_ref[...], gpost_ref[...])
    xo_ref[...] = xn
    if nxt is not None:
        shift_row, scale_row = nxt
        h = _rms(xn, gpre_ref[...]) * (1.0 + modn_ref[scale_row:scale_row + 1, :]) \
            + modn_ref[shift_row:shift_row + 1, :]
        refs[pos][...] = h.astype(BF16)
        pos += 1
        if n_experts:
            refs[pos][...] = _route(h, router_ref, n_experts)


def _postnorm(x3, y3, mod, g_post, gate_row, *, nxt=None, mod_next=None, g_pre=None,
              router_t=None, tr=256):
    b, s, d = x3.shape
    tr = min(tr, s)
    n_experts = 0 if router_t is None else router_t.shape[0]
    row = lambda bi, i: (bi, i, 0)
    mod_spec = pl.BlockSpec((None, N_MOD, d), lambda bi, i: (bi, 0, 0))
    in_specs = [pl.BlockSpec((None, tr, d), row), pl.BlockSpec((None, tr, d), row), mod_spec,
                pl.BlockSpec((1, d), lambda bi, i: (0, 0))]
    args = [x3, y3, mod, g_post.reshape(1, d)]
    out_shape = [jax.ShapeDtypeStruct((b, s, d), F32)]
    out_specs = [pl.BlockSpec((None, tr, d), row)]
    if nxt is not None:
        in_specs += [mod_spec, pl.BlockSpec((1, d), lambda bi, i: (0, 0))]
        args += [mod_next, g_pre.reshape(1, d)]
        out_shape.append(jax.ShapeDtypeStruct((b, s, d), BF16))
        out_specs.append(pl.BlockSpec((None, tr, d), row))
    if n_experts:
        in_specs.append(pl.BlockSpec((n_experts, d), lambda bi, i: (0, 0)))
        args.append(router_t)
        out_shape.append(jax.ShapeDtypeStruct((b, s, LANE), F32))
        out_specs.append(pl.BlockSpec((None, tr, LANE), row))
    vmem = 2 * 3 * tr * d * 4 + 2 * tr * d * 2 + 6 * tr * d * 4
    return pl.pallas_call(
        functools.partial(_post_body, gate_row, nxt, n_experts),
        out_shape=out_shape,
        grid=(b, s // tr),
        in_specs=in_specs,
        out_specs=out_specs,
        compiler_params=_params(("arbitrary", "arbitrary"), vmem),
        name="postnorm",
    )(*args)


def _lat_body(alat_ref, pik_ref, g_ref, lat_ref, latt_ref, ikt_ref):
    lat = _rms(alat_ref[...], g_ref[...])
    lat_ref[...] = lat.astype(BF16)
    latt_ref[...] = lat.T.astype(BF16)
    ikt_ref[...] = pik_ref[...].T.astype(BF16)


def _latnorm(pa3, pik3, g_lat, lat_col, n_lat, tr=512):
    b, s, _ = pa3.shape
    tr = min(tr, s)
    assert lat_col % n_lat == 0
    vmem = 2 * tr * (n_lat + LANE) * 4 + 4 * tr * (n_lat + LANE) * 2 + 4 * tr * n_lat * 4
    return pl.pallas_call(
        _lat_body,
        out_shape=[jax.ShapeDtypeStruct((b, s, n_lat), BF16),
                   jax.ShapeDtypeStruct((b, n_lat, s), BF16),
                   jax.ShapeDtypeStruct((b, LANE, s), BF16)],
        grid=(b, s // tr),
        in_specs=[pl.BlockSpec((None, tr, n_lat), lambda bi, i: (bi, i, lat_col // n_lat)),
                  pl.BlockSpec((None, tr, LANE), lambda bi, i: (bi, i, 0)),
                  pl.BlockSpec((1, n_lat), lambda bi, i: (0, 0))],
        out_specs=[pl.BlockSpec((None, tr, n_lat), lambda bi, i: (bi, i, 0)),
                   pl.BlockSpec((None, n_lat, tr), lambda bi, i: (bi, 0, i)),
                   pl.BlockSpec((None, LANE, tr), lambda bi, i: (bi, 0, i))],
        compiler_params=_params(("arbitrary", "arbitrary"), vmem),
        name="latnorm",
    )(pa3, pik3, g_lat.reshape(1, n_lat))


def _dsa_body(topk, q_ref, iq_ref, iw_ref, ikt_ref, lat_ref, latt_ref, wuk_ref, wuv_ref,
              bias_ref, o_ref, madd_ref):
    i = pl.program_id(1)
    seq = lat_ref.shape[0]
    n_blk = seq // A_BLOCK
    n_heads = wuk_ref.shape[0]
    int_min = jnp.int32(-2 ** 31)
    qpos = i * A_BLOCK + lax.broadcasted_iota(jnp.int32, (A_BLOCK, 1), 0)
    kpos = lax.broadcasted_iota(jnp.int32, (A_BLOCK, seq), 1)

    iq = iq_ref[...].astype(BF16)
    iw = iw_ref[...]
    ikt = ikt_ref[0:IDX_DIM, :]
    score = jnp.zeros((A_BLOCK, seq), F32)
    for h in range(IDX_HEADS):
        l = jnp.dot(iq[:, h * IDX_DIM:(h + 1) * IDX_DIM], ikt, preferred_element_type=F32)
        score = score + iw[:, IDX_DIM + h:IDX_DIM + h + 1] * jnp.maximum(l, 0.0)

    score = jnp.where(score == 0.0, 0.0, score)
    key = pltpu.bitcast(score, jnp.int32)
    key = jnp.where(key < 0, key ^ jnp.int32(0x7FFFFFFF), key)
    key = jnp.where(kpos <= qpos, key, int_min)
    k_eff = jnp.minimum(qpos + 1, topk).astype(F32)

    def count(mask):
        return jnp.sum(jnp.where(mask, 1.0, 0.0), axis=1, keepdims=True)

    t0 = jnp.where(count(key >= 0) >= k_eff, jnp.int32(0), int_min)

    def value_step(it, t):
        c = t | (jnp.int32(1) << (30 - it))
        return jnp.where(count(key >= c) >= k_eff, c, t)

    t = lax.fori_loop(0, 31, value_step, t0)

    need = k_eff - count(key > t)
    eq = key == t
    idx_bits = max(1, (seq - 1).bit_length())

    def index_step(it, p):
        c = p | (jnp.int32(1) << (idx_bits - 1 - it))
        below = jnp.sum(jnp.where(eq, jnp.where(kpos < c, 1.0, 0.0), 0.0), axis=1, keepdims=True)
        return jnp.where(below < need, c, p)

    p = lax.fori_loop(0, idx_bits, index_step, jnp.zeros((A_BLOCK, 1), jnp.int32))
    tie_add = jnp.where(eq, jnp.where(kpos <= p, 0.0, NEG_INF), NEG_INF)
    madd_ref[...] = jnp.where(key > t, 0.0, tie_add)

    scale = HEAD_DIM ** -0.5
    for h in range(n_heads):
        cols = slice(h * HEAD_DIM, (h + 1) * HEAD_DIM)
        qa = jnp.dot(q_ref[:, cols].astype(BF16), wuk_ref[h],
                     preferred_element_type=F32).astype(BF16)
        lg = jnp.dot(qa, latt_ref[...], preferred_element_type=F32) * scale
        bias = jnp.concatenate([bias_ref[h, jnp.maximum(i - j, 0)] for j in range(n_blk)], axis=1)
        lg = lg + bias + madd_ref[...]
        m = jnp.max(lg, axis=1, keepdims=True)
        e = jnp.exp(lg - m)
        den = jnp.sum(e, axis=1, keepdims=True)
        o_lat = jnp.dot(e.astype(BF16), lat_ref[...], preferred_element_type=F32) / den
        o_ref[:, cols] = jnp.dot(o_lat.astype(BF16), wuv_ref[h],
                                 preferred_element_type=F32).astype(o_ref.dtype)


def _dsa(pa3, pik3, lat, latt, ikt, wuk_t, wuv_t, bias_a, *, q_width, iq_col, iq_width):
    b, s, _ = pa3.shape
    n_lat = lat.shape[-1]
    n_heads = wuk_t.shape[0]
    n_blk = s // A_BLOCK
    topk = min(IDX_TOPK, s // 4)
    assert iq_col % iq_width == 0
    vmem = 2 * (A_BLOCK * (q_width + iq_width + LANE) * 4 + LANE * s * 2 + 2 * s * n_lat * 2
                + 2 * n_heads * HEAD_DIM * n_lat * 2 + n_heads * n_blk * A_BLOCK * A_BLOCK * 4
                + A_BLOCK * q_width * 2) + 10 * A_BLOCK * s * 4
    return pl.pallas_call(
        functools.partial(_dsa_body, topk),
        out_shape=jax.ShapeDtypeStruct((b, s, q_width), BF16),
        grid=(b, n_blk),
        in_specs=[pl.BlockSpec((None, A_BLOCK, q_width), lambda bi, i: (bi, i, 0)),
                  pl.BlockSpec((None, A_BLOCK, iq_width), lambda bi, i: (bi, i, iq_col // iq_width)),
                  pl.BlockSpec((None, A_BLOCK, LANE), lambda bi, i: (bi, i, 0)),
                  pl.BlockSpec((None, LANE, s), lambda bi, i: (bi, 0, 0)),
                  pl.BlockSpec((None, s, n_lat), lambda bi, i: (bi, 0, 0)),
                  pl.BlockSpec((None, n_lat, s), lambda bi, i: (bi, 0, 0)),
                  pl.BlockSpec(wuk_t.shape, lambda bi, i: (0, 0, 0)),
                  pl.BlockSpec(wuv_t.shape, lambda bi, i: (0, 0, 0)),
                  pl.BlockSpec(bias_a.shape, lambda bi, i: (0, 0, 0, 0))],
        out_specs=pl.BlockSpec((None, A_BLOCK, q_width), lambda bi, i: (bi, i, 0)),
        scratch_shapes=[pltpu.VMEM((A_BLOCK, s), F32)],
        compiler_params=_params(("arbitrary", "arbitrary"), vmem),
        name="dsa",
    )(pa3, pa3, pik3, ikt, lat, latt, wuk_t, wuv_t, bias_a)


def _dil_body(span, n_heads, q_ref, kp_ref, kc_ref, vp_ref, vc_ref, bias_ref, o_ref, lse_ref):
    i = pl.program_id(2)
    qi = lax.broadcasted_iota(jnp.int32, (B_BLOCK, 2 * B_BLOCK), 0)
    kj = lax.broadcasted_iota(jnp.int32, (B_BLOCK, 2 * B_BLOCK), 1)
    sub_dist = qi + B_BLOCK - kj
    in_band = jnp.where(sub_dist <= span,
                        jnp.where((i - 1) * B_BLOCK + kj >= 0, 0.0, NEG_INF), NEG_INF)
    madd = jnp.where(sub_dist >= 0, in_band, NEG_INF)
    lane = lax.broadcasted_iota(jnp.int32, (B_BLOCK, LANE), 1)
    lse_acc = jnp.zeros((B_BLOCK, LANE), F32)
    scale = HEAD_DIM ** -0.5
    for h in range(n_heads):
        cols = slice(h * HEAD_DIM, (h + 1) * HEAD_DIM)
        q = q_ref[:, cols].astype(BF16)
        k2 = jnp.concatenate([kp_ref[:, cols], kc_ref[:, cols]], axis=0).astype(BF16)
        v2 = jnp.concatenate([vp_ref[:, cols], vc_ref[:, cols]], axis=0).astype(BF16)
        lg = lax.dot_general(q, k2, (((1,), (1,)), ((), ())), preferred_element_type=F32)
        lg = lg * scale + bias_ref[h] + madd
        m = jnp.max(lg, axis=1, keepdims=True)
        e = jnp.exp(lg - m)
        den = jnp.sum(e, axis=1, keepdims=True)
        o_ref[:, cols] = jnp.dot((e / den).astype(BF16), v2, preferred_element_type=F32)
        lse_acc = jnp.where(lane == h, m + jnp.log(den), lse_acc)
    lse_ref[...] = lse_acc


def _dilated(rest3, bias_g, g, n_groups, window, dil, width):
    b, s, w_all = rest3.shape
    n_heads = width // HEAD_DIM
    span = window // dil
    n_sub = s // dil
    assert span == B_BLOCK and n_sub % B_BLOCK == 0 and w_all % width == 0
    nb = n_sub // B_BLOCK
    wb = w_all // width
    sub = rest3.reshape(b, n_sub, dil * w_all)
    cur = lambda slot: (lambda bi, r, i: (bi, i, r * wb + slot * n_groups + g))
    prev = lambda slot: (lambda bi, r, i: (bi, jnp.maximum(i - 1, 0), r * wb + slot * n_groups + g))
    blk = (None, B_BLOCK, width)
    vmem = 2 * 6 * B_BLOCK * width * 4 + 2 * n_heads * B_BLOCK * 2 * B_BLOCK * 4 + (4 << 20)
    o, lse = pl.pallas_call(
        functools.partial(_dil_body, span, n_heads),
        out_shape=[jax.ShapeDtypeStruct((b, n_sub, dil * width), F32),
                   jax.ShapeDtypeStruct((b, n_sub, dil * LANE), F32)],
        grid=(b, dil, nb),
        in_specs=[pl.BlockSpec(blk, cur(0)), pl.BlockSpec(blk, prev(1)), pl.BlockSpec(blk, cur(1)),
                  pl.BlockSpec(blk, prev(2)), pl.BlockSpec(blk, cur(2)),
                  pl.BlockSpec(bias_g.shape, lambda bi, r, i: (0, 0, 0))],
        out_specs=[pl.BlockSpec(blk, lambda bi, r, i: (bi, i, r)),
                   pl.BlockSpec((None, B_BLOCK, LANE), lambda bi, r, i: (bi, i, r))],
        compiler_params=_params(("arbitrary", "arbitrary", "arbitrary"), vmem),
        name=f"dilated{g}",
    )(sub, sub, sub, sub, sub, bias_g)
    return o.reshape(b * s, width), lse.reshape(b * s, LANE)


def _mix_body(n_groups, n_heads, *refs):
    o_refs = refs[:n_groups]
    l_refs = refs[n_groups:2 * n_groups]
    out_ref = refs[2 * n_groups]
    ls = [r[...] for r in l_refs]
    m = functools.reduce(jnp.maximum, ls)
    es = [jnp.exp(l - m) for l in ls]
    tot = functools.reduce(jnp.add, es)
    ws = [e / tot for e in es]
    for h in range(n_heads):
        cols = slice(h * HEAD_DIM, (h + 1) * HEAD_DIM)
        acc = ws[0][:, h:h + 1] * o_refs[0][:, cols]
        for gi in range(1, n_groups):
            acc = acc + ws[gi][:, h:h + 1] * o_refs[gi][:, cols]
        out_ref[:, cols] = acc.astype(out_ref.dtype)


def _mixture(outs, lses, tr=512):
    t, width = outs[0].shape
    tr = min(tr, t)
    n_groups = len(outs)
    vmem = 2 * n_groups * tr * (width + LANE) * 4 + 2 * tr * width * 2 + 4 * tr * width * 4
    return pl.pallas_call(
        functools.partial(_mix_body, n_groups, width // HEAD_DIM),
        out_shape=jax.ShapeDtypeStruct((t, width), BF16),
        grid=(t // tr,),
        in_specs=[pl.BlockSpec((tr, width), lambda i: (i, 0))] * n_groups
        + [pl.BlockSpec((tr, LANE), lambda i: (i, 0))] * n_groups,
        out_specs=pl.BlockSpec((tr, width), lambda i: (i, 0)),
        compiler_params=_params(("arbitrary",), vmem),
        name="mixture",
    )(*outs, *lses)


def _gmlp_body(u_ref, v_ref, g_ref, b_ref, ws_ref, bs_ref, o_ref):
    u = jax.nn.gelu(u_ref[...])
    v = jax.nn.gelu(v_ref[...])
    mu = jnp.mean(v, axis=-1, keepdims=True)
    var = jnp.mean(jnp.square(v - mu), axis=-1, keepdims=True)
    v = (v - mu) * lax.rsqrt(var + EPS) * g_ref[...] + b_ref[...]
    ti = lax.broadcasted_iota(jnp.int32, (C_CHUNK, C_CHUNK), 0)
    si = lax.broadcasted_iota(jnp.int32, (C_CHUNK, C_CHUNK), 1)
    causal = si <= ti
    gw = u.shape[1] // C_GROUPS
    for g in range(C_GROUPS):
        cols = slice(g * gw, (g + 1) * gw)
        w = jnp.where(causal, ws_ref[g], 0.0).astype(BF16)
        mixed = jnp.dot(w, v[:, cols].astype(BF16), preferred_element_type=F32) + bs_ref[:, g:g + 1]
        o_ref[:, cols] = (u[:, cols] * mixed).astype(o_ref.dtype)


def _gmlp(rest, u_col, width, ln_g, ln_b, w_s, b_s_t):
    t = rest.shape[0]
    assert u_col % width == 0 and (width // C_GROUPS) % LANE == 0
    vmem = 2 * 2 * C_CHUNK * width * 4 + 2 * C_CHUNK * width * 2 + 2 * C_GROUPS * C_CHUNK * C_CHUNK * 4 \
        + 8 * C_CHUNK * width * 4
    return pl.pallas_call(
        _gmlp_body,
        out_shape=jax.ShapeDtypeStruct((t, width), BF16),
        grid=(t // C_CHUNK,),
        in_specs=[pl.BlockSpec((C_CHUNK, width), lambda i: (i, u_col // width)),
                  pl.BlockSpec((C_CHUNK, width), lambda i: (i, u_col // width + 1)),
                  pl.BlockSpec((1, width), lambda i: (0, 0)),
                  pl.BlockSpec((1, width), lambda i: (0, 0)),
                  pl.BlockSpec(w_s.shape, lambda i: (0, 0, 0)),
                  pl.BlockSpec(b_s_t.shape, lambda i: (0, 0))],
        out_specs=pl.BlockSpec((C_CHUNK, width), lambda i: (i, 0)),
        compiler_params=_params(("arbitrary",), vmem),
        name="gmlp",
    )(rest, rest, ln_g.reshape(1, width), ln_b.reshape(1, width), w_s, b_s_t)


def _conv_body(h_ref, bg_ref, cg_ref, w_ref, o_ref, carry_ref):
    @pl.when(pl.program_id(1) == 0)
    def _():
        carry_ref[...] = jnp.zeros_like(carry_ref)

    ch = cg_ref[...] * h_ref[...]
    rows = ch.shape[0]
    row = lax.broadcasted_iota(jnp.int32, ch.shape, 0)
    pm1 = carry_ref[7:8, :]
    pm2 = carry_ref[6:7, :]
    s1 = jnp.where(row == 0, pm1, pltpu.roll(ch, 1, 0))
    s2 = jnp.where(row == 0, pm2, jnp.where(row == 1, pm1, pltpu.roll(ch, 2, 0)))
    z = w_ref[0:1, :] * s2 + w_ref[1:2, :] * s1 + w_ref[2:3, :] * ch
    o_ref[...] = (bg_ref[...] * z).astype(o_ref.dtype)
    carry_ref[...] = ch[rows - 8:, :]


def _short_conv(rest3, h_col, width, conv_w, tr=256):
    b, s, _ = rest3.shape
    tr = min(tr, s)
    assert h_col % width == 0 and conv_w.shape[0] == D_CONV_WIDTH
    cb = h_col // width
    vmem = 2 * 3 * tr * width * 4 + 2 * tr * width * 2 + 6 * tr * width * 4
    return pl.pallas_call(
        _conv_body,
        out_shape=jax.ShapeDtypeStruct((b, s, width), BF16),
        grid=(b, s // tr),
        in_specs=[pl.BlockSpec((None, tr, width), lambda bi, i: (bi, i, cb)),
                  pl.BlockSpec((None, tr, width), lambda bi, i: (bi, i, cb + 1)),
                  pl.BlockSpec((None, tr, width), lambda bi, i: (bi, i, cb + 2)),
                  pl.BlockSpec(conv_w.shape, lambda bi, i: (0, 0))],
        out_specs=pl.BlockSpec((None, tr, width), lambda bi, i: (bi, i, 0)),
        scratch_shapes=[pltpu.VMEM((8, width), F32)],
        compiler_params=_params(("arbitrary", "arbitrary"), vmem),
        name="short_conv",
    )(rest3, rest3, rest3, conv_w)


def _merge_body(chunk, *refs):
    br_refs = refs[:N_BRANCH]
    gate_refs = refs[N_BRANCH:2 * N_BRANCH]
    w_ref, o_ref, wb_ref = refs[2 * N_BRANCH:]

    @pl.when(pl.program_id(1) == 0)
    def _():
        for n in range(N_BRANCH):
            _cast_rows(lambda rows, n=n: w_ref[n, rows, :], wb_ref.at[n], w_ref.shape[1], chunk)

    acc = None
    for n in range(N_BRANCH):
        proj = jnp.dot(br_refs[n][...], wb_ref[n], preferred_element_type=F32)
        term = jax.nn.sigmoid(gate_refs[n][...]) * proj
        acc = term if acc is None else acc + term
    o_ref[...] = acc.astype(o_ref.dtype)


def _merge(branches, rest, gate_col, w_branch, tn=512, tm=512):
    t, width = branches[0].shape
    d = w_branch.shape[-1]
    tn = min(tn, d)
    tm = min(tm, t)
    assert d % tn == 0 and t % tm == 0 and gate_col % tn == 0
    chunk = _pick_tile(width, (256, 128, 8))
    gate_spec = lambda n: pl.BlockSpec((tm, tn), lambda j, i: (i, (gate_col + n * d) // tn + j))
    vmem = 2 * N_BRANCH * (tm * width * 2 + tm * tn * 4 + width * tn * 4) + N_BRANCH * width * tn * 2 \
        + 2 * tm * tn * 2 + 4 * tm * tn * 4
    return pl.pallas_call(
        functools.partial(_merge_body, chunk),
        out_shape=jax.ShapeDtypeStruct((t, d), BF16),
        grid=(d // tn, t // tm),
        in_specs=[pl.BlockSpec((tm, width), lambda j, i: (i, 0))] * N_BRANCH
        + [gate_spec(n) for n in range(N_BRANCH)]
        + [pl.BlockSpec((N_BRANCH, width, tn), lambda j, i: (0, 0, j))],
        out_specs=pl.BlockSpec((tm, tn), lambda j, i: (i, j)),
        scratch_shapes=[pltpu.VMEM((N_BRANCH, width, tn), BF16)],
        compiler_params=_params(("arbitrary", "arbitrary"), vmem),
        name="merge",
    )(*branches, *([rest] * N_BRANCH), w_branch)


def _up_body(chunk, h_ref, wg_ref, wu_ref, o_ref, wgb_ref, wub_ref):
    @pl.when(pl.program_id(1) == 0)
    def _():
        _cast_rows(lambda rows: wg_ref[rows, :], wgb_ref, wg_ref.shape[0], chunk)
        _cast_rows(lambda rows: wu_ref[rows, :], wub_ref, wu_ref.shape[0], chunk)

    h = h_ref[...]
    g = jnp.dot(h, wgb_ref[...], preferred_element_type=F32)
    u = jnp.dot(h, wub_ref[...], preferred_element_type=F32)
    o_ref[...] = (g * jax.nn.sigmoid(g) * u).astype(o_ref.dtype)


def _swiglu_up(h, wg, wu, tm=512):
    t, k = h.shape
    n_e, _, f = wg.shape
    tn = _pick_tile(f, (256, 128))
    tm = min(tm, t)
    nj = f // tn
    chunk = _pick_tile(k, (256, 128, 8))
    w_spec = pl.BlockSpec((None, k, tn), lambda j, i: (j // nj, 0, j % nj))
    vmem = 2 * tm * k * 2 + 4 * k * tn * 4 + 2 * k * tn * 2 + 2 * tm * tn * 2 + 4 * tm * tn * 4
    return pl.pallas_call(
        functools.partial(_up_body, chunk),
        out_shape=jax.ShapeDtypeStruct((t, n_e * f), BF16),
        grid=(n_e * nj, t // tm),
        in_specs=[pl.BlockSpec((tm, k), lambda j, i: (i, 0)), w_spec, w_spec],
        out_specs=pl.BlockSpec((tm, tn), lambda j, i: (i, j)),
        scratch_shapes=[pltpu.VMEM((k, tn), BF16), pltpu.VMEM((k, tn), BF16)],
        compiler_params=_params(("arbitrary", "arbitrary"), vmem),
        name="swiglu_up",
    )(h, wg, wu)


def _down_body(scaled, tm, chunk, *refs):
    if scaled:
        a_ref, w_ref, c_ref, o_ref, wb_ref = refs
    else:
        a_ref, w_ref, o_ref, wb_ref = refs
    kk = pl.program_id(1)
    i = pl.program_id(2)

    @pl.when(i == 0)
    def _():
        _cast_rows(lambda rows: w_ref[rows, :], wb_ref, w_ref.shape[0], chunk)

    part = jnp.dot(a_ref[...], wb_ref[...], preferred_element_type=F32)
    if scaled:
        lane = lax.broadcasted_iota(jnp.int32, c_ref.shape, 1)
        part = part * jnp.sum(jnp.where(lane == kk, c_ref[...], 0.0), axis=1, keepdims=True)
    rows = pl.ds(pl.multiple_of(i * tm, tm), tm)

    @pl.when(kk == 0)
    def _():
        o_ref[rows, :] = part

    @pl.when(kk > 0)
    def _():
        o_ref[rows, :] += part


def _swiglu_down(a, wd, tk, comb=None, tn=256, tm=512):
    t, k = a.shape
    n = wd.shape[1]
    tn = min(tn, n)
    tm = min(tm, t)
    assert k % tk == 0 and n % tn == 0 and t % tm == 0
    chunk = _pick_tile(tk, (256, 128, 8))
    in_specs = [pl.BlockSpec((tm, tk), lambda j, kk, i: (i, kk)),
                pl.BlockSpec((tk, tn), lambda j, kk, i: (kk, j))]
    args = [a, wd]
    if comb is not None:
        in_specs.append(pl.BlockSpec((tm, LANE), lambda j, kk, i: (i, 0)))
        args.append(comb)
    vmem = 2 * tm * tk * 2 + 2 * tk * tn * 4 + tk * tn * 2 + 2 * t * tn * 4 + 4 * tm * tn * 4
    return pl.pallas_call(
        functools.partial(_down_body, comb is not None, tm, chunk),
        out_shape=jax.ShapeDtypeStruct((t, n), F32),
        grid=(n // tn, k // tk, t // tm),
        in_specs=in_specs,
        out_specs=pl.BlockSpec((t, tn), lambda j, kk, i: (0, j)),
        scratch_shapes=[pltpu.VMEM((tk, tn), BF16)],
        compiler_params=_params(("arbitrary", "arbitrary", "arbitrary"), vmem),
        name="swiglu_down",
    )(*args)


def _rel_bucket(dist):
    max_exact = REL_BUCKETS // 2
    d = jnp.maximum(dist, 0)
    df = jnp.maximum(d, max_exact).astype(F32)
    large = max_exact + (jnp.log(df / max_exact) / math.log(REL_MAX_DIST / max_exact)
                         * (REL_BUCKETS - max_exact)).astype(jnp.int32)
    large = jnp.minimum(large, REL_BUCKETS - 1)
    return jnp.where(d < max_exact, d, large)


def _bias_tiles_a(rel_bias, n_heads, seq):
    n_blk = seq // A_BLOCK
    qi = jnp.arange(A_BLOCK)[:, None]
    kj = jnp.arange(A_BLOCK)[None, :]
    dist = jnp.arange(n_blk)[:, None, None] * A_BLOCK + (qi - kj)[None]
    return jnp.moveaxis(rel_bias[:, :n_heads][_rel_bucket(dist)], -1, 0)


def _bias_tiles_b(rel_bias, col0, n_heads, dil):
    qi = jnp.arange(B_BLOCK)[:, None]
    kj = jnp.arange(2 * B_BLOCK)[None, :]
    sub_dist = qi + B_BLOCK - kj
    return jnp.moveaxis(rel_bias[:, col0:col0 + n_heads][_rel_bucket(sub_dist * dil)], -1, 0)


def _hybrid_mixer(h, bsz, seq, w_in, lat_g, w_uk, w_uv, c_ln_g, c_ln_b, c_w_s, c_b_s, d_conv,
                  w_branch, w_out, rel_bias):
    t, d = h.shape
    width = d // N_BRANCH
    n_lat = w_uk.shape[0]
    a_heads = width // HEAD_DIM
    n_groups = len(B_GROUPS)
    iq_width = IDX_HEADS * IDX_DIM
    lat_col = width
    iq_col = lat_col + n_lat
    ik_col = iq_col + iq_width
    front = ik_col
    shift = IDX_DIM + IDX_HEADS
    qkv_w = 3 * n_groups * width
    rest_w = qkv_w + 2 * width + 3 * width + N_BRANCH * d
    assert front % 512 == 0 and w_in.shape[1] == front + shift + rest_w and shift < LANE

    pa = _matmul(h, w_in, col0=0, n_out=front, name="proj_front")
    pik = _matmul(h, w_in, col0=front, n_out=LANE, name="proj_index_key")
    rest = _matmul(h, w_in, col0=front, n_out=rest_w, shift=shift, name="proj_rest")

    pa3 = pa.reshape(bsz, seq, front)
    pik3 = pik.reshape(bsz, seq, LANE)
    rest3 = rest.reshape(bsz, seq, rest_w)

    lat, latt, ikt = _latnorm(pa3, pik3, lat_g, lat_col, n_lat)
    wuk_t = jnp.transpose(w_uk, (1, 2, 0)).astype(BF16)
    wuv_t = jnp.transpose(w_uv, (1, 0, 2)).astype(BF16)
    bias_a = _bias_tiles_a(rel_bias, a_heads, seq)
    o_a = _dsa(pa3, pik3, lat, latt, ikt, wuk_t, wuv_t, bias_a,
               q_width=width, iq_col=iq_col, iq_width=iq_width).reshape(t, width)

    outs, lses = [], []
    for g, (window, dil) in enumerate(B_GROUPS):
        bias_g = _bias_tiles_b(rel_bias, a_heads + g * (width // HEAD_DIM), width // HEAD_DIM, dil)
        o, l = _dilated(rest3, bias_g, g, n_groups, window, dil, width)
        outs.append(o)
        lses.append(l)
    o_b = _mixture(outs, lses)

    o_c = _gmlp(rest, qkv_w, width, c_ln_g, c_ln_b, c_w_s, c_b_s.T)
    o_d = _short_conv(rest3, qkv_w + 2 * width, width, d_conv).reshape(t, width)

    mixed = _merge([o_a, o_b, o_c, o_d], rest, qkv_w + 5 * width, w_branch)
    return _matmul(mixed, w_out, col0=0, n_out=d, name="proj_out")


def kernel(x, c, w_ada, b_ada, ada_table, rel_bias, norm_pre_mix, norm_post_mix, norm_pre_ffn,
           norm_post_ffn, w_in, a_lat_norm, a_w_uk, a_w_uv, c_ln_g, c_ln_b, c_w_s, c_b_s, d_conv,
           w_branch, w_out, ffn_w_gate, ffn_w_up, ffn_w_down, moe_router, moe_w_gate, moe_w_up,
           moe_w_down):
    bsz, seq, d = x.shape
    depth = w_in.shape[0]
    t = bsz * seq

    c_pad = jnp.pad(c, ((0, 16 - bsz % 16 if bsz % 16 else 0), (0, 0)))
    mod_shared = _ada(c_pad, w_ada, b_ada)[:bsz].reshape(bsz, N_MOD, d)
    mods = [mod_shared + ada_table[layer] for layer in range(depth)]

    h = _prenorm(x, mods[0], norm_pre_mix[0])
    for layer in range(depth):
        mod = mods[layer]
        y = _hybrid_mixer(h.reshape(t, d), bsz, seq, w_in[layer], a_lat_norm[layer],
                          a_w_uk[layer], a_w_uv[layer], c_ln_g[layer], c_ln_b[layer],
                          c_w_s[layer], c_b_s[layer], d_conv[layer], w_branch[layer],
                          w_out[layer], rel_bias)
        j = layer // 2
        dense = layer % 2 == 0
        res = _postnorm(x, y.reshape(bsz, seq, d), mod, norm_post_mix[layer], 2, nxt=(3, 4),
                        mod_next=mod, g_pre=norm_pre_ffn[layer],
                        router_t=None if dense else moe_router[j].T)
        if dense:
            x, h = res
            a = _swiglu_up(h.reshape(t, d), ffn_w_gate[j][None], ffn_w_up[j][None])
            f = ffn_w_down.shape[1]
            tk = f // 2 if (f // 2) % LANE == 0 and f > 4096 else f
            y = _swiglu_down(a, ffn_w_down[j], tk)
        else:
            x, h, comb = res
            n_e, f, _ = moe_w_down[j].shape
            a = _swiglu_up(h.reshape(t, d), moe_w_gate[j], moe_w_up[j])
            y = _swiglu_down(a, moe_w_down[j].reshape(n_e * f, d), f, comb=comb.reshape(t, LANE))
        y3 = y.reshape(bsz, seq, d)
        if layer + 1 < depth:
            x, h = _postnorm(x, y3, mod, norm_post_ffn[layer], 5, nxt=(0, 1),
                             mod_next=mods[layer + 1], g_pre=norm_pre_mix[layer + 1])
        else:
            (x,) = _postnorm(x, y3, mod, norm_post_ffn[layer], 5)
    return x
```

```python
import functools
import math

import jax
import jax.numpy as jnp
from jax import lax
from jax.experimental import pallas as pl
from jax.experimental.pallas import tpu as pltpu

F32 = jnp.float32
BF16 = jnp.bfloat16

LANE = 128
V7X_VMEM_BYTES = 64 * 1024 * 1024
VMEM_CAP = V7X_VMEM_BYTES - 8 * 1024 * 1024

HEAD_DIM = 128
N_BRANCH = 4
IDX_HEADS = 8
IDX_DIM = 64
IDX_TOPK = 256
A_BLOCK = 128
B_GROUPS = ((128, 1), (512, 4), (2048, 16))
B_BLOCK = 128
C_CHUNK = 128
C_GROUPS = 8
D_CONV_WIDTH = 3
REL_BUCKETS = 32
REL_MAX_DIST = 2048
TOP_K = 2
N_MOD = 6
EPS = 1e-6
NEG_INF = float("-inf")


def _params(semantics, vmem_bytes):
    limit = int(min(VMEM_CAP, max(vmem_bytes * 5 // 4 + (4 << 20), 16 << 20)))
    return pltpu.CompilerParams(dimension_semantics=semantics, vmem_limit_bytes=limit)


def _pick_tile(n, candidates):
    for c in candidates:
        if n % c == 0:
            return c
    raise ValueError(f"no tile in {candidates} divides {n}")


def _cast_rows(src_fn, dst_ref, n_rows, chunk):
    def body(r, carry):
        rows = pl.ds(pl.multiple_of(r * chunk, chunk), chunk)
        dst_ref[rows, :] = src_fn(rows).astype(BF16)
        return carry
    lax.fori_loop(0, n_rows // chunk, body, 0)


def _mm_body(shift, tn, chunk, *refs):
    if shift:
        lhs_ref, w_ref, wx_ref, o_ref, wb_ref = refs
    else:
        lhs_ref, w_ref, o_ref, wb_ref = refs

    @pl.when(pl.program_id(1) == 0)
    def _():
        if shift:
            def src(rows):
                w = jnp.concatenate([w_ref[rows, :], wx_ref[rows, :]], axis=1)
                return w[:, shift:shift + tn]
        else:
            def src(rows):
                return w_ref[rows, :]
        _cast_rows(src, wb_ref, w_ref.shape[0], chunk)

    o_ref[...] = jnp.dot(lhs_ref[...], wb_ref[...],
                         preferred_element_type=F32).astype(o_ref.dtype)


def _matmul(lhs, w, layer, *, col0, n_out, shift=0, tn=512, tm=512, out_dtype=F32, name="mm"):
    m, k = lhs.shape
    tn = min(tn, n_out)
    tm = min(tm, m)
    assert n_out % tn == 0 and m % tm == 0 and col0 % tn == 0 and tn % LANE == 0
    chunk = _pick_tile(k, (256, 128, 64, 8))
    in_specs = [
        pl.BlockSpec((tm, k), lambda j, i: (i, 0)),
        pl.BlockSpec((None, k, tn), lambda j, i: (layer, 0, col0 // tn + j)),
    ]
    args = [lhs, w]
    if shift:
        in_specs.append(pl.BlockSpec((None, k, LANE),
                                     lambda j, i: (layer, 0, (col0 + (j + 1) * tn) // LANE)))
        args.append(w)
    osz = jnp.dtype(out_dtype).itemsize
    vmem = 2 * tm * k * 2 + 2 * k * tn * 4 + (2 * k * LANE * 4 if shift else 0) + k * tn * 2 \
        + 2 * tm * tn * osz + tm * tn * 4 + chunk * (tn + LANE) * 8
    return pl.pallas_call(
        functools.partial(_mm_body, shift, tn, chunk),
        out_shape=jax.ShapeDtypeStruct((m, n_out), out_dtype),
        grid=(n_out // tn, m // tm),
        in_specs=in_specs,
        out_specs=pl.BlockSpec((tm, tn), lambda j, i: (i, j)),
        scratch_shapes=[pltpu.VMEM((k, tn), BF16)],
        compiler_params=_params(("arbitrary", "arbitrary"), vmem),
        name=name,
    )(*args)


def _ada_body(c_ref, w_ref, b_ref, o_ref):
    c = c_ref[...]
    a = (c * jax.nn.sigmoid(c)).astype(BF16)
    o_ref[...] = jnp.dot(a, w_ref[...].astype(BF16), preferred_element_type=F32) + b_ref[...]


def _ada(c_pad, w_ada, b_ada, tn=512):
    m, k = c_pad.shape
    n = w_ada.shape[1]
    tn = _pick_tile(n, (tn, 256, 128))
    vmem = 2 * k * tn * 4 + k * tn * 2 + 4 * m * k * 4
    return pl.pallas_call(
        _ada_body,
        out_shape=jax.ShapeDtypeStruct((m, n), F32),
        grid=(n // tn,),
        in_specs=[pl.BlockSpec((m, k), lambda j: (0, 0)),
                  pl.BlockSpec((k, tn), lambda j: (0, j)),
                  pl.BlockSpec((1, tn), lambda j: (0, j))],
        out_specs=pl.BlockSpec((m, tn), lambda j: (0, j)),
        compiler_params=_params(("arbitrary",), vmem),
        name="ada",
    )(c_pad, w_ada, b_ada.reshape(1, n))


def _rms(x, g):
    return x * lax.rsqrt(jnp.mean(x * x, axis=-1, keepdims=True) + EPS) * g


def _route(h, router_ref, n_experts):
    rows = h.shape[0]
    lane = lax.broadcasted_iota(jnp.int32, (rows, LANE), 1)
    logits = jnp.full((rows, LANE), NEG_INF, F32)
    for e in range(n_experts):
        le = jnp.sum(h * router_ref[e:e + 1, :], axis=-1, keepdims=True)
        logits = jnp.where(lane == e, le, logits)
    m1 = jnp.max(logits, axis=-1, keepdims=True)
    i1 = jnp.min(jnp.where(logits == m1, lane, LANE), axis=-1, keepdims=True)
    rest = jnp.where(lane == i1, NEG_INF, logits)
    m2 = jnp.max(rest, axis=-1, keepdims=True)
    i2 = jnp.min(jnp.where(rest == m2, lane, LANE), axis=-1, keepdims=True)
    e2 = jnp.exp(m2 - m1)
    den = 1.0 + e2
    return jnp.where(lane == i1, 1.0 / den, jnp.where(lane == i2, e2 / den, 0.0))


def _pre_body(x_ref, mod_ref, g_ref, h_ref):
    h = _rms(x_ref[...], g_ref[...]) * (1.0 + mod_ref[1:2, :]) + mod_ref[0:1, :]
    h_ref[...] = h.astype(h_ref.dtype)


def _prenorm(x3, mod, g, tr=256):
    b, s, d = x3.shape
    tr = min(tr, s)
    vmem = 2 * tr * d * 4 + 2 * tr * d * 2 + 4 * tr * d * 4
    return pl.pallas_call(
        _pre_body,
        out_shape=jax.ShapeDtypeStruct((b, s, d), BF16),
        grid=(b, s // tr),
        in_specs=[pl.BlockSpec((None, tr, d), lambda bi, i: (bi, i, 0)),
                  pl.BlockSpec((None, N_MOD, d), lambda bi, i: (bi, 0, 0)),
                  pl.BlockSpec((1, d), lambda bi, i: (0, 0))],
        out_specs=pl.BlockSpec((None, tr, d), lambda bi, i: (bi, i, 0)),
        compiler_params=_params(("arbitrary", "arbitrary"), vmem),
        name="prenorm",
    )(x3, mod, g.reshape(1, d))


def _post_body(gate_row, nxt, n_experts, *refs):
    x_ref, y_ref, mod_ref, gpost_ref = refs[:4]
    pos = 4
    if nxt is not None:
        modn_ref, gpre_ref = refs[pos:pos + 2]
        pos += 2
    if n_experts:
        router_ref = refs[pos]
        pos += 1
    xo_ref = refs[pos]
    pos += 1
    xn = x_ref[...] + mod_ref[gate_row:gate_row + 1, :] * _rms(y_ref[...], gpost_ref[...])
    xo_ref[...] = xn
    if nxt is not None:
        shift_row, scale_row = nxt
        h = _rms(xn, gpre_ref[...]) * (1.0 + modn_ref[scale_row:scale_row + 1, :]) \
            + modn_ref[shift_row:shift_row + 1, :]
        refs[pos][...] = h.astype(BF16)
        pos += 1
        if n_experts:
            refs[pos][...] = _route(h, router_ref, n_experts)


def _postnorm(x3, y3, mod, g_post, gate_row, *, nxt=None, mod_next=None, g_pre=None,
              router_t=None, tr=256):
    b, s, d = x3.shape
    tr = min(tr, s)
    n_experts = 0 if router_t is None else router_t.shape[0]
    row = lambda bi, i: (bi, i, 0)
    mod_spec = pl.BlockSpec((None, N_MOD, d), lambda bi, i: (bi, 0, 0))
    in_specs = [pl.BlockSpec((None, tr, d), row), pl.BlockSpec((None, tr, d), row), mod_spec,
                pl.BlockSpec((1, d), lambda bi, i: (0, 0))]
    args = [x3, y3, mod, g_post.reshape(1, d)]
    out_shape = [jax.ShapeDtypeStruct((b, s, d), F32)]
    out_specs = [pl.BlockSpec((None, tr, d), row)]
    if nxt is not None:
        in_specs += [mod_spec, pl.BlockSpec((1, d), lambda bi, i: (0, 0))]
        args += [mod_next, g_pre.reshape(1, d)]
        out_shape.append(jax.ShapeDtypeStruct((b, s, d), BF16))
        out_specs.append(pl.BlockSpec((None, tr, d), row))
    if n_experts:
        in_specs.append(pl.BlockSpec((n_experts, d), lambda bi, i: (0, 0)))
        args.append(router_t)
        out_shape.append(jax.ShapeDtypeStruct((b, s, LANE), F32))
        out_specs.append(pl.BlockSpec((None, tr, LANE), row))
    vmem = 2 * 3 * tr * d * 4 + 2 * tr * d * 2 + 6 * tr * d * 4
    return pl.pallas_call(
        functools.partial(_post_body, gate_row, nxt, n_experts),
        out_shape=out_shape,
        grid=(b, s // tr),
        in_specs=in_specs,
        out_specs=out_specs,
        compiler_params=_params(("arbitrary", "arbitrary"), vmem),
        name="postnorm",
    )(*args)


def _lat_body(alat_ref, pik_ref, g_ref, lat_ref, latt_ref, ikt_ref):
    lat = _rms(alat_ref[...], g_ref[...])
    lat_ref[...] = lat.astype(BF16)
    latt_ref[...] = lat.T.astype(BF16)
    ikt_ref[...] = pik_ref[...].T.astype(BF16)


def _latnorm(pa3, pik3, g_lat, lat_col, n_lat, tr=512):
    b, s, _ = pa3.shape
    tr = min(tr, s)
    assert lat_col % n_lat == 0
    vmem = 2 * tr * (n_lat + LANE) * 4 + 4 * tr * (n_lat + LANE) * 2 + 4 * tr * n_lat * 4
    return pl.pallas_call(
        _lat_body,
        out_shape=[jax.ShapeDtypeStruct((b, s, n_lat), BF16),
                   jax.ShapeDtypeStruct((b, n_lat, s), BF16),
                   jax.ShapeDtypeStruct((b, LANE, s), BF16)],
        grid=(b, s // tr),
        in_specs=[pl.BlockSpec((None, tr, n_lat), lambda bi, i: (bi, i, lat_col // n_lat)),
                  pl.BlockSpec((None, tr, LANE), lambda bi, i: (bi, i, 0)),
                  pl.BlockSpec((1, n_lat), lambda bi, i: (0, 0))],
        out_specs=[pl.BlockSpec((None, tr, n_lat), lambda bi, i: (bi, i, 0)),
                   pl.BlockSpec((None, n_lat, tr), lambda bi, i: (bi, 0, i)),
                   pl.BlockSpec((None, LANE, tr), lambda bi, i: (bi, 0, i))],
        compiler_params=_params(("arbitrary", "arbitrary"), vmem),
        name="latnorm",
    )(pa3, pik3, g_lat.reshape(1, n_lat))


def _dsa_body(topk, q_ref, iq_ref, iw_ref, ikt_ref, lat_ref, latt_ref, wuk_ref, wuv_ref,
              bias_ref, o_ref, madd_ref):
    i = pl.program_id(1)
    seq = lat_ref.shape[0]
    n_blk = seq // A_BLOCK
    n_heads = wuk_ref.shape[0]
    int_min = jnp.int32(-2 ** 31)
    qpos = i * A_BLOCK + lax.broadcasted_iota(jnp.int32, (A_BLOCK, 1), 0)
    kpos = lax.broadcasted_iota(jnp.int32, (A_BLOCK, seq), 1)

    iq = iq_ref[...].astype(BF16)
    iw = iw_ref[...]
    ikt = ikt_ref[0:IDX_DIM, :]
    score = jnp.zeros((A_BLOCK, seq), F32)
    for h in range(IDX_HEADS):
        l = jnp.dot(iq[:, h * IDX_DIM:(h + 1) * IDX_DIM], ikt, preferred_element_type=F32)
        score = score + iw[:, IDX_DIM + h:IDX_DIM + h + 1] * jnp.maximum(l, 0.0)

    score = jnp.where(score == 0.0, 0.0, score)
    key = pltpu.bitcast(score, jnp.int32)
    key = jnp.where(key < 0, key ^ jnp.int32(0x7FFFFFFF), key)
    key = jnp.where(kpos <= qpos, key, int_min)
    k_eff = jnp.minimum(qpos + 1, topk).astype(F32)

    def count(mask):
        return jnp.sum(jnp.where(mask, 1.0, 0.0), axis=1, keepdims=True)

    t0 = jnp.where(count(key >= 0) >= k_eff, jnp.int32(0), int_min)

    def value_step(it, t):
        c = t | (jnp.int32(1) << (30 - it))
        return jnp.where(count(key >= c) >= k_eff, c, t)

    t = lax.fori_loop(0, 31, value_step, t0)

    need = k_eff - count(key > t)
    eq = key == t
    idx_bits = max(1, (seq - 1).bit_length())

    def index_step(it, p):
        c = p | (jnp.int32(1) << (idx_bits - 1 - it))
        below = jnp.sum(jnp.where(eq, jnp.where(kpos < c, 1.0, 0.0), 0.0), axis=1, keepdims=True)
        return jnp.where(below < need, c, p)

    p = lax.fori_loop(0, idx_bits, index_step, jnp.zeros((A_BLOCK, 1), jnp.int32))
    tie_add = jnp.where(eq, jnp.where(kpos <= p, 0.0, NEG_INF), NEG_INF)
    madd_ref[...] = jnp.where(key > t, 0.0, tie_add)

    scale = HEAD_DIM ** -0.5
    for h in range(n_heads):
        cols = slice(h * HEAD_DIM, (h + 1) * HEAD_DIM)
        qa = jnp.dot(q_ref[:, cols].astype(BF16), wuk_ref[h],
                     preferred_element_type=F32).astype(BF16)
        lg = jnp.dot(qa, latt_ref[...], preferred_element_type=F32) * scale
        bias = jnp.concatenate([bias_ref[jnp.maximum(i - j, 0), h] for j in range(n_blk)], axis=1)
        lg = lg + bias + madd_ref[...]
        m = jnp.max(lg, axis=1, keepdims=True)
        e = jnp.exp(lg - m)
        den = jnp.sum(e, axis=1, keepdims=True)
        o_lat = jnp.dot(e.astype(BF16), lat_ref[...], preferred_element_type=F32) / den
        o_ref[:, cols] = jnp.dot(o_lat.astype(BF16), wuv_ref[h],
                                 preferred_element_type=F32).astype(o_ref.dtype)


def _dsa(pa3, pik3, lat, latt, ikt, wuk_t, wuv_t, bias_a, *, q_width, iq_col, iq_width):
    b, s, _ = pa3.shape
    n_lat = lat.shape[-1]
    n_heads = wuk_t.shape[0]
    n_blk = s // A_BLOCK
    topk = min(IDX_TOPK, s // 4)
    assert iq_col % iq_width == 0
    vmem = 2 * (A_BLOCK * (q_width + iq_width + LANE) * 4 + LANE * s * 2 + 2 * s * n_lat * 2
                + 2 * n_heads * HEAD_DIM * n_lat * 2 + n_heads * n_blk * A_BLOCK * A_BLOCK * 4
                + A_BLOCK * q_width * 2) + 10 * A_BLOCK * s * 4
    return pl.pallas_call(
        functools.partial(_dsa_body, topk),
        out_shape=jax.ShapeDtypeStruct((b, s, q_width), BF16),
        grid=(b, n_blk),
        in_specs=[pl.BlockSpec((None, A_BLOCK, q_width), lambda bi, i: (bi, i, 0)),
                  pl.BlockSpec((None, A_BLOCK, iq_width), lambda bi, i: (bi, i, iq_col // iq_width)),
                  pl.BlockSpec((None, A_BLOCK, LANE), lambda bi, i: (bi, i, 0)),
                  pl.BlockSpec((None, LANE, s), lambda bi, i: (bi, 0, 0)),
                  pl.BlockSpec((None, s, n_lat), lambda bi, i: (bi, 0, 0)),
                  pl.BlockSpec((None, n_lat, s), lambda bi, i: (bi, 0, 0)),
                  pl.BlockSpec(wuk_t.shape, lambda bi, i: (0, 0, 0)),
                  pl.BlockSpec(wuv_t.shape, lambda bi, i: (0, 0, 0)),
                  pl.BlockSpec(bias_a.shape, lambda bi, i: (0, 0, 0, 0))],
        out_specs=pl.BlockSpec((None, A_BLOCK, q_width), lambda bi, i: (bi, i, 0)),
        scratch_shapes=[pltpu.VMEM((A_BLOCK, s), F32)],
        compiler_params=_params(("arbitrary", "arbitrary"), vmem),
        name="dsa",
    )(pa3, pa3, pik3, ikt, lat, latt, wuk_t, wuv_t, bias_a)


def _dil_body(span, dil, hb, has_prev, *refs):
    if has_prev:
        q_ref, kp_ref, kc_ref, vp_ref, vc_ref, bias_ref, o_ref, lse_ref = refs
    else:
        q_ref, kc_ref, vc_ref, bias_ref, o_ref, lse_ref = refs
    i = pl.program_id(1)
    hblk = pl.program_id(2)
    n_keys = 2 * B_BLOCK if has_prev else B_BLOCK
    key0 = 0 if has_prev else B_BLOCK
    qi = lax.broadcasted_iota(jnp.int32, (B_BLOCK, n_keys), 0)
    kj = lax.broadcasted_iota(jnp.int32, (B_BLOCK, n_keys), 1) + key0
    sub_dist = qi + B_BLOCK - kj
    in_band = jnp.where(sub_dist <= span,
                        jnp.where((i - 1) * B_BLOCK + kj >= 0, 0.0, NEG_INF), NEG_INF)
    madd = jnp.where(sub_dist >= 0, in_band, NEG_INF)
    lane = lax.broadcasted_iota(jnp.int32, (B_BLOCK, LANE), 1)
    scale = HEAD_DIM ** -0.5

    @pl.when(hblk == 0)
    def _():
        lse_ref[...] = jnp.zeros_like(lse_ref)

    def one_subsequence(r, carry):
        rows = pl.ds(r, B_BLOCK, stride=dil) if dil > 1 else slice(None)
        lse_acc = lse_ref[rows, :]
        for h in range(hb):
            cols = slice(h * HEAD_DIM, (h + 1) * HEAD_DIM)
            q = q_ref[rows, cols].astype(BF16)
            if has_prev:
                k2 = jnp.concatenate([kp_ref[rows, cols], kc_ref[rows, cols]], axis=0).astype(BF16)
                v2 = jnp.concatenate([vp_ref[rows, cols], vc_ref[rows, cols]], axis=0).astype(BF16)
            else:
                k2 = kc_ref[rows, cols].astype(BF16)
                v2 = vc_ref[rows, cols].astype(BF16)
            lg = lax.dot_general(q, k2, (((1,), (1,)), ((), ())), preferred_element_type=F32)
            lg = lg * scale + bias_ref[h, :, key0:] + madd
            m = jnp.max(lg, axis=1, keepdims=True)
            e = jnp.exp(lg - m)
            den = jnp.sum(e, axis=1, keepdims=True)
            o_ref[rows, cols] = jnp.dot((e / den).astype(BF16), v2, preferred_element_type=F32)
            lse_acc = jnp.where(lane == hblk * hb + h, m + jnp.log(den), lse_acc)
        lse_ref[rows, :] = lse_acc
        return carry

    if dil > 1:
        lax.fori_loop(0, dil, one_subsequence, 0)
    else:
        one_subsequence(0, 0)


def _dilated(rest3, bias_b, g, n_groups, window, dil, width):
    b, s, w_all = rest3.shape
    n_heads = width // HEAD_DIM
    span = window // dil
    n_sub = s // dil
    assert span == B_BLOCK and n_sub % B_BLOCK == 0 and w_all % width == 0
    nb = n_sub // B_BLOCK
    rows = B_BLOCK * dil
    hb = n_heads if dil == 1 else 1
    assert HEAD_DIM == LANE
    n_hblk = n_heads // hb
    has_prev = nb > 1
    cur = lambda slot: (lambda bi, i, hk: (bi, i, (slot * n_groups + g) * n_hblk + hk))
    prev = lambda slot: (lambda bi, i, hk: (bi, jnp.maximum(i - 1, 0),
                                            (slot * n_groups + g) * n_hblk + hk))
    blk = (None, rows, hb * HEAD_DIM)
    in_specs = [pl.BlockSpec(blk, cur(0))]
    for slot in (1, 2):
        if has_prev:
            in_specs.append(pl.BlockSpec(blk, prev(slot)))
        in_specs.append(pl.BlockSpec(blk, cur(slot)))
    in_specs.append(pl.BlockSpec((None, hb, B_BLOCK, 2 * B_BLOCK), lambda bi, i, hk: (g, hk, 0, 0)))
    vmem = 2 * (len(in_specs) * rows * hb * HEAD_DIM + rows * LANE) * 4 \
        + 2 * hb * B_BLOCK * 2 * B_BLOCK * 4 + (4 << 20)
    o, lse = pl.pallas_call(
        functools.partial(_dil_body, span, dil, hb, has_prev),
        out_shape=[jax.ShapeDtypeStruct((b, s, width), F32),
                   jax.ShapeDtypeStruct((b, s, LANE), F32)],
        grid=(b, nb, n_hblk),
        in_specs=in_specs,
        out_specs=[pl.BlockSpec(blk, lambda bi, i, hk: (bi, i, hk)),
                   pl.BlockSpec((None, rows, LANE), lambda bi, i, hk: (bi, i, 0))],
        compiler_params=_params(("arbitrary", "arbitrary", "arbitrary"), vmem),
        name=f"dilated{g}",
    )(*([rest3] * (len(in_specs) - 1)), bias_b)
    return o.reshape(b * s, width), lse.reshape(b * s, LANE)


def _mix_body(n_groups, n_heads, *refs):
    o_refs = refs[:n_groups]
    l_refs = refs[n_groups:2 * n_groups]
    out_ref = refs[2 * n_groups]
    ls = [r[...] for r in l_refs]
    m = functools.reduce(jnp.maximum, ls)
    es = [jnp.exp(l - m) for l in ls]
    tot = functools.reduce(jnp.add, es)
    ws = [e / tot for e in es]
    for h in range(n_heads):
        cols = slice(h * HEAD_DIM, (h + 1) * HEAD_DIM)
        acc = ws[0][:, h:h + 1] * o_refs[0][:, cols]
        for gi in range(1, n_groups):
            acc = acc + ws[gi][:, h:h + 1] * o_refs[gi][:, cols]
        out_ref[:, cols] = acc.astype(out_ref.dtype)


def _mixture(outs, lses, tr=512):
    t, width = outs[0].shape
    tr = min(tr, t)
    n_groups = len(outs)
    vmem = 2 * n_groups * tr * (width + LANE) * 4 + 2 * tr * width * 2 + 4 * tr * width * 4
    return pl.pallas_call(
        functools.partial(_mix_body, n_groups, width // HEAD_DIM),
        out_shape=jax.ShapeDtypeStruct((t, width), BF16),
        grid=(t // tr,),
        in_specs=[pl.BlockSpec((tr, width), lambda i: (i, 0))] * n_groups
        + [pl.BlockSpec((tr, LANE), lambda i: (i, 0))] * n_groups,
        out_specs=pl.BlockSpec((tr, width), lambda i: (i, 0)),
        compiler_params=_params(("arbitrary",), vmem),
        name="mixture",
    )(*outs, *lses)


def _gmlp_body(u_ref, v_ref, g_ref, b_ref, ws_ref, bs_ref, o_ref):
    u = jax.nn.gelu(u_ref[...])
    v = jax.nn.gelu(v_ref[...])
    mu = jnp.mean(v, axis=-1, keepdims=True)
    var = jnp.mean(jnp.square(v - mu), axis=-1, keepdims=True)
    v = (v - mu) * lax.rsqrt(var + EPS) * g_ref[...] + b_ref[...]
    ti = lax.broadcasted_iota(jnp.int32, (C_CHUNK, C_CHUNK), 0)
    si = lax.broadcasted_iota(jnp.int32, (C_CHUNK, C_CHUNK), 1)
    causal = si <= ti
    gw = u.shape[1] // C_GROUPS
    for g in range(C_GROUPS):
        cols = slice(g * gw, (g + 1) * gw)
        w = jnp.where(causal, ws_ref[g], 0.0).astype(BF16)
        mixed = jnp.dot(w, v[:, cols].astype(BF16), preferred_element_type=F32) + bs_ref[:, g:g + 1]
        o_ref[:, cols] = (u[:, cols] * mixed).astype(o_ref.dtype)


def _gmlp(rest, u_col, width, ln_g, ln_b, w_s, b_s_t):
    t = rest.shape[0]
    assert u_col % width == 0 and (width // C_GROUPS) % LANE == 0
    vmem = 2 * 2 * C_CHUNK * width * 4 + 2 * C_CHUNK * width * 2 + 2 * C_GROUPS * C_CHUNK * C_CHUNK * 4 \
        + 8 * C_CHUNK * width * 4
    return pl.pallas_call(
        _gmlp_body,
        out_shape=jax.ShapeDtypeStruct((t, width), BF16),
        grid=(t // C_CHUNK,),
        in_specs=[pl.BlockSpec((C_CHUNK, width), lambda i: (i, u_col // width)),
                  pl.BlockSpec((C_CHUNK, width), lambda i: (i, u_col // width + 1)),
                  pl.BlockSpec((1, width), lambda i: (0, 0)),
                  pl.BlockSpec((1, width), lambda i: (0, 0)),
                  pl.BlockSpec(w_s.shape, lambda i: (0, 0, 0)),
                  pl.BlockSpec(b_s_t.shape, lambda i: (0, 0))],
        out_specs=pl.BlockSpec((C_CHUNK, width), lambda i: (i, 0)),
        compiler_params=_params(("arbitrary",), vmem),
        name="gmlp",
    )(rest, rest, ln_g.reshape(1, width), ln_b.reshape(1, width), w_s, b_s_t)


def _conv_body(h_ref, bg_ref, cg_ref, w_ref, o_ref, carry_ref):
    @pl.when(pl.program_id(1) == 0)
    def _():
        carry_ref[...] = jnp.zeros_like(carry_ref)

    ch = cg_ref[...] * h_ref[...]
    rows = ch.shape[0]
    row = lax.broadcasted_iota(jnp.int32, ch.shape, 0)
    pm1 = carry_ref[7:8, :]
    pm2 = carry_ref[6:7, :]
    s1 = jnp.where(row == 0, pm1, pltpu.roll(ch, 1, 0))
    s2 = jnp.where(row == 0, pm2, jnp.where(row == 1, pm1, pltpu.roll(ch, 2, 0)))
    z = w_ref[0:1, :] * s2 + w_ref[1:2, :] * s1 + w_ref[2:3, :] * ch
    o_ref[...] = (bg_ref[...] * z).astype(o_ref.dtype)
    carry_ref[...] = ch[rows - 8:, :]


def _short_conv(rest3, h_col, width, conv_w, tr=256):
    b, s, _ = rest3.shape
    tr = min(tr, s)
    assert h_col % width == 0 and conv_w.shape[0] == D_CONV_WIDTH
    cb = h_col // width
    vmem = 2 * 3 * tr * width * 4 + 2 * tr * width * 2 + 6 * tr * width * 4
    return pl.pallas_call(
        _conv_body,
        out_shape=jax.ShapeDtypeStruct((b, s, width), BF16),
        grid=(b, s // tr),
        in_specs=[pl.BlockSpec((None, tr, width), lambda bi, i: (bi, i, cb)),
                  pl.BlockSpec((None, tr, width), lambda bi, i: (bi, i, cb + 1)),
                  pl.BlockSpec((None, tr, width), lambda bi, i: (bi, i, cb + 2)),
                  pl.BlockSpec(conv_w.shape, lambda bi, i: (0, 0))],
        out_specs=pl.BlockSpec((None, tr, width), lambda bi, i: (bi, i, 0)),
        scratch_shapes=[pltpu.VMEM((8, width), F32)],
        compiler_params=_params(("arbitrary", "arbitrary"), vmem),
        name="short_conv",
    )(rest3, rest3, rest3, conv_w)


def _merge_body(chunk, *refs):
    br_refs = refs[:N_BRANCH]
    gate_refs = refs[N_BRANCH:2 * N_BRANCH]
    w_ref, o_ref, wb_ref = refs[2 * N_BRANCH:]

    @pl.when(pl.program_id(1) == 0)
    def _():
        for n in range(N_BRANCH):
            _cast_rows(lambda rows, n=n: w_ref[n, rows, :], wb_ref.at[n], w_ref.shape[1], chunk)

    acc = None
    for n in range(N_BRANCH):
        proj = jnp.dot(br_refs[n][...], wb_ref[n], preferred_element_type=F32)
        term = jax.nn.sigmoid(gate_refs[n][...]) * proj
        acc = term if acc is None else acc + term
    o_ref[...] = acc.astype(o_ref.dtype)


def _merge(branches, rest, gate_col, w_branch, layer, tn=512, tm=512):
    t, width = branches[0].shape
    d = w_branch.shape[-1]
    tn = min(tn, d)
    tm = min(tm, t)
    assert d % tn == 0 and t % tm == 0 and gate_col % tn == 0
    chunk = _pick_tile(width, (256, 128, 8))
    gate_spec = lambda n: pl.BlockSpec((tm, tn), lambda j, i: (i, (gate_col + n * d) // tn + j))
    vmem = 2 * N_BRANCH * (tm * width * 2 + tm * tn * 4 + width * tn * 4) + N_BRANCH * width * tn * 2 \
        + 2 * tm * tn * 2 + 4 * tm * tn * 4
    return pl.pallas_call(
        functools.partial(_merge_body, chunk),
        out_shape=jax.ShapeDtypeStruct((t, d), BF16),
        grid=(d // tn, t // tm),
        in_specs=[pl.BlockSpec((tm, width), lambda j, i: (i, 0))] * N_BRANCH
        + [gate_spec(n) for n in range(N_BRANCH)]
        + [pl.BlockSpec((None, N_BRANCH, width, tn), lambda j, i: (layer, 0, 0, j))],
        out_specs=pl.BlockSpec((tm, tn), lambda j, i: (i, j)),
        scratch_shapes=[pltpu.VMEM((N_BRANCH, width, tn), BF16)],
        compiler_params=_params(("arbitrary", "arbitrary"), vmem),
        name="merge",
    )(*branches, *([rest] * N_BRANCH), w_branch)


def _up_body(chunk, h_ref, wg_ref, wu_ref, o_ref, wgb_ref, wub_ref):
    @pl.when(pl.program_id(1) == 0)
    def _():
        _cast_rows(lambda rows: wg_ref[rows, :], wgb_ref, wg_ref.shape[0], chunk)
        _cast_rows(lambda rows: wu_ref[rows, :], wub_ref, wu_ref.shape[0], chunk)

    h = h_ref[...]
    g = jnp.dot(h, wgb_ref[...], preferred_element_type=F32)
    u = jnp.dot(h, wub_ref[...], preferred_element_type=F32)
    o_ref[...] = (g * jax.nn.sigmoid(g) * u).astype(o_ref.dtype)


def _swiglu_up(h, wg, wu, tm=512):
    t, k = h.shape
    n_e, _, f = wg.shape
    tn = _pick_tile(f, (256, 128))
    tm = min(tm, t)
    nj = f // tn
    chunk = _pick_tile(k, (256, 128, 8))
    w_spec = pl.BlockSpec((None, k, tn), lambda j, i: (j // nj, 0, j % nj))
    vmem = 2 * tm * k * 2 + 4 * k * tn * 4 + 2 * k * tn * 2 + 2 * tm * tn * 2 + 4 * tm * tn * 4
    return pl.pallas_call(
        functools.partial(_up_body, chunk),
        out_shape=jax.ShapeDtypeStruct((t, n_e * f), BF16),
        grid=(n_e * nj, t // tm),
        in_specs=[pl.BlockSpec((tm, k), lambda j, i: (i, 0)), w_spec, w_spec],
        out_specs=pl.BlockSpec((tm, tn), lambda j, i: (i, j)),
        scratch_shapes=[pltpu.VMEM((k, tn), BF16), pltpu.VMEM((k, tn), BF16)],
        compiler_params=_params(("arbitrary", "arbitrary"), vmem),
        name="swiglu_up",
    )(h, wg, wu)


def _down_body(scaled, tm, chunk, *refs):
    if scaled:
        a_ref, w_ref, c_ref, o_ref, wb_ref = refs
    else:
        a_ref, w_ref, o_ref, wb_ref = refs
    kk = pl.program_id(1)
    i = pl.program_id(2)

    @pl.when(i == 0)
    def _():
        _cast_rows(lambda rows: w_ref[rows, :], wb_ref, w_ref.shape[0], chunk)

    part = jnp.dot(a_ref[...], wb_ref[...], preferred_element_type=F32)
    if scaled:
        lane = lax.broadcasted_iota(jnp.int32, c_ref.shape, 1)
        part = part * jnp.sum(jnp.where(lane == kk, c_ref[...], 0.0), axis=1, keepdims=True)
    rows = pl.ds(pl.multiple_of(i * tm, tm), tm)

    @pl.when(kk == 0)
    def _():
        o_ref[rows, :] = part

    @pl.when(kk > 0)
    def _():
        o_ref[rows, :] += part


def _swiglu_down(a, wd, tk, comb=None, tn=256, tm=512):
    t, k = a.shape
    n = wd.shape[1]
    tn = min(tn, n)
    tm = min(tm, t)
    assert k % tk == 0 and n % tn == 0 and t % tm == 0
    chunk = _pick_tile(tk, (256, 128, 8))
    in_specs = [pl.BlockSpec((tm, tk), lambda j, kk, i: (i, kk)),
                pl.BlockSpec((tk, tn), lambda j, kk, i: (kk, j))]
    args = [a, wd]
    if comb is not None:
        in_specs.append(pl.BlockSpec((tm, LANE), lambda j, kk, i: (i, 0)))
        args.append(comb)
    vmem = 2 * tm * tk * 2 + 2 * tk * tn * 4 + tk * tn * 2 + 2 * t * tn * 4 + 4 * tm * tn * 4
    return pl.pallas_call(
        functools.partial(_down_body, comb is not None, tm, chunk),
        out_shape=jax.ShapeDtypeStruct((t, n), F32),
        grid=(n // tn, k // tk, t // tm),
        in_specs=in_specs,
        out_specs=pl.BlockSpec((t, tn), lambda j, kk, i: (0, j)),
        scratch_shapes=[pltpu.VMEM((tk, tn), BF16)],
        compiler_params=_params(("arbitrary", "arbitrary", "arbitrary"), vmem),
        name="swiglu_down",
    )(*args)


def _rel_bucket(dist):
    max_exact = REL_BUCKETS // 2
    d = jnp.maximum(dist, 0)
    df = jnp.maximum(d, max_exact).astype(F32)
    large = max_exact + (jnp.log(df / max_exact) / math.log(REL_MAX_DIST / max_exact)
                         * (REL_BUCKETS - max_exact)).astype(jnp.int32)
    large = jnp.minimum(large, REL_BUCKETS - 1)
    return jnp.where(d < max_exact, d, large)


def _bias_body(n_heads, head0, head_stride, bucket_ref, tab_ref, o_ref):
    bucket = bucket_ref[...]
    base = head0 + head_stride * pl.program_id(0)
    for h in range(n_heads):
        acc = jnp.zeros(bucket.shape, F32)
        for b in range(REL_BUCKETS):
            acc = jnp.where(bucket == b, tab_ref[b, base + h], acc)
        o_ref[h] = acc


def _bias_tiles(rel_bias, buckets, n_heads, head0, head_stride):
    n_tiles, q, k = buckets.shape
    return pl.pallas_call(
        functools.partial(_bias_body, n_heads, head0, head_stride),
        out_shape=jax.ShapeDtypeStruct((n_tiles, n_heads, q, k), F32),
        grid=(n_tiles,),
        in_specs=[pl.BlockSpec((None, q, k), lambda t: (t, 0, 0)),
                  pl.BlockSpec(memory_space=pltpu.SMEM)],
        out_specs=pl.BlockSpec((None, n_heads, q, k), lambda t: (t, 0, 0, 0)),
        compiler_params=_params(("arbitrary",), 4 * (n_heads + 1) * q * k * 4),
        name="bias_tiles",
    )(buckets, rel_bias)


def _bias_tiles_a(rel_bias, n_heads, seq):
    n_blk = seq // A_BLOCK
    qi = jnp.arange(A_BLOCK)[:, None]
    kj = jnp.arange(A_BLOCK)[None, :]
    dist = jnp.arange(n_blk)[:, None, None] * A_BLOCK + (qi - kj)[None]
    return _bias_tiles(rel_bias, _rel_bucket(dist), n_heads, 0, 0)


def _bias_tiles_b(rel_bias, head0, n_heads):
    qi = jnp.arange(B_BLOCK)[:, None]
    kj = jnp.arange(2 * B_BLOCK)[None, :]
    sub_dist = qi + B_BLOCK - kj
    buckets = jnp.stack([_rel_bucket(sub_dist * dil) for _, dil in B_GROUPS])
    return _bias_tiles(rel_bias, buckets, n_heads, head0, n_heads)


def _hybrid_mixer(h, bsz, seq, layer, w_in, lat_g, w_uk, w_uv, c_ln_g, c_ln_b, c_w_s, c_b_s,
                  d_conv, w_branch, w_out, bias_a, bias_b):
    t, d = h.shape
    width = d // N_BRANCH
    n_lat = w_uk.shape[0]
    a_heads = width // HEAD_DIM
    n_groups = len(B_GROUPS)
    iq_width = IDX_HEADS * IDX_DIM
    lat_col = width
    iq_col = lat_col + n_lat
    ik_col = iq_col + iq_width
    front = ik_col
    shift = IDX_DIM + IDX_HEADS
    qkv_w = 3 * n_groups * width
    rest_w = qkv_w + 2 * width + 3 * width + N_BRANCH * d
    assert front % 512 == 0 and w_in.shape[2] == front + shift + rest_w and shift < LANE

    pa = _matmul(h, w_in, layer, col0=0, n_out=front, name="proj_front")
    pik = _matmul(h, w_in, layer, col0=front, n_out=LANE, name="proj_index_key")
    rest = _matmul(h, w_in, layer, col0=front, n_out=rest_w, shift=shift, name="proj_rest")

    pa3 = pa.reshape(bsz, seq, front)
    pik3 = pik.reshape(bsz, seq, LANE)
    rest3 = rest.reshape(bsz, seq, rest_w)

    lat, latt, ikt = _latnorm(pa3, pik3, lat_g, lat_col, n_lat)
    wuk_t = jnp.transpose(w_uk, (1, 2, 0)).astype(BF16)
    wuv_t = jnp.transpose(w_uv, (1, 0, 2)).astype(BF16)
    o_a = _dsa(pa3, pik3, lat, latt, ikt, wuk_t, wuv_t, bias_a,
               q_width=width, iq_col=iq_col, iq_width=iq_width).reshape(t, width)

    outs, lses = [], []
    for g, (window, dil) in enumerate(B_GROUPS):
        o, l = _dilated(rest3, bias_b, g, n_groups, window, dil, width)
        outs.append(o)
        lses.append(l)
    o_b = _mixture(outs, lses)

    o_c = _gmlp(rest, qkv_w, width, c_ln_g, c_ln_b, c_w_s, c_b_s.T)
    o_d = _short_conv(rest3, qkv_w + 2 * width, width, d_conv).reshape(t, width)

    mixed = _merge([o_a, o_b, o_c, o_d], rest, qkv_w + 5 * width, w_branch, layer)
    return _matmul(mixed, w_out, layer, col0=0, n_out=d, name="proj_out")


def kernel(x, c, w_ada, b_ada, ada_table, rel_bias, norm_pre_mix, norm_post_mix, norm_pre_ffn,
           norm_post_ffn, w_in, a_lat_norm, a_w_uk, a_w_uv, c_ln_g, c_ln_b, c_w_s, c_b_s, d_conv,
           w_branch, w_out, ffn_w_gate, ffn_w_up, ffn_w_down, moe_router, moe_w_gate, moe_w_up,
           moe_w_down):
    bsz, seq, d = x.shape
    depth = w_in.shape[0]
    t = bsz * seq

    c_pad = jnp.pad(c, ((0, 16 - bsz % 16 if bsz % 16 else 0), (0, 0)))
    mod_shared = _ada(c_pad, w_ada, b_ada)[:bsz].reshape(bsz, N_MOD, d)
    mods = [mod_shared + ada_table[layer] for layer in range(depth)]

    width = d // N_BRANCH
    bias_a = _bias_tiles_a(rel_bias, width // HEAD_DIM, seq)
    bias_b = _bias_tiles_b(rel_bias, width // HEAD_DIM, width // HEAD_DIM)

    h = _prenorm(x, mods[0], norm_pre_mix[0])
    for layer in range(depth):
        mod = mods[layer]
        y = _hybrid_mixer(h.reshape(t, d), bsz, seq, layer, w_in, a_lat_norm[layer],
                          a_w_uk[layer], a_w_uv[layer], c_ln_g[layer], c_ln_b[layer],
                          c_w_s[layer], c_b_s[layer], d_conv[layer], w_branch, w_out,
                          bias_a, bias_b)
        j = layer // 2
        dense = layer % 2 == 0
        res = _postnorm(x, y.reshape(bsz, seq, d), mod, norm_post_mix[layer], 2, nxt=(3, 4),
                        mod_next=mod, g_pre=norm_pre_ffn[layer],
                        router_t=None if dense else moe_router[j].T)
        if dense:
            x, h = res
            a = _swiglu_up(h.reshape(t, d), ffn_w_gate[j][None], ffn_w_up[j][None])
            f = ffn_w_down.shape[1]
            tk = f // 2 if (f // 2) % LANE == 0 and f > 4096 else f
            y = _swiglu_down(a, ffn_w_down[j], tk)
        else:
            x, h, comb = res
            n_e, f, _ = moe_w_down[j].shape
            a = _swiglu_up(h.reshape(t, d), moe_w_gate[j], moe_w_up[j])
            y = _swiglu_down(a, moe_w_down[j].reshape(n_e * f, d), f, comb=comb.reshape(t, LANE))
        y3 = y.reshape(bsz, seq, d)
        if layer + 1 < depth:
            x, h = _postnorm(x, y3, mod, norm_post_ffn[layer], 5, nxt=(0, 1),
                             mod_next=mods[layer + 1], g_pre=norm_pre_mix[layer + 1])
        else:
            (x,) = _postnorm(x, y3, mod, norm_post_ffn[layer], 5)
    return x
```

```python
import functools
import math

import jax
import jax.numpy as jnp
from jax import lax
from jax.experimental import pallas as pl
from jax.experimental.pallas import tpu as pltpu

F32 = jnp.float32
BF16 = jnp.bfloat16

LANE = 128
V7X_VMEM_BYTES = 64 * 1024 * 1024
VMEM_CAP = V7X_VMEM_BYTES - 8 * 1024 * 1024

HEAD_DIM = 128
N_BRANCH = 4
IDX_HEADS = 8
IDX_DIM = 64
IDX_TOPK = 256
A_BLOCK = 128
B_GROUPS = ((128, 1), (512, 4), (2048, 16))
B_BLOCK = 128
C_CHUNK = 128
C_GROUPS = 8
D_CONV_WIDTH = 3
REL_BUCKETS = 32
REL_MAX_DIST = 2048
TOP_K = 2
N_MOD = 6
EPS = 1e-6
NEG_INF = float("-inf")


def _params(semantics, vmem_bytes):
    limit = int(min(VMEM_CAP, max(vmem_bytes * 5 // 4 + (4 << 20), 16 << 20)))
    return pltpu.CompilerParams(dimension_semantics=semantics, vmem_limit_bytes=limit)


def _pick_tile(n, candidates):
    for c in candidates:
        if n % c == 0:
            return c
    raise ValueError(f"no tile in {candidates} divides {n}")


def _cast_rows(src_fn, dst_ref, n_rows, chunk):
    def body(r, carry):
        rows = pl.ds(pl.multiple_of(r * chunk, chunk), chunk)
        dst_ref[rows, :] = src_fn(rows).astype(BF16)
        return carry
    lax.fori_loop(0, n_rows // chunk, body, 0)


def _mm_body(chunk, lhs_ref, w_ref, o_ref, wb_ref):
    @pl.when(pl.program_id(1) == 0)
    def _():
        _cast_rows(lambda rows: w_ref[rows, :], wb_ref, w_ref.shape[0], chunk)

    o_ref[...] = jnp.dot(lhs_ref[...], wb_ref[...],
                         preferred_element_type=F32).astype(o_ref.dtype)


def _matmul(lhs, w, layer, *, tn=512, tm=512, out_dtype=F32, name="mm"):
    m, k = lhs.shape
    n_out = w.shape[2]
    tn = min(tn, n_out)
    tm = min(tm, m)
    assert n_out % tn == 0 and m % tm == 0 and tn % LANE == 0
    chunk = _pick_tile(k, (256, 128, 64, 8))
    osz = jnp.dtype(out_dtype).itemsize
    vmem = 2 * tm * k * 2 + 2 * k * tn * 4 + k * tn * 2 + 2 * tm * tn * osz + tm * tn * 4 \
        + chunk * tn * 8
    return pl.pallas_call(
        functools.partial(_mm_body, chunk),
        out_shape=jax.ShapeDtypeStruct((m, n_out), out_dtype),
        grid=(n_out // tn, m // tm),
        in_specs=[pl.BlockSpec((tm, k), lambda j, i: (i, 0)),
                  pl.BlockSpec((None, k, tn), lambda j, i: (layer, 0, j))],
        out_specs=pl.BlockSpec((tm, tn), lambda j, i: (i, j)),
        scratch_shapes=[pltpu.VMEM((k, tn), BF16)],
        compiler_params=_params(("arbitrary", "arbitrary"), vmem),
        name=name,
    )(lhs, w)


def _mm_t_body(pre_transpose, chunk, lhs_ref, w_ref, o_ref, wb_ref):
    k = w_ref.shape[1]

    @pl.when(pl.program_id(1) == 0)
    def _():
        if pre_transpose:
            for c in range(k // chunk):
                cols = slice(c * chunk, (c + 1) * chunk)
                wb_ref[cols, :] = w_ref[:, cols].T.astype(BF16)
        else:
            _cast_rows(lambda rows: w_ref[rows, :], wb_ref, w_ref.shape[0], chunk)

    if pre_transpose:
        acc = jnp.dot(lhs_ref[...], wb_ref[...], preferred_element_type=F32)
    else:
        acc = lax.dot_general(lhs_ref[...], wb_ref[...], (((1,), (1,)), ((), ())),
                              preferred_element_type=F32)
    o_ref[...] = acc.astype(o_ref.dtype)


def _matmul_t(lhs, w_t, layer, *, row0, n_out, tn=512, tm=512, pre_transpose=False,
              out_dtype=F32, name="mm_t"):
    m, k = lhs.shape
    tn = min(tn, n_out)
    tm = min(tm, m)
    assert n_out % tn == 0 and m % tm == 0 and row0 % 8 == 0 and tn % LANE == 0
    chunk = _pick_tile(k if pre_transpose else tn, (256, 128))
    osz = jnp.dtype(out_dtype).itemsize
    vmem = 2 * tm * k * 2 + 2 * k * tn * 4 + k * tn * 2 + 2 * tm * tn * osz + tm * tn * 4 \
        + 4 * chunk * max(tn, k) * 4
    return pl.pallas_call(
        functools.partial(_mm_t_body, pre_transpose, chunk),
        out_shape=jax.ShapeDtypeStruct((m, n_out), out_dtype),
        grid=(n_out // tn, m // tm),
        in_specs=[pl.BlockSpec((tm, k), lambda j, i: (i, 0)),
                  pl.BlockSpec((None, pl.Element(tn), pl.Element(k)),
                               lambda j, i: (layer, pl.multiple_of(row0 + j * tn, 8), 0))],
        out_specs=pl.BlockSpec((tm, tn), lambda j, i: (i, j)),
        scratch_shapes=[pltpu.VMEM((k, tn) if pre_transpose else (tn, k), BF16)],
        compiler_params=_params(("arbitrary", "arbitrary"), vmem),
        name=name,
    )(lhs, w_t)


def _ada_body(c_ref, w_ref, b_ref, o_ref):
    c = c_ref[...]
    a = (c * jax.nn.sigmoid(c)).astype(BF16)
    o_ref[...] = jnp.dot(a, w_ref[...].astype(BF16), preferred_element_type=F32) + b_ref[...]


def _ada(c_pad, w_ada, b_ada, tn=512):
    m, k = c_pad.shape
    n = w_ada.shape[1]
    tn = _pick_tile(n, (tn, 256, 128))
    vmem = 2 * k * tn * 4 + k * tn * 2 + 4 * m * k * 4
    return pl.pallas_call(
        _ada_body,
        out_shape=jax.ShapeDtypeStruct((m, n), F32),
        grid=(n // tn,),
        in_specs=[pl.BlockSpec((m, k), lambda j: (0, 0)),
                  pl.BlockSpec((k, tn), lambda j: (0, j)),
                  pl.BlockSpec((1, tn), lambda j: (0, j))],
        out_specs=pl.BlockSpec((m, tn), lambda j: (0, j)),
        compiler_params=_params(("arbitrary",), vmem),
        name="ada",
    )(c_pad, w_ada, b_ada.reshape(1, n))


def _rms(x, g):
    return x * lax.rsqrt(jnp.mean(x * x, axis=-1, keepdims=True) + EPS) * g


def _route(h, router_ref, n_experts):
    rows = h.shape[0]
    lane = lax.broadcasted_iota(jnp.int32, (rows, LANE), 1)
    logits = jnp.full((rows, LANE), NEG_INF, F32)
    for e in range(n_experts):
        le = jnp.sum(h * router_ref[e:e + 1, :], axis=-1, keepdims=True)
        logits = jnp.where(lane == e, le, logits)
    m1 = jnp.max(logits, axis=-1, keepdims=True)
    i1 = jnp.min(jnp.where(logits == m1, lane, LANE), axis=-1, keepdims=True)
    rest = jnp.where(lane == i1, NEG_INF, logits)
    m2 = jnp.max(rest, axis=-1, keepdims=True)
    i2 = jnp.min(jnp.where(rest == m2, lane, LANE), axis=-1, keepdims=True)
    e2 = jnp.exp(m2 - m1)
    den = 1.0 + e2
    member = jnp.where(lane == i1, 1.0, jnp.where(lane == i2, 1.0, 0.0))
    extra = jnp.where(lane == n_experts, i1.astype(F32),
                      jnp.where(lane == n_experts + 1, i2.astype(F32),
                                jnp.where(lane == n_experts + 2, 1.0 / den, e2 / den)))
    return jnp.where(lane < n_experts, member, jnp.where(lane < n_experts + 4, extra, 0.0))


def _pack_bf16_pairs(h):
    half = h.shape[1] // 2
    lo = pltpu.bitcast(h[:, :half].astype(BF16).astype(F32), jnp.uint32)
    hi = pltpu.bitcast(h[:, half:].astype(BF16).astype(F32), jnp.uint32)
    return (hi & jnp.uint32(0xFFFF0000)) | (lo >> 16)


def _unpack_bf16_pairs(u):
    lo = pltpu.bitcast(u << 16, F32).astype(BF16)
    hi = pltpu.bitcast(u & jnp.uint32(0xFFFF0000), F32).astype(BF16)
    return lo, hi


def _pre_body(x_ref, mod_ref, g_ref, h_ref):
    h = _rms(x_ref[...], g_ref[...]) * (1.0 + mod_ref[1:2, :]) + mod_ref[0:1, :]
    h_ref[...] = h.astype(h_ref.dtype)


def _prenorm(x3, mod, g, tr=256):
    b, s, d = x3.shape
    tr = min(tr, s)
    vmem = 2 * tr * d * 4 + 2 * tr * d * 2 + 4 * tr * d * 4
    return pl.pallas_call(
        _pre_body,
        out_shape=jax.ShapeDtypeStruct((b, s, d), BF16),
        grid=(b, s // tr),
        in_specs=[pl.BlockSpec((None, tr, d), lambda bi, i: (bi, i, 0)),
                  pl.BlockSpec((None, N_MOD, d), lambda bi, i: (bi, 0, 0)),
                  pl.BlockSpec((1, d), lambda bi, i: (0, 0))],
        out_specs=pl.BlockSpec((None, tr, d), lambda bi, i: (bi, i, 0)),
        compiler_params=_params(("arbitrary", "arbitrary"), vmem),
        name="prenorm",
    )(x3, mod, g.reshape(1, d))


def _post_body(gate_row, nxt, n_experts, *refs):
    x_ref, y_ref, mod_ref, gpost_ref = refs[:4]
    pos = 4
    if nxt is not None:
        modn_ref, gpre_ref = refs[pos:pos + 2]
        pos += 2
    if n_experts:
        router_ref = refs[pos]
        pos += 1
    xo_ref = refs[pos]
    pos += 1
    xn = x_ref[...] + mod_ref[gate_row:gate_row + 1, :] * _rms(y_ref[...], gpost_ref[...])
    xo_ref[...] = xn
    if nxt is not None:
        shift_row, scale_row = nxt
        h = _rms(xn, gpre_ref[...]) * (1.0 + modn_ref[scale_row:scale_row + 1, :]) \
            + modn_ref[shift_row:shift_row + 1, :]
        if not n_experts:
            refs[pos][...] = h.astype(BF16)
            return
        hp_ref, route_ref, rank_ref, counts_ref, carry_ref = refs[pos:pos + 5]
        hp_ref[...] = _pack_bf16_pairs(h)
        route = _route(h, router_ref, n_experts)
        route_ref[...] = route

        @pl.when((pl.program_id(0) == 0) & (pl.program_id(1) == 0))
        def _():
            carry_ref[...] = jnp.zeros_like(carry_ref)

        rows = route.shape[0]
        lane = lax.broadcasted_iota(jnp.int32, route.shape, 1)
        member = jnp.where(lane < n_experts, route, 0.0)
        ri = lax.broadcasted_iota(jnp.int32, (rows, rows), 0)
        ci = lax.broadcasted_iota(jnp.int32, (rows, rows), 1)
        earlier = jnp.where(ci < ri, 1.0, 0.0).astype(BF16)
        within = jnp.dot(earlier, member.astype(BF16), preferred_element_type=F32)
        rank_ref[...] = within + carry_ref[0:1, :]
        carry_ref[...] = carry_ref[...] + jnp.sum(member, axis=0, keepdims=True)
        counts_ref[...] = carry_ref[...]


def _postnorm(x3, y3, mod, g_post, gate_row, *, nxt=None, mod_next=None, g_pre=None,
              router_t=None, tr=256):
    b, s, d = x3.shape
    tr = min(tr, s)
    n_experts = 0 if router_t is None else router_t.shape[0]
    row = lambda bi, i: (bi, i, 0)
    mod_spec = pl.BlockSpec((None, N_MOD, d), lambda bi, i: (bi, 0, 0))
    in_specs = [pl.BlockSpec((None, tr, d), row), pl.BlockSpec((None, tr, d), row), mod_spec,
                pl.BlockSpec((1, d), lambda bi, i: (0, 0))]
    args = [x3, y3, mod, g_post.reshape(1, d)]
    out_shape = [jax.ShapeDtypeStruct((b, s, d), F32)]
    out_specs = [pl.BlockSpec((None, tr, d), row)]
    scratch = []
    if nxt is not None:
        in_specs += [mod_spec, pl.BlockSpec((1, d), lambda bi, i: (0, 0))]
        args += [mod_next, g_pre.reshape(1, d)]
    if nxt is not None and not n_experts:
        out_shape.append(jax.ShapeDtypeStruct((b, s, d), BF16))
        out_specs.append(pl.BlockSpec((None, tr, d), row))
    if n_experts:
        assert nxt is not None and n_experts + 4 <= LANE
        in_specs.append(pl.BlockSpec((n_experts, d), lambda bi, i: (0, 0)))
        args.append(router_t)
        out_shape += [jax.ShapeDtypeStruct((b, s, d // 2), jnp.uint32),
                      jax.ShapeDtypeStruct((b, s, LANE), F32),
                      jax.ShapeDtypeStruct((b, s, LANE), F32),
                      jax.ShapeDtypeStruct((8, LANE), F32)]
        out_specs += [pl.BlockSpec((None, tr, d // 2), row), pl.BlockSpec((None, tr, LANE), row),
                      pl.BlockSpec((None, tr, LANE), row),
                      pl.BlockSpec((8, LANE), lambda bi, i: (0, 0))]
        scratch.append(pltpu.VMEM((8, LANE), F32))
    vmem = 2 * 3 * tr * d * 4 + 2 * tr * d * 2 + 6 * tr * d * 4
    return pl.pallas_call(
        functools.partial(_post_body, gate_row, nxt, n_experts),
        out_shape=out_shape,
        grid=(b, s // tr),
        in_specs=in_specs,
        out_specs=out_specs,
        scratch_shapes=scratch,
        compiler_params=_params(("arbitrary", "arbitrary"), vmem),
        name="postnorm",
    )(*args)


def _lat_body(alat_ref, pik_ref, g_ref, lat_ref, latt_ref, ikt_ref):
    lat = _rms(alat_ref[...], g_ref[...])
    lat_ref[...] = lat.astype(BF16)
    latt_ref[...] = lat.T.astype(BF16)
    ikt_ref[...] = pik_ref[...].T.astype(BF16)


def _latnorm(pa3, pik3, g_lat, lat_col, n_lat, tr=512):
    b, s, _ = pa3.shape
    tr = min(tr, s)
    assert lat_col % n_lat == 0
    vmem = 2 * tr * (n_lat + LANE) * 4 + 4 * tr * (n_lat + LANE) * 2 + 4 * tr * n_lat * 4
    return pl.pallas_call(
        _lat_body,
        out_shape=[jax.ShapeDtypeStruct((b, s, n_lat), BF16),
                   jax.ShapeDtypeStruct((b, n_lat, s), BF16),
                   jax.ShapeDtypeStruct((b, LANE, s), BF16)],
        grid=(b, s // tr),
        in_specs=[pl.BlockSpec((None, tr, n_lat), lambda bi, i: (bi, i, lat_col // n_lat)),
                  pl.BlockSpec((None, tr, LANE), lambda bi, i: (bi, i, 0)),
                  pl.BlockSpec((1, n_lat), lambda bi, i: (0, 0))],
        out_specs=[pl.BlockSpec((None, tr, n_lat), lambda bi, i: (bi, i, 0)),
                   pl.BlockSpec((None, n_lat, tr), lambda bi, i: (bi, 0, i)),
                   pl.BlockSpec((None, LANE, tr), lambda bi, i: (bi, 0, i))],
        compiler_params=_params(("arbitrary", "arbitrary"), vmem),
        name="latnorm",
    )(pa3, pik3, g_lat.reshape(1, n_lat))


def _dsa_body(topk, n_variants, *refs):
    i = pl.program_id(1)
    n_blk = refs[4].shape[0] // A_BLOCK
    per = -(-n_blk // n_variants)
    for v in range(n_variants):
        blocks = min(n_blk, (v + 1) * per)

        @pl.when((i >= v * per) & (i < (v + 1) * per))
        def _(blocks=blocks):
            _dsa_compute(topk, blocks * A_BLOCK, i, *refs)


def _dsa_compute(topk, seq, i, q_ref, iq_ref, iw_ref, ikt_ref, lat_ref, latt_ref, wuk_ref, wuv_ref,
                 bias_ref, o_ref, madd_ref):
    n_blk = seq // A_BLOCK
    n_heads = wuk_ref.shape[0]
    int_min = jnp.int32(-2 ** 31)
    qpos = i * A_BLOCK + lax.broadcasted_iota(jnp.int32, (A_BLOCK, 1), 0)
    kpos = lax.broadcasted_iota(jnp.int32, (A_BLOCK, seq), 1)

    iq = iq_ref[...].astype(BF16)
    iw = iw_ref[...]
    ikt = ikt_ref[0:IDX_DIM, 0:seq]
    score = jnp.zeros((A_BLOCK, seq), F32)
    for h in range(IDX_HEADS):
        l = jnp.dot(iq[:, h * IDX_DIM:(h + 1) * IDX_DIM], ikt, preferred_element_type=F32)
        score = score + iw[:, IDX_DIM + h:IDX_DIM + h + 1] * jnp.maximum(l, 0.0)

    score = jnp.where(score == 0.0, 0.0, score)
    key = pltpu.bitcast(score, jnp.int32)
    key = jnp.where(key < 0, key ^ jnp.int32(0x7FFFFFFF), key)
    key = jnp.where(kpos <= qpos, key, int_min)
    k_eff = jnp.minimum(qpos + 1, topk).astype(F32)

    def count(mask):
        return jnp.sum(jnp.where(mask, 1.0, 0.0), axis=1, keepdims=True)

    t0 = jnp.where(count(key >= 0) >= k_eff, jnp.int32(0), int_min)

    def value_step(it, t):
        c = t | (jnp.int32(1) << (30 - it))
        return jnp.where(count(key >= c) >= k_eff, c, t)

    t = lax.fori_loop(0, 31, value_step, t0)

    need = k_eff - count(key > t)
    eq = key == t
    idx_bits = max(1, (seq - 1).bit_length())

    def index_step(it, p):
        c = p | (jnp.int32(1) << (idx_bits - 1 - it))
        below = jnp.sum(jnp.where(eq, jnp.where(kpos < c, 1.0, 0.0), 0.0), axis=1, keepdims=True)
        return jnp.where(below < need, c, p)

    p = lax.fori_loop(0, idx_bits, index_step, jnp.zeros((A_BLOCK, 1), jnp.int32))
    tie_add = jnp.where(eq, jnp.where(kpos <= p, 0.0, NEG_INF), NEG_INF)
    madd_ref[:, 0:seq] = jnp.where(key > t, 0.0, tie_add)

    scale = HEAD_DIM ** -0.5
    for h in range(n_heads):
        cols = slice(h * HEAD_DIM, (h + 1) * HEAD_DIM)
        qa = jnp.dot(q_ref[:, cols].astype(BF16), wuk_ref[h],
                     preferred_element_type=F32).astype(BF16)
        lg = jnp.dot(qa, latt_ref[:, 0:seq], preferred_element_type=F32) * scale
        bias = jnp.concatenate([bias_ref[jnp.maximum(i - j, 0), h] for j in range(n_blk)], axis=1)
        lg = lg + bias + madd_ref[:, 0:seq]
        m = jnp.max(lg, axis=1, keepdims=True)
        e = jnp.exp(lg - m)
        den = jnp.sum(e, axis=1, keepdims=True)
        o_lat = jnp.dot(e.astype(BF16), lat_ref[0:seq, :], preferred_element_type=F32) / den
        o_ref[:, cols] = jnp.dot(o_lat.astype(BF16), wuv_ref[h],
                                 preferred_element_type=F32).astype(o_ref.dtype)


def _dsa(pa3, pik3, lat, latt, ikt, wuk_t, wuv_t, bias_a, *, q_width, iq_col, iq_width,
         n_variants):
    b, s, _ = pa3.shape
    n_lat = lat.shape[-1]
    n_heads = wuk_t.shape[0]
    n_blk = s // A_BLOCK
    topk = min(IDX_TOPK, s // 4)
    assert iq_col % iq_width == 0
    vmem = 2 * (A_BLOCK * (q_width + iq_width + LANE) * 4 + LANE * s * 2 + 2 * s * n_lat * 2
                + 2 * n_heads * HEAD_DIM * n_lat * 2 + n_heads * n_blk * A_BLOCK * A_BLOCK * 4
                + A_BLOCK * q_width * 2) + 10 * A_BLOCK * s * 4
    return pl.pallas_call(
        functools.partial(_dsa_body, topk, n_variants),
        out_shape=jax.ShapeDtypeStruct((b, s, q_width), BF16),
        grid=(b, n_blk),
        in_specs=[pl.BlockSpec((None, A_BLOCK, q_width), lambda bi, i: (bi, i, 0)),
                  pl.BlockSpec((None, A_BLOCK, iq_width), lambda bi, i: (bi, i, iq_col // iq_width)),
                  pl.BlockSpec((None, A_BLOCK, LANE), lambda bi, i: (bi, i, 0)),
                  pl.BlockSpec((None, LANE, s), lambda bi, i: (bi, 0, 0)),
                  pl.BlockSpec((None, s, n_lat), lambda bi, i: (bi, 0, 0)),
                  pl.BlockSpec((None, n_lat, s), lambda bi, i: (bi, 0, 0)),
                  pl.BlockSpec(wuk_t.shape, lambda bi, i: (0, 0, 0)),
                  pl.BlockSpec(wuv_t.shape, lambda bi, i: (0, 0, 0)),
                  pl.BlockSpec(bias_a.shape, lambda bi, i: (0, 0, 0, 0))],
        out_specs=pl.BlockSpec((None, A_BLOCK, q_width), lambda bi, i: (bi, i, 0)),
        scratch_shapes=[pltpu.VMEM((A_BLOCK, s), F32)],
        compiler_params=_params(("arbitrary", "arbitrary"), vmem),
        name="dsa",
    )(pa3, pa3, pik3, ikt, lat, latt, wuk_t, wuv_t, bias_a)


def _dil_body(span, dil, hb, has_prev, *refs):
    if has_prev:
        q_ref, kp_ref, kc_ref, vp_ref, vc_ref, bias_ref, o_ref, lse_ref = refs
    else:
        q_ref, kc_ref, vc_ref, bias_ref, o_ref, lse_ref = refs
    i = pl.program_id(1)
    hblk = pl.program_id(2)
    n_keys = 2 * B_BLOCK if has_prev else B_BLOCK
    key0 = 0 if has_prev else B_BLOCK
    qi = lax.broadcasted_iota(jnp.int32, (B_BLOCK, n_keys), 0)
    kj = lax.broadcasted_iota(jnp.int32, (B_BLOCK, n_keys), 1) + key0
    sub_dist = qi + B_BLOCK - kj
    in_band = jnp.where(sub_dist <= span,
                        jnp.where((i - 1) * B_BLOCK + kj >= 0, 0.0, NEG_INF), NEG_INF)
    madd = jnp.where(sub_dist >= 0, in_band, NEG_INF)
    lane = lax.broadcasted_iota(jnp.int32, (B_BLOCK, LANE), 1)
    scale = HEAD_DIM ** -0.5

    @pl.when(hblk == 0)
    def _():
        lse_ref[...] = jnp.zeros_like(lse_ref)

    def one_subsequence(r, carry):
        rows = pl.ds(r, B_BLOCK, stride=dil) if dil > 1 else slice(None)
        lse_acc = lse_ref[rows, :]
        for h in range(hb):
            cols = slice(h * HEAD_DIM, (h + 1) * HEAD_DIM)
            q = q_ref[rows, cols].astype(BF16)
            if has_prev:
                k2 = jnp.concatenate([kp_ref[rows, cols], kc_ref[rows, cols]], axis=0).astype(BF16)
                v2 = jnp.concatenate([vp_ref[rows, cols], vc_ref[rows, cols]], axis=0).astype(BF16)
            else:
                k2 = kc_ref[rows, cols].astype(BF16)
                v2 = vc_ref[rows, cols].astype(BF16)
            lg = lax.dot_general(q, k2, (((1,), (1,)), ((), ())), preferred_element_type=F32)
            lg = lg * scale + bias_ref[h, :, key0:] + madd
            m = jnp.max(lg, axis=1, keepdims=True)
            e = jnp.exp(lg - m)
            den = jnp.sum(e, axis=1, keepdims=True)
            o_ref[rows, cols] = jnp.dot((e / den).astype(BF16), v2, preferred_element_type=F32)
            lse_acc = jnp.where(lane == hblk * hb + h, m + jnp.log(den), lse_acc)
        lse_ref[rows, :] = lse_acc
        return carry

    if dil > 1:
        lax.fori_loop(0, dil, one_subsequence, 0)
    else:
        one_subsequence(0, 0)


def _dilated(rest3, bias_b, g, n_groups, window, dil, width):
    b, s, w_all = rest3.shape
    n_heads = width // HEAD_DIM
    span = window // dil
    n_sub = s // dil
    assert span == B_BLOCK and n_sub % B_BLOCK == 0 and w_all % width == 0
    nb = n_sub // B_BLOCK
    rows = B_BLOCK * dil
    hb = n_heads if dil == 1 else 1
    assert HEAD_DIM == LANE
    n_hblk = n_heads // hb
    has_prev = nb > 1
    cur = lambda slot: (lambda bi, i, hk: (bi, i, (slot * n_groups + g) * n_hblk + hk))
    prev = lambda slot: (lambda bi, i, hk: (bi, jnp.maximum(i - 1, 0),
                                            (slot * n_groups + g) * n_hblk + hk))
    blk = (None, rows, hb * HEAD_DIM)
    in_specs = [pl.BlockSpec(blk, cur(0))]
    for slot in (1, 2):
        if has_prev:
            in_specs.append(pl.BlockSpec(blk, prev(slot)))
        in_specs.append(pl.BlockSpec(blk, cur(slot)))
    in_specs.append(pl.BlockSpec((None, hb, B_BLOCK, 2 * B_BLOCK), lambda bi, i, hk: (g, hk, 0, 0)))
    vmem = 2 * (len(in_specs) * rows * hb * HEAD_DIM + rows * LANE) * 4 \
        + 2 * hb * B_BLOCK * 2 * B_BLOCK * 4 + (4 << 20)
    o, lse = pl.pallas_call(
        functools.partial(_dil_body, span, dil, hb, has_prev),
        out_shape=[jax.ShapeDtypeStruct((b, s, width), F32),
                   jax.ShapeDtypeStruct((b, s, LANE), F32)],
        grid=(b, nb, n_hblk),
        in_specs=in_specs,
        out_specs=[pl.BlockSpec(blk, lambda bi, i, hk: (bi, i, hk)),
                   pl.BlockSpec((None, rows, LANE), lambda bi, i, hk: (bi, i, 0))],
        compiler_params=_params(("arbitrary", "arbitrary", "arbitrary"), vmem),
        name=f"dilated{g}",
    )(*([rest3] * (len(in_specs) - 1)), bias_b)
    return o.reshape(b * s, width), lse.reshape(b * s, LANE)


def _mix_body(n_groups, n_heads, *refs):
    o_refs = refs[:n_groups]
    l_refs = refs[n_groups:2 * n_groups]
    out_ref = refs[2 * n_groups]
    ls = [r[...] for r in l_refs]
    m = functools.reduce(jnp.maximum, ls)
    es = [jnp.exp(l - m) for l in ls]
    tot = functools.reduce(jnp.add, es)
    ws = [e / tot for e in es]
    for h in range(n_heads):
        cols = slice(h * HEAD_DIM, (h + 1) * HEAD_DIM)
        acc = ws[0][:, h:h + 1] * o_refs[0][:, cols]
        for gi in range(1, n_groups):
            acc = acc + ws[gi][:, h:h + 1] * o_refs[gi][:, cols]
        out_ref[:, cols] = acc.astype(out_ref.dtype)


def _mixture(outs, lses, tr=512):
    t, width = outs[0].shape
    tr = min(tr, t)
    n_groups = len(outs)
    vmem = 2 * n_groups * tr * (width + LANE) * 4 + 2 * tr * width * 2 + 4 * tr * width * 4
    return pl.pallas_call(
        functools.partial(_mix_body, n_groups, width // HEAD_DIM),
        out_shape=jax.ShapeDtypeStruct((t, width), BF16),
        grid=(t // tr,),
        in_specs=[pl.BlockSpec((tr, width), lambda i: (i, 0))] * n_groups
        + [pl.BlockSpec((tr, LANE), lambda i: (i, 0))] * n_groups,
        out_specs=pl.BlockSpec((tr, width), lambda i: (i, 0)),
        compiler_params=_params(("arbitrary",), vmem),
        name="mixture",
    )(*outs, *lses)


def _gmlp_body(u_ref, v_ref, g_ref, b_ref, ws_ref, bs_ref, o_ref):
    u = jax.nn.gelu(u_ref[...])
    v = jax.nn.gelu(v_ref[...])
    mu = jnp.mean(v, axis=-1, keepdims=True)
    var = jnp.mean(jnp.square(v - mu), axis=-1, keepdims=True)
    v = (v - mu) * lax.rsqrt(var + EPS) * g_ref[...] + b_ref[...]
    ti = lax.broadcasted_iota(jnp.int32, (C_CHUNK, C_CHUNK), 0)
    si = lax.broadcasted_iota(jnp.int32, (C_CHUNK, C_CHUNK), 1)
    causal = si <= ti
    gw = u.shape[1] // C_GROUPS
    for g in range(C_GROUPS):
        cols = slice(g * gw, (g + 1) * gw)
        w = jnp.where(causal, ws_ref[g], 0.0).astype(BF16)
        mixed = jnp.dot(w, v[:, cols].astype(BF16), preferred_element_type=F32) + bs_ref[:, g:g + 1]
        o_ref[:, cols] = (u[:, cols] * mixed).astype(o_ref.dtype)


def _gmlp(rest, u_col, width, ln_g, ln_b, w_s, b_s_t):
    t = rest.shape[0]
    assert u_col % width == 0 and (width // C_GROUPS) % LANE == 0
    vmem = 2 * 2 * C_CHUNK * width * 4 + 2 * C_CHUNK * width * 2 + 2 * C_GROUPS * C_CHUNK * C_CHUNK * 4 \
        + 8 * C_CHUNK * width * 4
    return pl.pallas_call(
        _gmlp_body,
        out_shape=jax.ShapeDtypeStruct((t, width), BF16),
        grid=(t // C_CHUNK,),
        in_specs=[pl.BlockSpec((C_CHUNK, width), lambda i: (i, u_col // width)),
                  pl.BlockSpec((C_CHUNK, width), lambda i: (i, u_col // width + 1)),
                  pl.BlockSpec((1, width), lambda i: (0, 0)),
                  pl.BlockSpec((1, width), lambda i: (0, 0)),
                  pl.BlockSpec(w_s.shape, lambda i: (0, 0, 0)),
                  pl.BlockSpec(b_s_t.shape, lambda i: (0, 0))],
        out_specs=pl.BlockSpec((C_CHUNK, width), lambda i: (i, 0)),
        compiler_params=_params(("arbitrary",), vmem),
        name="gmlp",
    )(rest, rest, ln_g.reshape(1, width), ln_b.reshape(1, width), w_s, b_s_t)


def _conv_body(h_ref, bg_ref, cg_ref, w_ref, o_ref, carry_ref):
    @pl.when(pl.program_id(1) == 0)
    def _():
        carry_ref[...] = jnp.zeros_like(carry_ref)

    ch = cg_ref[...] * h_ref[...]
    rows = ch.shape[0]
    row = lax.broadcasted_iota(jnp.int32, ch.shape, 0)
    pm1 = carry_ref[7:8, :]
    pm2 = carry_ref[6:7, :]
    s1 = jnp.where(row == 0, pm1, pltpu.roll(ch, 1, 0))
    s2 = jnp.where(row == 0, pm2, jnp.where(row == 1, pm1, pltpu.roll(ch, 2, 0)))
    z = w_ref[0:1, :] * s2 + w_ref[1:2, :] * s1 + w_ref[2:3, :] * ch
    o_ref[...] = (bg_ref[...] * z).astype(o_ref.dtype)
    carry_ref[...] = ch[rows - 8:, :]


def _short_conv(rest3, h_col, width, conv_w, tr=256):
    b, s, _ = rest3.shape
    tr = min(tr, s)
    assert h_col % width == 0 and conv_w.shape[0] == D_CONV_WIDTH
    cb = h_col // width
    vmem = 2 * 3 * tr * width * 4 + 2 * tr * width * 2 + 6 * tr * width * 4
    return pl.pallas_call(
        _conv_body,
        out_shape=jax.ShapeDtypeStruct((b, s, width), BF16),
        grid=(b, s // tr),
        in_specs=[pl.BlockSpec((None, tr, width), lambda bi, i: (bi, i, cb)),
                  pl.BlockSpec((None, tr, width), lambda bi, i: (bi, i, cb + 1)),
                  pl.BlockSpec((None, tr, width), lambda bi, i: (bi, i, cb + 2)),
                  pl.BlockSpec(conv_w.shape, lambda bi, i: (0, 0))],
        out_specs=pl.BlockSpec((None, tr, width), lambda bi, i: (bi, i, 0)),
        scratch_shapes=[pltpu.VMEM((8, width), F32)],
        compiler_params=_params(("arbitrary", "arbitrary"), vmem),
        name="short_conv",
    )(rest3, rest3, rest3, conv_w)


def _merge_body(chunk, *refs):
    br_refs = refs[:N_BRANCH]
    gate_refs = refs[N_BRANCH:2 * N_BRANCH]
    w_ref, o_ref, wb_ref = refs[2 * N_BRANCH:]

    @pl.when(pl.program_id(1) == 0)
    def _():
        for n in range(N_BRANCH):
            _cast_rows(lambda rows, n=n: w_ref[n, rows, :], wb_ref.at[n], w_ref.shape[1], chunk)

    acc = None
    for n in range(N_BRANCH):
        proj = jnp.dot(br_refs[n][...], wb_ref[n], preferred_element_type=F32)
        term = jax.nn.sigmoid(gate_refs[n][...]) * proj
        acc = term if acc is None else acc + term
    o_ref[...] = acc.astype(o_ref.dtype)


def _merge(branches, rest, gate_col, w_branch, layer, tn=512, tm=512):
    t, width = branches[0].shape
    d = w_branch.shape[-1]
    tn = min(tn, d)
    tm = min(tm, t)
    assert d % tn == 0 and t % tm == 0 and gate_col % tn == 0
    chunk = _pick_tile(width, (256, 128, 8))
    gate_spec = lambda n: pl.BlockSpec((tm, tn), lambda j, i: (i, (gate_col + n * d) // tn + j))
    vmem = 2 * N_BRANCH * (tm * width * 2 + tm * tn * 4 + width * tn * 4) + N_BRANCH * width * tn * 2 \
        + 2 * tm * tn * 2 + 4 * tm * tn * 4
    return pl.pallas_call(
        functools.partial(_merge_body, chunk),
        out_shape=jax.ShapeDtypeStruct((t, d), BF16),
        grid=(d // tn, t // tm),
        in_specs=[pl.BlockSpec((tm, width), lambda j, i: (i, 0))] * N_BRANCH
        + [gate_spec(n) for n in range(N_BRANCH)]
        + [pl.BlockSpec((None, N_BRANCH, width, tn), lambda j, i: (layer, 0, 0, j))],
        out_specs=pl.BlockSpec((tm, tn), lambda j, i: (i, j)),
        scratch_shapes=[pltpu.VMEM((N_BRANCH, width, tn), BF16)],
        compiler_params=_params(("arbitrary", "arbitrary"), vmem),
        name="merge",
    )(*branches, *([rest] * N_BRANCH), w_branch)


def _up_body(chunk, h_ref, wg_ref, wu_ref, o_ref, wgb_ref, wub_ref):
    @pl.when(pl.program_id(1) == 0)
    def _():
        _cast_rows(lambda rows: wg_ref[rows, :], wgb_ref, wg_ref.shape[0], chunk)
        _cast_rows(lambda rows: wu_ref[rows, :], wub_ref, wu_ref.shape[0], chunk)

    h = h_ref[...]
    g = jnp.dot(h, wgb_ref[...], preferred_element_type=F32)
    u = jnp.dot(h, wub_ref[...], preferred_element_type=F32)
    o_ref[...] = (g * jax.nn.sigmoid(g) * u).astype(o_ref.dtype)


def _swiglu_up(h, wg, wu, tm=512):
    t, k = h.shape
    n_e, _, f = wg.shape
    tn = _pick_tile(f, (256, 128))
    tm = min(tm, t)
    nj = f // tn
    chunk = _pick_tile(k, (256, 128, 8))
    w_spec = pl.BlockSpec((None, k, tn), lambda j, i: (j // nj, 0, j % nj))
    vmem = 2 * tm * k * 2 + 4 * k * tn * 4 + 2 * k * tn * 2 + 2 * tm * tn * 2 + 4 * tm * tn * 4
    return pl.pallas_call(
        functools.partial(_up_body, chunk),
        out_shape=jax.ShapeDtypeStruct((t, n_e * f), BF16),
        grid=(n_e * nj, t // tm),
        in_specs=[pl.BlockSpec((tm, k), lambda j, i: (i, 0)), w_spec, w_spec],
        out_specs=pl.BlockSpec((tm, tn), lambda j, i: (i, j)),
        scratch_shapes=[pltpu.VMEM((k, tn), BF16), pltpu.VMEM((k, tn), BF16)],
        compiler_params=_params(("arbitrary", "arbitrary"), vmem),
        name="swiglu_up",
    )(h, wg, wu)


def _down_body(tm, chunk, a_ref, w_ref, o_ref, wb_ref):
    kk = pl.program_id(1)
    i = pl.program_id(2)

    @pl.when(i == 0)
    def _():
        _cast_rows(lambda rows: w_ref[rows, :], wb_ref, w_ref.shape[0], chunk)

    part = jnp.dot(a_ref[...], wb_ref[...], preferred_element_type=F32)
    rows = pl.ds(pl.multiple_of(i * tm, tm), tm)

    @pl.when(kk == 0)
    def _():
        o_ref[rows, :] = part

    @pl.when(kk > 0)
    def _():
        o_ref[rows, :] += part


def _swiglu_down(a, wd, tk, tn=256, tm=512):
    t, k = a.shape
    n = wd.shape[1]
    tn = min(tn, n)
    tm = min(tm, t)
    assert k % tk == 0 and n % tn == 0 and t % tm == 0
    chunk = _pick_tile(tk, (256, 128, 8))
    in_specs = [pl.BlockSpec((tm, tk), lambda j, kk, i: (i, kk)),
                pl.BlockSpec((tk, tn), lambda j, kk, i: (kk, j))]
    args = [a, wd]
    vmem = 2 * tm * tk * 2 + 2 * tk * tn * 4 + tk * tn * 2 + 2 * t * tn * 4 + 4 * tm * tn * 4
    return pl.pallas_call(
        functools.partial(_down_body, tm, chunk),
        out_shape=jax.ShapeDtypeStruct((t, n), F32),
        grid=(n // tn, k // tk, t // tm),
        in_specs=in_specs,
        out_specs=pl.BlockSpec((t, tn), lambda j, kk, i: (0, j)),
        scratch_shapes=[pltpu.VMEM((tk, tn), BF16)],
        compiler_params=_params(("arbitrary", "arbitrary", "arbitrary"), vmem),
        name="swiglu_down",
    )(*args)


MOE_TILE = 256
DISPATCH_CHUNK = 16


def _dispatch_body(n_tok, n_tiles, pos1_ref, pos2_ref, h_ref, xs_ref, zero_ref, zsem, sem):
    zero_ref[...] = jnp.zeros_like(zero_ref)

    def zero_copy(r):
        return pltpu.make_async_copy(zero_ref, xs_ref.at[pl.ds(r * MOE_TILE, MOE_TILE), :], zsem)

    lax.fori_loop(0, n_tiles, lambda r, c: (zero_copy(r).start(), c)[1], 0)
    lax.fori_loop(0, n_tiles, lambda r, c: (zero_copy(r).wait(), c)[1], 0)

    def row_copies(t):
        src = h_ref.at[pl.ds(t, 1), :]
        return (pltpu.make_async_copy(src, xs_ref.at[pl.ds(pos1_ref[t], 1), :], sem),
                pltpu.make_async_copy(src, xs_ref.at[pl.ds(pos2_ref[t], 1), :], sem))

    def start_chunk(c):
        for k in range(DISPATCH_CHUNK):
            for cp in row_copies(c * DISPATCH_CHUNK + k):
                cp.start()

    def wait_chunk(c):
        for k in range(DISPATCH_CHUNK):
            for cp in row_copies(c * DISPATCH_CHUNK + k):
                cp.wait()

    n_chunks = n_tok // DISPATCH_CHUNK
    start_chunk(0)

    def body(c, carry):
        start_chunk(c)
        wait_chunk(c - 1)
        return carry

    lax.fori_loop(1, n_chunks, body, 0)
    wait_chunk(n_chunks - 1)


def _moe_dispatch(hp, pos1, pos2, n_rows):
    t, w = hp.shape
    assert t % DISPATCH_CHUNK == 0 and n_rows % MOE_TILE == 0
    return pl.pallas_call(
        functools.partial(_dispatch_body, t, n_rows // MOE_TILE),
        out_shape=jax.ShapeDtypeStruct((n_rows, w), hp.dtype),
        grid_spec=pltpu.PrefetchScalarGridSpec(
            num_scalar_prefetch=2,
            grid=(1,),
            in_specs=[pl.BlockSpec(memory_space=pl.ANY)],
            out_specs=pl.BlockSpec(memory_space=pl.ANY),
            scratch_shapes=[pltpu.VMEM((MOE_TILE, w), hp.dtype),
                            pltpu.SemaphoreType.DMA(()), pltpu.SemaphoreType.DMA(())]),
        compiler_params=_params(("arbitrary",), 2 * MOE_TILE * w * 4),
        name="moe_dispatch",
    )(pos1, pos2, hp)


def _group_body(swiglu, chunk, te_ref, nused_ref, *refs):
    if swiglu:
        x_ref, wg_ref, wu_ref, o_ref, wgb_ref, wub_ref = refs
        pairs = ((wg_ref, wgb_ref), (wu_ref, wub_ref))
    else:
        x_ref, w_ref, o_ref, wb_ref = refs
        pairs = ((w_ref, wb_ref),)
    r = pl.program_id(1)
    expert = te_ref[r]
    prev = te_ref[jnp.maximum(r - 1, 0)]

    @pl.when(jnp.logical_or(r == 0, expert != prev))
    def _():
        for src_ref, dst_ref in pairs:
            _cast_rows(lambda rows, s=src_ref: s[rows, :], dst_ref, src_ref.shape[0], chunk)

    @pl.when(r < nused_ref[0])
    def _():
        if swiglu:
            lo, hi = _unpack_bf16_pairs(x_ref[...])
            half = lo.shape[1]

            def proj(wb):
                return jnp.dot(lo, wb[0:half, :], preferred_element_type=F32) \
                    + jnp.dot(hi, wb[half:, :], preferred_element_type=F32)

            g = proj(wgb_ref)
            o_ref[...] = (g * jax.nn.sigmoid(g) * proj(wub_ref)).astype(o_ref.dtype)
        else:
            o_ref[...] = jnp.dot(x_ref[...], wb_ref[...],
                                 preferred_element_type=F32).astype(o_ref.dtype)

    @pl.when(r >= nused_ref[0])
    def _():
        o_ref[...] = jnp.zeros_like(o_ref)


def _grouped_matmul(xs, weights, layer, tile_expert, n_used, out_dtype, tn, name):
    p, kx = xs.shape
    swiglu = len(weights) == 2
    _, _, k, n = weights[0].shape
    assert k == (2 * kx if swiglu else kx) and n % tn == 0 and p % MOE_TILE == 0
    chunk = _pick_tile(k, (256, 128, 8))
    w_spec = pl.BlockSpec((None, None, k, tn), lambda j, r, te, nu: (layer, te[r], 0, j))
    osz = jnp.dtype(out_dtype).itemsize
    vmem = 2 * MOE_TILE * kx * xs.dtype.itemsize + len(weights) * (2 * k * tn * 4 + k * tn * 2) \
        + 2 * MOE_TILE * tn * osz + 6 * MOE_TILE * tn * 4 + 2 * MOE_TILE * k * 2
    return pl.pallas_call(
        functools.partial(_group_body, swiglu, chunk),
        out_shape=jax.ShapeDtypeStruct((p, n), out_dtype),
        grid_spec=pltpu.PrefetchScalarGridSpec(
            num_scalar_prefetch=2,
            grid=(n // tn, p // MOE_TILE),
            in_specs=[pl.BlockSpec((MOE_TILE, kx), lambda j, r, te, nu: (r, 0))]
            + [w_spec] * len(weights),
            out_specs=pl.BlockSpec((MOE_TILE, tn), lambda j, r, te, nu: (r, j)),
            scratch_shapes=[pltpu.VMEM((k, tn), BF16)] * len(weights)),
        compiler_params=_params(("arbitrary", "arbitrary"), vmem),
        name=name,
    )(tile_expert, n_used, xs, *weights)


def _combine_body(tc, n_experts, pos1_ref, pos2_ref, route_ref, ye_ref, o_ref, buf_ref, sem):
    i = pl.program_id(0)

    def row_copies(tile, slot, k):
        t = tile * tc + k
        return (pltpu.make_async_copy(ye_ref.at[pl.ds(pos1_ref[t], 1), :],
                                      buf_ref.at[slot, 0, pl.ds(k, 1), :], sem.at[slot]),
                pltpu.make_async_copy(ye_ref.at[pl.ds(pos2_ref[t], 1), :],
                                      buf_ref.at[slot, 1, pl.ds(k, 1), :], sem.at[slot]))

    def start_tile(tile, slot):
        def body(k, c):
            for cp in row_copies(tile, slot, k):
                cp.start()
            return c
        lax.fori_loop(0, tc, body, 0)

    def wait_tile(tile, slot):
        def body(k, c):
            for cp in row_copies(tile, slot, k):
                cp.wait()
            return c
        lax.fori_loop(0, tc, body, 0)

    @pl.when(i == 0)
    def _():
        start_tile(0, 0)

    @pl.when(i + 1 < pl.num_programs(0))
    def _():
        start_tile(i + 1, (i + 1) % 2)

    slot = i % 2
    wait_tile(i, slot)
    g1 = route_ref[:, n_experts + 2:n_experts + 3]
    g2 = route_ref[:, n_experts + 3:n_experts + 4]
    o_ref[...] = g1 * buf_ref[slot, 0] + g2 * buf_ref[slot, 1]


def _moe_combine(ye, route, pos1, pos2, n_experts, tc=128):
    t = route.shape[0]
    d = ye.shape[1]
    tc = min(tc, t)
    assert t % tc == 0
    return pl.pallas_call(
        functools.partial(_combine_body, tc, n_experts),
        out_shape=jax.ShapeDtypeStruct((t, d), F32),
        grid_spec=pltpu.PrefetchScalarGridSpec(
            num_scalar_prefetch=2,
            grid=(t // tc,),
            in_specs=[pl.BlockSpec((tc, LANE), lambda i, p1, p2: (i, 0)),
                      pl.BlockSpec(memory_space=pl.ANY)],
            out_specs=pl.BlockSpec((tc, d), lambda i, p1, p2: (i, 0)),
            scratch_shapes=[pltpu.VMEM((2, 2, tc, d), F32), pltpu.SemaphoreType.DMA((2,))]),
        compiler_params=_params(("arbitrary",), 4 * tc * d * 4 + 4 * tc * d * 4),
        name="moe_combine",
    )(pos1, pos2, route, ye)


def _moe(hp, route, rank, counts, layer_idx, w_gate, w_up, w_down):
    t = hp.shape[0]
    n_e = w_gate.shape[1]
    n_rows = TOP_K * t + n_e * MOE_TILE
    n_tiles = n_rows // MOE_TILE
    i1 = route[:, n_e].astype(jnp.int32)
    i2 = route[:, n_e + 1].astype(jnp.int32)
    cnt = counts[0, :n_e].astype(jnp.int32)
    tiles = (cnt + MOE_TILE - 1) // MOE_TILE
    tile_end = jnp.cumsum(tiles)
    row0 = (tile_end - tiles) * MOE_TILE
    slot = row0[None, :] + rank[:, :n_e].astype(jnp.int32)
    experts = jnp.arange(n_e, dtype=jnp.int32)[None, :]
    pos1 = jnp.sum(jnp.where(experts == i1[:, None], slot, 0), axis=1).astype(jnp.int32)
    pos2 = jnp.sum(jnp.where(experts == i2[:, None], slot, 0), axis=1).astype(jnp.int32)
    n_used = tile_end[-1:].astype(jnp.int32)
    tile_ids = jnp.minimum(jnp.arange(n_tiles, dtype=jnp.int32), n_used[0] - 1)
    tile_expert = jnp.sum(tile_ids[:, None] >= tile_end[None, :], axis=1).astype(jnp.int32)

    xs = _moe_dispatch(hp, pos1, pos2, n_rows)
    f = w_gate.shape[-1]
    a = _grouped_matmul(xs, (w_gate, w_up), layer_idx, tile_expert, n_used, BF16,
                        _pick_tile(f, (256, 128)), "moe_up")
    ye = _grouped_matmul(a, (w_down,), layer_idx, tile_expert, n_used, F32,
                         _pick_tile(w_down.shape[-1], (512, 256, 128)), "moe_down")
    return _moe_combine(ye, route, pos1, pos2, n_e)


def _rel_bucket(dist):
    max_exact = REL_BUCKETS // 2
    d = jnp.maximum(dist, 0)
    df = jnp.maximum(d, max_exact).astype(F32)
    large = max_exact + (jnp.log(df / max_exact) / math.log(REL_MAX_DIST / max_exact)
                         * (REL_BUCKETS - max_exact)).astype(jnp.int32)
    large = jnp.minimum(large, REL_BUCKETS - 1)
    return jnp.where(d < max_exact, d, large)


def _bias_body(n_heads, head0, head_stride, bucket_ref, tab_ref, o_ref):
    bucket = bucket_ref[...]
    base = head0 + head_stride * pl.program_id(0)
    for h in range(n_heads):
        acc = jnp.zeros(bucket.shape, F32)
        for b in range(REL_BUCKETS):
            acc = jnp.where(bucket == b, tab_ref[b, base + h], acc)
        o_ref[h] = acc


def _bias_tiles(rel_bias, buckets, n_heads, head0, head_stride):
    n_tiles, q, k = buckets.shape
    return pl.pallas_call(
        functools.partial(_bias_body, n_heads, head0, head_stride),
        out_shape=jax.ShapeDtypeStruct((n_tiles, n_heads, q, k), F32),
        grid=(n_tiles,),
        in_specs=[pl.BlockSpec((None, q, k), lambda t: (t, 0, 0)),
                  pl.BlockSpec(memory_space=pltpu.SMEM)],
        out_specs=pl.BlockSpec((None, n_heads, q, k), lambda t: (t, 0, 0, 0)),
        compiler_params=_params(("arbitrary",), 4 * (n_heads + 1) * q * k * 4),
        name="bias_tiles",
    )(buckets, rel_bias)


def _bias_tiles_a(rel_bias, n_heads, seq):
    n_blk = seq // A_BLOCK
    qi = jnp.arange(A_BLOCK)[:, None]
    kj = jnp.arange(A_BLOCK)[None, :]
    dist = jnp.arange(n_blk)[:, None, None] * A_BLOCK + (qi - kj)[None]
    return _bias_tiles(rel_bias, _rel_bucket(dist), n_heads, 0, 0)


def _bias_tiles_b(rel_bias, head0, n_heads):
    qi = jnp.arange(B_BLOCK)[:, None]
    kj = jnp.arange(2 * B_BLOCK)[None, :]
    sub_dist = qi + B_BLOCK - kj
    buckets = jnp.stack([_rel_bucket(sub_dist * dil) for _, dil in B_GROUPS])
    return _bias_tiles(rel_bias, buckets, n_heads, head0, n_heads)


def _hybrid_mixer(h, bsz, seq, layer, w_in_t, lat_g, w_uk, w_uv, c_ln_g, c_ln_b, c_w_s, c_b_s,
                  d_conv, w_branch, w_out, bias_a, bias_b):
    t, d = h.shape
    width = d // N_BRANCH
    n_lat = w_uk.shape[0]
    a_heads = width // HEAD_DIM
    n_groups = len(B_GROUPS)
    iq_width = IDX_HEADS * IDX_DIM
    lat_col = width
    iq_col = lat_col + n_lat
    ik_col = iq_col + iq_width
    front = ik_col
    shift = IDX_DIM + IDX_HEADS
    qkv_w = 3 * n_groups * width
    rest_w = qkv_w + 2 * width + 3 * width + N_BRANCH * d
    assert front % 512 == 0 and w_in_t.shape[1] == front + shift + rest_w and shift <= LANE

    big_tm = 1024 if layer % 2 else 512
    pa = _matmul_t(h, w_in_t, layer, row0=0, n_out=front, tm=big_tm, pre_transpose=True,
                   name="proj_front")
    pik = _matmul_t(h, w_in_t, layer, row0=front, n_out=LANE, name="proj_index_key")
    rest = _matmul_t(h, w_in_t, layer, row0=front + shift, n_out=rest_w, tm=big_tm,
                     name="proj_rest")

    pa3 = pa.reshape(bsz, seq, front)
    pik3 = pik.reshape(bsz, seq, LANE)
    rest3 = rest.reshape(bsz, seq, rest_w)

    lat, latt, ikt = _latnorm(pa3, pik3, lat_g, lat_col, n_lat)
    wuk_t = jnp.transpose(w_uk, (1, 2, 0)).astype(BF16)
    wuv_t = jnp.transpose(w_uv, (1, 0, 2)).astype(BF16)
    o_a = _dsa(pa3, pik3, lat, latt, ikt, wuk_t, wuv_t, bias_a,
               q_width=width, iq_col=iq_col, iq_width=iq_width,
               n_variants=4 if layer % 2 else 1).reshape(t, width)

    outs, lses = [], []
    for g, (window, dil) in enumerate(B_GROUPS):
        o, l = _dilated(rest3, bias_b, g, n_groups, window, dil, width)
        outs.append(o)
        lses.append(l)
    o_b = _mixture(outs, lses)

    o_c = _gmlp(rest, qkv_w, width, c_ln_g, c_ln_b, c_w_s, c_b_s.T)
    o_d = _short_conv(rest3, qkv_w + 2 * width, width, d_conv).reshape(t, width)

    mixed = _merge([o_a, o_b, o_c, o_d], rest, qkv_w + 5 * width, w_branch, layer)
    return _matmul(mixed, w_out, layer, tm=big_tm, name="proj_out")


def kernel(x, c, w_ada, b_ada, ada_table, rel_bias, norm_pre_mix, norm_post_mix, norm_pre_ffn,
           norm_post_ffn, w_in, a_lat_norm, a_w_uk, a_w_uv, c_ln_g, c_ln_b, c_w_s, c_b_s, d_conv,
           w_branch, w_out, ffn_w_gate, ffn_w_up, ffn_w_down, moe_router, moe_w_gate, moe_w_up,
           moe_w_down):
    bsz, seq, d = x.shape
    depth = w_in.shape[0]
    t = bsz * seq

    c_pad = jnp.pad(c, ((0, 16 - bsz % 16 if bsz % 16 else 0), (0, 0)))
    mod_shared = _ada(c_pad, w_ada, b_ada)[:bsz].reshape(bsz, N_MOD, d)
    mods = [mod_shared + ada_table[layer] for layer in range(depth)]

    w_in_t = jnp.transpose(w_in, (0, 2, 1))
    width = d // N_BRANCH
    bias_a = _bias_tiles_a(rel_bias, width // HEAD_DIM, seq)
    bias_b = _bias_tiles_b(rel_bias, width // HEAD_DIM, width // HEAD_DIM)

    h = _prenorm(x, mods[0], norm_pre_mix[0])
    for layer in range(depth):
        mod = mods[layer]
        y = _hybrid_mixer(h.reshape(t, d), bsz, seq, layer, w_in_t, a_lat_norm[layer],
                          a_w_uk[layer], a_w_uv[layer], c_ln_g[layer], c_ln_b[layer],
                          c_w_s[layer], c_b_s[layer], d_conv[layer], w_branch, w_out,
                          bias_a, bias_b)
        j = layer // 2
        dense = layer % 2 == 0
        res = _postnorm(x, y.reshape(bsz, seq, d), mod, norm_post_mix[layer], 2, nxt=(3, 4),
                        mod_next=mod, g_pre=norm_pre_ffn[layer],
                        router_t=None if dense else moe_router[j].T)
        if dense:
            x, h = res
            a = _swiglu_up(h.reshape(t, d), ffn_w_gate[j][None], ffn_w_up[j][None])
            f = ffn_w_down.shape[1]
            tk = f // 2 if (f // 2) % LANE == 0 and f > 4096 else f
            y = _swiglu_down(a, ffn_w_down[j], tk)
        else:
            x, hp, route, rank, counts = res
            y = _moe(hp.reshape(t, d // 2), route.reshape(t, LANE), rank.reshape(t, LANE), counts,
                     j, moe_w_gate, moe_w_up, moe_w_down)
        y3 = y.reshape(bsz, seq, d)
        if layer + 1 < depth:
            x, h = _postnorm(x, y3, mod, norm_post_ffn[layer], 5, nxt=(0, 1),
                             mod_next=mods[layer + 1], g_pre=norm_pre_mix[layer + 1])
        else:
            (x,) = _postnorm(x, y3, mod, norm_post_ffn[layer], 5)
    return x
```

```python
import functools
import math

import jax
import jax.numpy as jnp
from jax import lax
from jax.experimental import pallas as pl
from jax.experimental.pallas import tpu as pltpu

F32 = jnp.float32
BF16 = jnp.bfloat16

LANE = 128
V7X_VMEM_BYTES = 64 * 1024 * 1024
VMEM_CAP = V7X_VMEM_BYTES - 8 * 1024 * 1024

HEAD_DIM = 128
N_BRANCH = 4
IDX_HEADS = 8
IDX_DIM = 64
IDX_TOPK = 256
A_BLOCK = 128
DSA_KEY_RANGE_VARIANTS = 8
B_GROUPS = ((128, 1), (512, 4), (2048, 16))
B_BLOCK = 128
C_CHUNK = 128
C_GROUPS = 8
D_CONV_WIDTH = 3
REL_BUCKETS = 32
REL_MAX_DIST = 2048
TOP_K = 2
N_MOD = 6
EPS = 1e-6
NEG_INF = float("-inf")


def _params(semantics, vmem_bytes):
    limit = int(min(VMEM_CAP, max(vmem_bytes * 5 // 4 + (4 << 20), 16 << 20)))
    return pltpu.CompilerParams(dimension_semantics=semantics, vmem_limit_bytes=limit)


def _pick_tile(n, candidates):
    for c in candidates:
        if n % c == 0:
            return c
    raise ValueError(f"no tile in {candidates} divides {n}")


def _cast_rows(src_fn, dst_ref, n_rows, chunk):
    def body(r, carry):
        rows = pl.ds(pl.multiple_of(r * chunk, chunk), chunk)
        dst_ref[rows, :] = src_fn(rows).astype(BF16)
        return carry
    lax.fori_loop(0, n_rows // chunk, body, 0)


def _mm_body(chunk, lhs_ref, w_ref, o_ref, wb_ref):
    @pl.when(pl.program_id(1) == 0)
    def _():
        _cast_rows(lambda rows: w_ref[rows, :], wb_ref, w_ref.shape[0], chunk)

    o_ref[...] = jnp.dot(lhs_ref[...], wb_ref[...],
                         preferred_element_type=F32).astype(o_ref.dtype)


def _matmul(lhs, w, layer, *, tn=512, tm=512, out_dtype=F32, name="mm"):
    m, k = lhs.shape
    n_out = w.shape[2]
    tn = min(tn, n_out)
    tm = min(tm, m)
    assert n_out % tn == 0 and m % tm == 0 and tn % LANE == 0
    chunk = _pick_tile(k, (256, 128, 64, 8))
    osz = jnp.dtype(out_dtype).itemsize
    vmem = 2 * tm * k * 2 + 2 * k * tn * 4 + k * tn * 2 + 2 * tm * tn * osz + tm * tn * 4 \
        + chunk * tn * 8
    return pl.pallas_call(
        functools.partial(_mm_body, chunk),
        out_shape=jax.ShapeDtypeStruct((m, n_out), out_dtype),
        grid=(n_out // tn, m // tm),
        in_specs=[pl.BlockSpec((tm, k), lambda j, i: (i, 0)),
                  pl.BlockSpec((None, k, tn), lambda j, i: (layer, 0, j))],
        out_specs=pl.BlockSpec((tm, tn), lambda j, i: (i, j)),
        scratch_shapes=[pltpu.VMEM((k, tn), BF16)],
        compiler_params=_params(("arbitrary", "arbitrary"), vmem),
        name=name,
    )(lhs, w)


def _mm_t_body(chunk, lhs_ref, w_ref, o_ref, wb_ref):
    @pl.when(pl.program_id(1) == 0)
    def _():
        _cast_rows(lambda rows: w_ref[rows, :], wb_ref, w_ref.shape[0], chunk)

    acc = lax.dot_general(lhs_ref[...], wb_ref[...], (((1,), (1,)), ((), ())),
                          preferred_element_type=F32)
    o_ref[...] = acc.astype(o_ref.dtype)


def _matmul_t(lhs, w_t, layer, *, row0, n_out, tn=512, tm=512, out_dtype=F32, name="mm_t"):
    m, k = lhs.shape
    tn = min(tn, n_out)
    tm = min(tm, m)
    assert n_out % tn == 0 and m % tm == 0 and row0 % 8 == 0 and tn % LANE == 0
    chunk = _pick_tile(tn, (256, 128))
    osz = jnp.dtype(out_dtype).itemsize
    vmem = 2 * tm * k * 2 + 2 * k * tn * 4 + k * tn * 2 + 2 * tm * tn * osz + tm * tn * 4 \
        + 4 * chunk * k * 4
    return pl.pallas_call(
        functools.partial(_mm_t_body, chunk),
        out_shape=jax.ShapeDtypeStruct((m, n_out), out_dtype),
        grid=(n_out // tn, m // tm),
        in_specs=[pl.BlockSpec((tm, k), lambda j, i: (i, 0)),
                  pl.BlockSpec((None, pl.Element(tn), pl.Element(k)),
                               lambda j, i: (layer, pl.multiple_of(row0 + j * tn, 8), 0))],
        out_specs=pl.BlockSpec((tm, tn), lambda j, i: (i, j)),
        scratch_shapes=[pltpu.VMEM((tn, k), BF16)],
        compiler_params=_params(("arbitrary", "arbitrary"), vmem),
        name=name,
    )(lhs, w_t)


def _ada_body(c_ref, w_ref, b_ref, o_ref):
    c = c_ref[...]
    a = (c * jax.nn.sigmoid(c)).astype(BF16)
    o_ref[...] = jnp.dot(a, w_ref[...].astype(BF16), preferred_element_type=F32) + b_ref[...]


def _ada(c_pad, w_ada, b_ada, tn=512):
    m, k = c_pad.shape
    n = w_ada.shape[1]
    tn = _pick_tile(n, (tn, 256, 128))
    vmem = 2 * k * tn * 4 + k * tn * 2 + 4 * m * k * 4
    return pl.pallas_call(
        _ada_body,
        out_shape=jax.ShapeDtypeStruct((m, n), F32),
        grid=(n // tn,),
        in_specs=[pl.BlockSpec((m, k), lambda j: (0, 0)),
                  pl.BlockSpec((k, tn), lambda j: (0, j)),
                  pl.BlockSpec((1, tn), lambda j: (0, j))],
        out_specs=pl.BlockSpec((m, tn), lambda j: (0, j)),
        compiler_params=_params(("arbitrary",), vmem),
        name="ada",
    )(c_pad, w_ada, b_ada.reshape(1, n))


def _rms(x, g):
    return x * lax.rsqrt(jnp.mean(x * x, axis=-1, keepdims=True) + EPS) * g


def _route(h, router_ref, n_experts):
    rows = h.shape[0]
    lane = lax.broadcasted_iota(jnp.int32, (rows, LANE), 1)
    logits = jnp.full((rows, LANE), NEG_INF, F32)
    for e in range(n_experts):
        le = jnp.sum(h * router_ref[e:e + 1, :], axis=-1, keepdims=True)
        logits = jnp.where(lane == e, le, logits)
    m1 = jnp.max(logits, axis=-1, keepdims=True)
    i1 = jnp.min(jnp.where(logits == m1, lane, LANE), axis=-1, keepdims=True)
    rest = jnp.where(lane == i1, NEG_INF, logits)
    m2 = jnp.max(rest, axis=-1, keepdims=True)
    i2 = jnp.min(jnp.where(rest == m2, lane, LANE), axis=-1, keepdims=True)
    e2 = jnp.exp(m2 - m1)
    den = 1.0 + e2
    member = jnp.where(lane == i1, 1.0, jnp.where(lane == i2, 1.0, 0.0))
    extra = jnp.where(lane == n_experts, i1.astype(F32),
                      jnp.where(lane == n_experts + 1, i2.astype(F32),
                                jnp.where(lane == n_experts + 2, 1.0 / den, e2 / den)))
    return jnp.where(lane < n_experts, member, jnp.where(lane < n_experts + 4, extra, 0.0))


def _pack_bf16_pairs(h):
    half = h.shape[1] // 2
    lo = pltpu.bitcast(h[:, :half].astype(BF16).astype(F32), jnp.uint32)
    hi = pltpu.bitcast(h[:, half:].astype(BF16).astype(F32), jnp.uint32)
    return (hi & jnp.uint32(0xFFFF0000)) | (lo >> 16)


def _unpack_bf16_pairs(u):
    lo = pltpu.bitcast(u << 16, F32).astype(BF16)
    hi = pltpu.bitcast(u & jnp.uint32(0xFFFF0000), F32).astype(BF16)
    return lo, hi


def _pre_body(x_ref, mod_ref, g_ref, h_ref):
    h = _rms(x_ref[...], g_ref[...]) * (1.0 + mod_ref[1:2, :]) + mod_ref[0:1, :]
    h_ref[...] = h.astype(h_ref.dtype)


def _prenorm(x3, mod, g, tr=256):
    b, s, d = x3.shape
    tr = min(tr, s)
    vmem = 2 * tr * d * 4 + 2 * tr * d * 2 + 4 * tr * d * 4
    return pl.pallas_call(
        _pre_body,
        out_shape=jax.ShapeDtypeStruct((b, s, d), BF16),
        grid=(b, s // tr),
        in_specs=[pl.BlockSpec((None, tr, d), lambda bi, i: (bi, i, 0)),
                  pl.BlockSpec((None, N_MOD, d), lambda bi, i: (bi, 0, 0)),
                  pl.BlockSpec((1, d), lambda bi, i: (0, 0))],
        out_specs=pl.BlockSpec((None, tr, d), lambda bi, i: (bi, i, 0)),
        compiler_params=_params(("arbitrary", "arbitrary"), vmem),
        name="prenorm",
    )(x3, mod, g.reshape(1, d))


def _post_body(gate_row, nxt, n_experts, *refs):
    x_ref, y_ref, mod_ref, gpost_ref = refs[:4]
    pos = 4
    if nxt is not None:
        modn_ref, gpre_ref = refs[pos:pos + 2]
        pos += 2
    if n_experts:
        router_ref = refs[pos]
        pos += 1
    xo_ref = refs[pos]
    pos += 1
    xn = x_ref[...] + mod_ref[gate_row:gate_row + 1, :] * _rms(y_ref[...], gpost_ref[...])
    xo_ref[...] = xn
    if nxt is not None:
        shift_row, scale_row = nxt
        h = _rms(xn, gpre_ref[...]) * (1.0 + modn_ref[scale_row:scale_row + 1, :]) \
            + modn_ref[shift_row:shift_row + 1, :]
        if not n_experts:
            refs[pos][...] = h.astype(BF16)
            return
        hp_ref, route_ref, rank_ref, counts_ref, carry_ref = refs[pos:pos + 5]
        hp_ref[...] = _pack_bf16_pairs(h)
        route = _route(h, router_ref, n_experts)
        route_ref[...] = route

        @pl.when((pl.program_id(0) == 0) & (pl.program_id(1) == 0))
        def _():
            carry_ref[...] = jnp.zeros_like(carry_ref)

        rows = route.shape[0]
        lane = lax.broadcasted_iota(jnp.int32, route.shape, 1)
        member = jnp.where(lane < n_experts, route, 0.0)
        ri = lax.broadcasted_iota(jnp.int32, (rows, rows), 0)
        ci = lax.broadcasted_iota(jnp.int32, (rows, rows), 1)
        earlier = jnp.where(ci < ri, 1.0, 0.0).astype(BF16)
        within = jnp.dot(earlier, member.astype(BF16), preferred_element_type=F32)
        rank_ref[...] = within + carry_ref[0:1, :]
        carry_ref[...] = carry_ref[...] + jnp.sum(member, axis=0, keepdims=True)
        counts_ref[...] = carry_ref[...]


def _postnorm(x3, y3, mod, g_post, gate_row, *, nxt=None, mod_next=None, g_pre=None,
              router_t=None, tr=256):
    b, s, d = x3.shape
    tr = min(tr, s)
    n_experts = 0 if router_t is None else router_t.shape[0]
    row = lambda bi, i: (bi, i, 0)
    mod_spec = pl.BlockSpec((None, N_MOD, d), lambda bi, i: (bi, 0, 0))
    in_specs = [pl.BlockSpec((None, tr, d), row), pl.BlockSpec((None, tr, d), row), mod_spec,
                pl.BlockSpec((1, d), lambda bi, i: (0, 0))]
    args = [x3, y3, mod, g_post.reshape(1, d)]
    out_shape = [jax.ShapeDtypeStruct((b, s, d), F32)]
    out_specs = [pl.BlockSpec((None, tr, d), row)]
    scratch = []
    if nxt is not None:
        in_specs += [mod_spec, pl.BlockSpec((1, d), lambda bi, i: (0, 0))]
        args += [mod_next, g_pre.reshape(1, d)]
    if nxt is not None and not n_experts:
        out_shape.append(jax.ShapeDtypeStruct((b, s, d), BF16))
        out_specs.append(pl.BlockSpec((None, tr, d), row))
    if n_experts:
        assert nxt is not None and n_experts + 4 <= LANE
        in_specs.append(pl.BlockSpec((n_experts, d), lambda bi, i: (0, 0)))
        args.append(router_t)
        out_shape += [jax.ShapeDtypeStruct((b, s, d // 2), jnp.uint32),
                      jax.ShapeDtypeStruct((b, s, LANE), F32),
                      jax.ShapeDtypeStruct((b, s, LANE), F32),
                      jax.ShapeDtypeStruct((8, LANE), F32)]
        out_specs += [pl.BlockSpec((None, tr, d // 2), row), pl.BlockSpec((None, tr, LANE), row),
                      pl.BlockSpec((None, tr, LANE), row),
                      pl.BlockSpec((8, LANE), lambda bi, i: (0, 0))]
        scratch.append(pltpu.VMEM((8, LANE), F32))
    vmem = 2 * 3 * tr * d * 4 + 2 * tr * d * 2 + 6 * tr * d * 4
    return pl.pallas_call(
        functools.partial(_post_body, gate_row, nxt, n_experts),
        out_shape=out_shape,
        grid=(b, s // tr),
        in_specs=in_specs,
        out_specs=out_specs,
        scratch_shapes=scratch,
        compiler_params=_params(("arbitrary", "arbitrary"), vmem),
        name="postnorm",
    )(*args)


def _lat_body(alat_ref, pik_ref, g_ref, lat_ref, latt_ref, ikt_ref):
    lat = _rms(alat_ref[...], g_ref[...])
    lat_ref[...] = lat.astype(BF16)
    latt_ref[...] = lat.T.astype(BF16)
    ikt_ref[...] = pik_ref[...].T.astype(BF16)


def _latnorm(pa3, pik3, g_lat, lat_col, n_lat, tr=512):
    b, s, _ = pa3.shape
    tr = min(tr, s)
    assert lat_col % n_lat == 0
    vmem = 2 * tr * (n_lat + LANE) * 4 + 4 * tr * (n_lat + LANE) * 2 + 4 * tr * n_lat * 4
    return pl.pallas_call(
        _lat_body,
        out_shape=[jax.ShapeDtypeStruct((b, s, n_lat), BF16),
                   jax.ShapeDtypeStruct((b, n_lat, s), BF16),
                   jax.ShapeDtypeStruct((b, LANE, s), BF16)],
        grid=(b, s // tr),
        in_specs=[pl.BlockSpec((None, tr, n_lat), lambda bi, i: (bi, i, lat_col // n_lat)),
                  pl.BlockSpec((None, tr, LANE), lambda bi, i: (bi, i, 0)),
                  pl.BlockSpec((1, n_lat), lambda bi, i: (0, 0))],
        out_specs=[pl.BlockSpec((None, tr, n_lat), lambda bi, i: (bi, i, 0)),
                   pl.BlockSpec((None, n_lat, tr), lambda bi, i: (bi, 0, i)),
                   pl.BlockSpec((None, LANE, tr), lambda bi, i: (bi, 0, i))],
        compiler_params=_params(("arbitrary", "arbitrary"), vmem),
        name="latnorm",
    )(pa3, pik3, g_lat.reshape(1, n_lat))


def _dsa_body(topk, n_variants, *refs):
    i = pl.program_id(1)
    n_blk = refs[4].shape[0] // A_BLOCK
    per = -(-n_blk // n_variants)
    for v in range(n_variants):
        blocks = min(n_blk, (v + 1) * per)

        @pl.when((i >= v * per) & (i < (v + 1) * per))
        def _(blocks=blocks):
            _dsa_compute(topk, blocks * A_BLOCK, i, *refs)


def _dsa_compute(topk, seq, i, q_ref, iq_ref, iw_ref, ikt_ref, lat_ref, latt_ref, wuk_ref, wuv_ref,
                 bias_ref, o_ref, madd_ref):
    n_blk = seq // A_BLOCK
    n_heads = wuk_ref.shape[0]
    int_min = jnp.int32(-2 ** 31)
    qpos = i * A_BLOCK + lax.broadcasted_iota(jnp.int32, (A_BLOCK, 1), 0)
    kpos = lax.broadcasted_iota(jnp.int32, (A_BLOCK, seq), 1)

    iq = iq_ref[...].astype(BF16)
    iw = iw_ref[...]
    ikt = ikt_ref[0:IDX_DIM, 0:seq]
    score = jnp.zeros((A_BLOCK, seq), F32)
    for h in range(IDX_HEADS):
        l = jnp.dot(iq[:, h * IDX_DIM:(h + 1) * IDX_DIM], ikt, preferred_element_type=F32)
        score = score + iw[:, IDX_DIM + h:IDX_DIM + h + 1] * jnp.maximum(l, 0.0)

    score = jnp.where(score == 0.0, 0.0, score)
    key = pltpu.bitcast(score, jnp.int32)
    key = jnp.where(key < 0, key ^ jnp.int32(0x7FFFFFFF), key)
    key = jnp.where(kpos <= qpos, key, int_min)
    k_eff = jnp.minimum(qpos + 1, topk).astype(F32)

    def count(mask):
        return jnp.sum(jnp.where(mask, 1.0, 0.0), axis=1, keepdims=True)

    t0 = jnp.where(count(key >= 0) >= k_eff, jnp.int32(0), int_min)

    def value_step(it, t):
        c = t | (jnp.int32(1) << (30 - it))
        return jnp.where(count(key >= c) >= k_eff, c, t)

    t = lax.fori_loop(0, 31, value_step, t0)

    need = k_eff - count(key > t)
    eq = key == t
    idx_bits = max(1, (seq - 1).bit_length())

    def index_step(it, p):
        c = p | (jnp.int32(1) << (idx_bits - 1 - it))
        below = jnp.sum(jnp.where(eq, jnp.where(kpos < c, 1.0, 0.0), 0.0), axis=1, keepdims=True)
        return jnp.where(below < need, c, p)

    p = lax.fori_loop(0, idx_bits, index_step, jnp.zeros((A_BLOCK, 1), jnp.int32))
    tie_add = jnp.where(eq, jnp.where(kpos <= p, 0.0, NEG_INF), NEG_INF)
    madd_ref[:, 0:seq] = jnp.where(key > t, 0.0, tie_add)

    scale = HEAD_DIM ** -0.5
    for h in range(n_heads):
        cols = slice(h * HEAD_DIM, (h + 1) * HEAD_DIM)
        qa = jnp.dot(q_ref[:, cols].astype(BF16), wuk_ref[h],
                     preferred_element_type=F32).astype(BF16)
        lg = jnp.dot(qa, latt_ref[:, 0:seq], preferred_element_type=F32) * scale
        bias = jnp.concatenate([bias_ref[jnp.maximum(i - j, 0), h] for j in range(n_blk)], axis=1)
        lg = lg + bias + madd_ref[:, 0:seq]
        m = jnp.max(lg, axis=1, keepdims=True)
        e = jnp.exp(lg - m)
        den = jnp.sum(e, axis=1, keepdims=True)
        o_lat = jnp.dot(e.astype(BF16), lat_ref[0:seq, :], preferred_element_type=F32) / den
        o_ref[:, cols] = jnp.dot(o_lat.astype(BF16), wuv_ref[h],
                                 preferred_element_type=F32).astype(o_ref.dtype)


def _dsa(pa3, pik3, lat, latt, ikt, wuk_t, wuv_t, bias_a, *, q_width, iq_col, iq_width,
         n_variants):
    b, s, _ = pa3.shape
    n_lat = lat.shape[-1]
    n_heads = wuk_t.shape[0]
    n_blk = s // A_BLOCK
    topk = min(IDX_TOPK, s // 4)
    assert iq_col % iq_width == 0
    vmem = 2 * (A_BLOCK * (q_width + iq_width + LANE) * 4 + LANE * s * 2 + 2 * s * n_lat * 2
                + 2 * n_heads * HEAD_DIM * n_lat * 2 + n_heads * n_blk * A_BLOCK * A_BLOCK * 4
                + A_BLOCK * q_width * 2) + 10 * A_BLOCK * s * 4
    return pl.pallas_call(
        functools.partial(_dsa_body, topk, n_variants),
        out_shape=jax.ShapeDtypeStruct((b, s, q_width), BF16),
        grid=(b, n_blk),
        in_specs=[pl.BlockSpec((None, A_BLOCK, q_width), lambda bi, i: (bi, i, 0)),
                  pl.BlockSpec((None, A_BLOCK, iq_width), lambda bi, i: (bi, i, iq_col // iq_width)),
                  pl.BlockSpec((None, A_BLOCK, LANE), lambda bi, i: (bi, i, 0)),
                  pl.BlockSpec((None, LANE, s), lambda bi, i: (bi, 0, 0)),
                  pl.BlockSpec((None, s, n_lat), lambda bi, i: (bi, 0, 0)),
                  pl.BlockSpec((None, n_lat, s), lambda bi, i: (bi, 0, 0)),
                  pl.BlockSpec(wuk_t.shape, lambda bi, i: (0, 0, 0)),
                  pl.BlockSpec(wuv_t.shape, lambda bi, i: (0, 0, 0)),
                  pl.BlockSpec(bias_a.shape, lambda bi, i: (0, 0, 0, 0))],
        out_specs=pl.BlockSpec((None, A_BLOCK, q_width), lambda bi, i: (bi, i, 0)),
        scratch_shapes=[pltpu.VMEM((A_BLOCK, s), F32)],
        compiler_params=_params(("arbitrary", "arbitrary"), vmem),
        name="dsa",
    )(pa3, pa3, pik3, ikt, lat, latt, wuk_t, wuv_t, bias_a)


def _dil_body(span, dil, hb, has_prev, *refs):
    if has_prev:
        q_ref, kp_ref, kc_ref, vp_ref, vc_ref, bias_ref, o_ref, lse_ref = refs
    else:
        q_ref, kc_ref, vc_ref, bias_ref, o_ref, lse_ref = refs
    i = pl.program_id(1)
    hblk = pl.program_id(2)
    n_keys = 2 * B_BLOCK if has_prev else B_BLOCK
    key0 = 0 if has_prev else B_BLOCK
    qi = lax.broadcasted_iota(jnp.int32, (B_BLOCK, n_keys), 0)
    kj = lax.broadcasted_iota(jnp.int32, (B_BLOCK, n_keys), 1) + key0
    sub_dist = qi + B_BLOCK - kj
    in_band = jnp.where(sub_dist <= span,
                        jnp.where((i - 1) * B_BLOCK + kj >= 0, 0.0, NEG_INF), NEG_INF)
    madd = jnp.where(sub_dist >= 0, in_band, NEG_INF)
    lane = lax.broadcasted_iota(jnp.int32, (B_BLOCK, LANE), 1)
    scale = HEAD_DIM ** -0.5

    @pl.when(hblk == 0)
    def _():
        lse_ref[...] = jnp.zeros_like(lse_ref)

    def one_subsequence(r, carry):
        rows = pl.ds(r, B_BLOCK, stride=dil) if dil > 1 else slice(None)
        lse_acc = lse_ref[rows, :]
        for h in range(hb):
            cols = slice(h * HEAD_DIM, (h + 1) * HEAD_DIM)
            q = q_ref[rows, cols].astype(BF16)
            if has_prev:
                k2 = jnp.concatenate([kp_ref[rows, cols], kc_ref[rows, cols]], axis=0).astype(BF16)
                v2 = jnp.concatenate([vp_ref[rows, cols], vc_ref[rows, cols]], axis=0).astype(BF16)
            else:
                k2 = kc_ref[rows, cols].astype(BF16)
                v2 = vc_ref[rows, cols].astype(BF16)
            lg = lax.dot_general(q, k2, (((1,), (1,)), ((), ())), preferred_element_type=F32)
            lg = lg * scale + bias_ref[h, :, key0:] + madd
            m = jnp.max(lg, axis=1, keepdims=True)
            e = jnp.exp(lg - m)
            den = jnp.sum(e, axis=1, keepdims=True)
            o_ref[rows, cols] = jnp.dot((e / den).astype(BF16), v2, preferred_element_type=F32)
            lse_acc = jnp.where(lane == hblk * hb + h, m + jnp.log(den), lse_acc)
        lse_ref[rows, :] = lse_acc
        return carry

    if dil > 1:
        lax.fori_loop(0, dil, one_subsequence, 0)
    else:
        one_subsequence(0, 0)


def _dilated(rest3, bias_b, g, n_groups, window, dil, width):
    b, s, w_all = rest3.shape
    n_heads = width // HEAD_DIM
    span = window // dil
    n_sub = s // dil
    assert span == B_BLOCK and n_sub % B_BLOCK == 0 and w_all % width == 0
    nb = n_sub // B_BLOCK
    rows = B_BLOCK * dil
    hb = n_heads if dil == 1 else 1
    assert HEAD_DIM == LANE
    n_hblk = n_heads // hb
    has_prev = nb > 1
    cur = lambda slot: (lambda bi, i, hk: (bi, i, (slot * n_groups + g) * n_hblk + hk))
    prev = lambda slot: (lambda bi, i, hk: (bi, jnp.maximum(i - 1, 0),
                                            (slot * n_groups + g) * n_hblk + hk))
    blk = (None, rows, hb * HEAD_DIM)
    in_specs = [pl.BlockSpec(blk, cur(0))]
    for slot in (1, 2):
        if has_prev:
            in_specs.append(pl.BlockSpec(blk, prev(slot)))
        in_specs.append(pl.BlockSpec(blk, cur(slot)))
    in_specs.append(pl.BlockSpec((None, hb, B_BLOCK, 2 * B_BLOCK), lambda bi, i, hk: (g, hk, 0, 0)))
    vmem = 2 * (len(in_specs) * rows * hb * HEAD_DIM + rows * LANE) * 4 \
        + 2 * hb * B_BLOCK * 2 * B_BLOCK * 4 + (4 << 20)
    o, lse = pl.pallas_call(
        functools.partial(_dil_body, span, dil, hb, has_prev),
        out_shape=[jax.ShapeDtypeStruct((b, s, width), F32),
                   jax.ShapeDtypeStruct((b, s, LANE), F32)],
        grid=(b, nb, n_hblk),
        in_specs=in_specs,
        out_specs=[pl.BlockSpec(blk, lambda bi, i, hk: (bi, i, hk)),
                   pl.BlockSpec((None, rows, LANE), lambda bi, i, hk: (bi, i, 0))],
        compiler_params=_params(("arbitrary", "arbitrary", "arbitrary"), vmem),
        name=f"dilated{g}",
    )(*([rest3] * (len(in_specs) - 1)), bias_b)
    return o.reshape(b * s, width), lse.reshape(b * s, LANE)


def _mix_body(n_groups, n_heads, *refs):
    o_refs = refs[:n_groups]
    l_refs = refs[n_groups:2 * n_groups]
    out_ref = refs[2 * n_groups]
    ls = [r[...] for r in l_refs]
    m = functools.reduce(jnp.maximum, ls)
    es = [jnp.exp(l - m) for l in ls]
    tot = functools.reduce(jnp.add, es)
    ws = [e / tot for e in es]
    for h in range(n_heads):
        cols = slice(h * HEAD_DIM, (h + 1) * HEAD_DIM)
        acc = ws[0][:, h:h + 1] * o_refs[0][:, cols]
        for gi in range(1, n_groups):
            acc = acc + ws[gi][:, h:h + 1] * o_refs[gi][:, cols]
        out_ref[:, cols] = acc.astype(out_ref.dtype)


def _mixture(outs, lses, tr=512):
    t, width = outs[0].shape
    tr = min(tr, t)
    n_groups = len(outs)
    vmem = 2 * n_groups * tr * (width + LANE) * 4 + 2 * tr * width * 2 + 4 * tr * width * 4
    return pl.pallas_call(
        functools.partial(_mix_body, n_groups, width // HEAD_DIM),
        out_shape=jax.ShapeDtypeStruct((t, width), BF16),
        grid=(t // tr,),
        in_specs=[pl.BlockSpec((tr, width), lambda i: (i, 0))] * n_groups
        + [pl.BlockSpec((tr, LANE), lambda i: (i, 0))] * n_groups,
        out_specs=pl.BlockSpec((tr, width), lambda i: (i, 0)),
        compiler_params=_params(("arbitrary",), vmem),
        name="mixture",
    )(*outs, *lses)


def _gmlp_body(u_ref, v_ref, g_ref, b_ref, ws_ref, bs_ref, o_ref):
    u = jax.nn.gelu(u_ref[...])
    v = jax.nn.gelu(v_ref[...])
    mu = jnp.mean(v, axis=-1, keepdims=True)
    var = jnp.mean(jnp.square(v - mu), axis=-1, keepdims=True)
    v = (v - mu) * lax.rsqrt(var + EPS) * g_ref[...] + b_ref[...]
    ti = lax.broadcasted_iota(jnp.int32, (C_CHUNK, C_CHUNK), 0)
    si = lax.broadcasted_iota(jnp.int32, (C_CHUNK, C_CHUNK), 1)
    causal = si <= ti
    gw = u.shape[1] // C_GROUPS
    for g in range(C_GROUPS):
        cols = slice(g * gw, (g + 1) * gw)
        w = jnp.where(causal, ws_ref[g], 0.0).astype(BF16)
        mixed = jnp.dot(w, v[:, cols].astype(BF16), preferred_element_type=F32) + bs_ref[:, g:g + 1]
        o_ref[:, cols] = (u[:, cols] * mixed).astype(o_ref.dtype)


def _gmlp(rest, u_col, width, ln_g, ln_b, w_s, b_s_t):
    t = rest.shape[0]
    assert u_col % width == 0 and (width // C_GROUPS) % LANE == 0
    vmem = 2 * 2 * C_CHUNK * width * 4 + 2 * C_CHUNK * width * 2 + 2 * C_GROUPS * C_CHUNK * C_CHUNK * 4 \
        + 8 * C_CHUNK * width * 4
    return pl.pallas_call(
        _gmlp_body,
        out_shape=jax.ShapeDtypeStruct((t, width), BF16),
        grid=(t // C_CHUNK,),
        in_specs=[pl.BlockSpec((C_CHUNK, width), lambda i: (i, u_col // width)),
                  pl.BlockSpec((C_CHUNK, width), lambda i: (i, u_col // width + 1)),
                  pl.BlockSpec((1, width), lambda i: (0, 0)),
                  pl.BlockSpec((1, width), lambda i: (0, 0)),
                  pl.BlockSpec(w_s.shape, lambda i: (0, 0, 0)),
                  pl.BlockSpec(b_s_t.shape, lambda i: (0, 0))],
        out_specs=pl.BlockSpec((C_CHUNK, width), lambda i: (i, 0)),
        compiler_params=_params(("arbitrary",), vmem),
        name="gmlp",
    )(rest, rest, ln_g.reshape(1, width), ln_b.reshape(1, width), w_s, b_s_t)


def _conv_body(h_ref, bg_ref, cg_ref, w_ref, o_ref, carry_ref):
    @pl.when(pl.program_id(1) == 0)
    def _():
        carry_ref[...] = jnp.zeros_like(carry_ref)

    ch = cg_ref[...] * h_ref[...]
    rows = ch.shape[0]
    row = lax.broadcasted_iota(jnp.int32, ch.shape, 0)
    pm1 = carry_ref[7:8, :]
    pm2 = carry_ref[6:7, :]
    s1 = jnp.where(row == 0, pm1, pltpu.roll(ch, 1, 0))
    s2 = jnp.where(row == 0, pm2, jnp.where(row == 1, pm1, pltpu.roll(ch, 2, 0)))
    z = w_ref[0:1, :] * s2 + w_ref[1:2, :] * s1 + w_ref[2:3, :] * ch
    o_ref[...] = (bg_ref[...] * z).astype(o_ref.dtype)
    carry_ref[...] = ch[rows - 8:, :]


def _short_conv(rest3, h_col, width, conv_w, tr=256):
    b, s, _ = rest3.shape
    tr = min(tr, s)
    assert h_col % width == 0 and conv_w.shape[0] == D_CONV_WIDTH
    cb = h_col // width
    vmem = 2 * 3 * tr * width * 4 + 2 * tr * width * 2 + 6 * tr * width * 4
    return pl.pallas_call(
        _conv_body,
        out_shape=jax.ShapeDtypeStruct((b, s, width), BF16),
        grid=(b, s // tr),
        in_specs=[pl.BlockSpec((None, tr, width), lambda bi, i: (bi, i, cb)),
                  pl.BlockSpec((None, tr, width), lambda bi, i: (bi, i, cb + 1)),
                  pl.BlockSpec((None, tr, width), lambda bi, i: (bi, i, cb + 2)),
                  pl.BlockSpec(conv_w.shape, lambda bi, i: (0, 0))],
        out_specs=pl.BlockSpec((None, tr, width), lambda bi, i: (bi, i, 0)),
        scratch_shapes=[pltpu.VMEM((8, width), F32)],
        compiler_params=_params(("arbitrary", "arbitrary"), vmem),
        name="short_conv",
    )(rest3, rest3, rest3, conv_w)


def _merge_body(chunk, *refs):
    br_refs = refs[:N_BRANCH]
    gate_refs = refs[N_BRANCH:2 * N_BRANCH]
    w_ref, o_ref, wb_ref = refs[2 * N_BRANCH:]

    @pl.when(pl.program_id(1) == 0)
    def _():
        for n in range(N_BRANCH):
            _cast_rows(lambda rows, n=n: w_ref[n, rows, :], wb_ref.at[n], w_ref.shape[1], chunk)

    acc = None
    for n in range(N_BRANCH):
        proj = jnp.dot(br_refs[n][...], wb_ref[n], preferred_element_type=F32)
        term = jax.nn.sigmoid(gate_refs[n][...]) * proj
        acc = term if acc is None else acc + term
    o_ref[...] = acc.astype(o_ref.dtype)


def _merge(branches, rest, gate_col, w_branch, layer, tn=512, tm=512):
    t, width = branches[0].shape
    d = w_branch.shape[-1]
    tn = min(tn, d)
    tm = min(tm, t)
    assert d % tn == 0 and t % tm == 0 and gate_col % tn == 0
    chunk = _pick_tile(width, (256, 128, 8))
    gate_spec = lambda n: pl.BlockSpec((tm, tn), lambda j, i: (i, (gate_col + n * d) // tn + j))
    vmem = 2 * N_BRANCH * (tm * width * 2 + tm * tn * 4 + width * tn * 4) + N_BRANCH * width * tn * 2 \
        + 2 * tm * tn * 2 + 4 * tm * tn * 4
    return pl.pallas_call(
        functools.partial(_merge_body, chunk),
        out_shape=jax.ShapeDtypeStruct((t, d), BF16),
        grid=(d // tn, t // tm),
        in_specs=[pl.BlockSpec((tm, width), lambda j, i: (i, 0))] * N_BRANCH
        + [gate_spec(n) for n in range(N_BRANCH)]
        + [pl.BlockSpec((None, N_BRANCH, width, tn), lambda j, i: (layer, 0, 0, j))],
        out_specs=pl.BlockSpec((tm, tn), lambda j, i: (i, j)),
        scratch_shapes=[pltpu.VMEM((N_BRANCH, width, tn), BF16)],
        compiler_params=_params(("arbitrary", "arbitrary"), vmem),
        name="merge",
    )(*branches, *([rest] * N_BRANCH), w_branch)


def _up_body(chunk, h_ref, wg_ref, wu_ref, o_ref, wgb_ref, wub_ref):
    @pl.when(pl.program_id(1) == 0)
    def _():
        _cast_rows(lambda rows: wg_ref[rows, :], wgb_ref, wg_ref.shape[0], chunk)
        _cast_rows(lambda rows: wu_ref[rows, :], wub_ref, wu_ref.shape[0], chunk)

    h = h_ref[...]
    g = jnp.dot(h, wgb_ref[...], preferred_element_type=F32)
    u = jnp.dot(h, wub_ref[...], preferred_element_type=F32)
    o_ref[...] = (g * jax.nn.sigmoid(g) * u).astype(o_ref.dtype)


def _swiglu_up(h, wg, wu, tm=512):
    t, k = h.shape
    n_e, _, f = wg.shape
    tn = _pick_tile(f, (256, 128))
    tm = min(tm, t)
    nj = f // tn
    chunk = _pick_tile(k, (256, 128, 8))
    w_spec = pl.BlockSpec((None, k, tn), lambda j, i: (j // nj, 0, j % nj))
    vmem = 2 * tm * k * 2 + 4 * k * tn * 4 + 2 * k * tn * 2 + 2 * tm * tn * 2 + 4 * tm * tn * 4
    return pl.pallas_call(
        functools.partial(_up_body, chunk),
        out_shape=jax.ShapeDtypeStruct((t, n_e * f), BF16),
        grid=(n_e * nj, t // tm),
        in_specs=[pl.BlockSpec((tm, k), lambda j, i: (i, 0)), w_spec, w_spec],
        out_specs=pl.BlockSpec((tm, tn), lambda j, i: (i, j)),
        scratch_shapes=[pltpu.VMEM((k, tn), BF16), pltpu.VMEM((k, tn), BF16)],
        compiler_params=_params(("arbitrary", "arbitrary"), vmem),
        name="swiglu_up",
    )(h, wg, wu)


def _down_body(tm, chunk, a_ref, w_ref, o_ref, wb_ref):
    kk = pl.program_id(1)
    i = pl.program_id(2)

    @pl.when(i == 0)
    def _():
        _cast_rows(lambda rows: w_ref[rows, :], wb_ref, w_ref.shape[0], chunk)

    part = jnp.dot(a_ref[...], wb_ref[...], preferred_element_type=F32)
    rows = pl.ds(pl.multiple_of(i * tm, tm), tm)

    @pl.when(kk == 0)
    def _():
        o_ref[rows, :] = part

    @pl.when(kk > 0)
    def _():
        o_ref[rows, :] += part


def _swiglu_down(a, wd, tk, tn=256, tm=512):
    t, k = a.shape
    n = wd.shape[1]
    tn = min(tn, n)
    tm = min(tm, t)
    assert k % tk == 0 and n % tn == 0 and t % tm == 0
    chunk = _pick_tile(tk, (256, 128, 8))
    in_specs = [pl.BlockSpec((tm, tk), lambda j, kk, i: (i, kk)),
                pl.BlockSpec((tk, tn), lambda j, kk, i: (kk, j))]
    args = [a, wd]
    vmem = 2 * tm * tk * 2 + 2 * tk * tn * 4 + tk * tn * 2 + 2 * t * tn * 4 + 4 * tm * tn * 4
    return pl.pallas_call(
        functools.partial(_down_body, tm, chunk),
        out_shape=jax.ShapeDtypeStruct((t, n), F32),
        grid=(n // tn, k // tk, t // tm),
        in_specs=in_specs,
        out_specs=pl.BlockSpec((t, tn), lambda j, kk, i: (0, j)),
        scratch_shapes=[pltpu.VMEM((tk, tn), BF16)],
        compiler_params=_params(("arbitrary", "arbitrary", "arbitrary"), vmem),
        name="swiglu_down",
    )(*args)


MOE_TILE = 256


def _dispatch_body(n_tok, pos1_ref, pos2_ref, h_ref, o_ref, src_ref, buf_ref, sem):
    r = pl.program_id(0)

    def row_copy(tile, slot, k):
        return pltpu.make_async_copy(h_ref.at[pl.ds(src_ref[tile * MOE_TILE + k], 1), :],
                                     buf_ref.at[slot, pl.ds(k, 1), :], sem.at[slot])

    def start_tile(tile, slot):
        lax.fori_loop(0, MOE_TILE, lambda k, c: (row_copy(tile, slot, k).start(), c)[1], 0)

    @pl.when(r == 0)
    def _():
        def clear(p, c):
            src_ref[p] = 0
            return c

        def invert(t, c):
            src_ref[pos1_ref[t]] = t
            src_ref[pos2_ref[t]] = t
            return c

        lax.fori_loop(0, src_ref.shape[0], clear, 0)
        lax.fori_loop(0, n_tok, invert, 0)
        start_tile(0, 0)

    @pl.when(r + 1 < pl.num_programs(0))
    def _():
        start_tile(r + 1, (r + 1) % 2)

    slot = r % 2
    lax.fori_loop(0, MOE_TILE, lambda k, c: (row_copy(r, slot, k).wait(), c)[1], 0)
    o_ref[...] = buf_ref[slot]


def _moe_dispatch(hp, pos1, pos2, n_rows):
    t, w = hp.shape
    assert n_rows % MOE_TILE == 0
    return pl.pallas_call(
        functools.partial(_dispatch_body, t),
        out_shape=jax.ShapeDtypeStruct((n_rows, w), hp.dtype),
        grid_spec=pltpu.PrefetchScalarGridSpec(
            num_scalar_prefetch=2,
            grid=(n_rows // MOE_TILE,),
            in_specs=[pl.BlockSpec(memory_space=pl.ANY)],
            out_specs=pl.BlockSpec((MOE_TILE, w), lambda r, p1, p2: (r, 0)),
            scratch_shapes=[pltpu.SMEM((n_rows,), jnp.int32),
                            pltpu.VMEM((2, MOE_TILE, w), hp.dtype),
                            pltpu.SemaphoreType.DMA((2,))]),
        compiler_params=_params(("arbitrary",), 6 * MOE_TILE * w * 4),
        name="moe_dispatch",
    )(pos1, pos2, hp)


def _group_body(swiglu, chunk, te_ref, nused_ref, *refs):
    if swiglu:
        x_ref, wg_ref, wu_ref, o_ref, wgb_ref, wub_ref = refs
        pairs = ((wg_ref, wgb_ref), (wu_ref, wub_ref))
    else:
        x_ref, w_ref, o_ref, wb_ref = refs
        pairs = ((w_ref, wb_ref),)
    r = pl.program_id(1)
    expert = te_ref[r]
    prev = te_ref[jnp.maximum(r - 1, 0)]

    @pl.when(jnp.logical_or(r == 0, expert != prev))
    def _():
        for src_ref, dst_ref in pairs:
            _cast_rows(lambda rows, s=src_ref: s[rows, :], dst_ref, src_ref.shape[0], chunk)

    @pl.when(r < nused_ref[0])
    def _():
        if swiglu:
            lo, hi = _unpack_bf16_pairs(x_ref[...])
            half = lo.shape[1]

            def proj(wb):
                return jnp.dot(lo, wb[0:half, :], preferred_element_type=F32) \
                    + jnp.dot(hi, wb[half:, :], preferred_element_type=F32)

            g = proj(wgb_ref)
            o_ref[...] = (g * jax.nn.sigmoid(g) * proj(wub_ref)).astype(o_ref.dtype)
        else:
            o_ref[...] = jnp.dot(x_ref[...], wb_ref[...],
                                 preferred_element_type=F32).astype(o_ref.dtype)

    @pl.when(r >= nused_ref[0])
    def _():
        o_ref[...] = jnp.zeros_like(o_ref)


def _grouped_matmul(xs, weights, layer, tile_expert, n_used, out_dtype, tn, name):
    p, kx = xs.shape
    swiglu = len(weights) == 2
    _, _, k, n = weights[0].shape
    assert k == (2 * kx if swiglu else kx) and n % tn == 0 and p % MOE_TILE == 0
    chunk = _pick_tile(k, (256, 128, 8))
    w_spec = pl.BlockSpec((None, None, k, tn), lambda j, r, te, nu: (layer, te[r], 0, j))
    osz = jnp.dtype(out_dtype).itemsize
    vmem = 2 * MOE_TILE * kx * xs.dtype.itemsize + len(weights) * (2 * k * tn * 4 + k * tn * 2) \
        + 2 * MOE_TILE * tn * osz + 6 * MOE_TILE * tn * 4 + 2 * MOE_TILE * k * 2
    return pl.pallas_call(
        functools.partial(_group_body, swiglu, chunk),
        out_shape=jax.ShapeDtypeStruct((p, n), out_dtype),
        grid_spec=pltpu.PrefetchScalarGridSpec(
            num_scalar_prefetch=2,
            grid=(n // tn, p // MOE_TILE),
            in_specs=[pl.BlockSpec((MOE_TILE, kx), lambda j, r, te, nu: (r, 0))]
            + [w_spec] * len(weights),
            out_specs=pl.BlockSpec((MOE_TILE, tn), lambda j, r, te, nu: (r, j)),
            scratch_shapes=[pltpu.VMEM((k, tn), BF16)] * len(weights)),
        compiler_params=_params(("arbitrary", "arbitrary"), vmem),
        name=name,
    )(tile_expert, n_used, xs, *weights)


def _combine_body(tc, n_experts, pos1_ref, pos2_ref, route_ref, ye_ref, o_ref, buf_ref, sem):
    i = pl.program_id(0)

    def row_copies(tile, slot, k):
        t = tile * tc + k
        return (pltpu.make_async_copy(ye_ref.at[pl.ds(pos1_ref[t], 1), :],
                                      buf_ref.at[slot, 0, pl.ds(k, 1), :], sem.at[slot]),
                pltpu.make_async_copy(ye_ref.at[pl.ds(pos2_ref[t], 1), :],
                                      buf_ref.at[slot, 1, pl.ds(k, 1), :], sem.at[slot]))

    def start_tile(tile, slot):
        def body(k, c):
            for cp in row_copies(tile, slot, k):
                cp.start()
            return c
        lax.fori_loop(0, tc, body, 0)

    def wait_tile(tile, slot):
        def body(k, c):
            for cp in row_copies(tile, slot, k):
                cp.wait()
            return c
        lax.fori_loop(0, tc, body, 0)

    @pl.when(i == 0)
    def _():
        start_tile(0, 0)

    @pl.when(i + 1 < pl.num_programs(0))
    def _():
        start_tile(i + 1, (i + 1) % 2)

    slot = i % 2
    wait_tile(i, slot)
    g1 = route_ref[:, n_experts + 2:n_experts + 3]
    g2 = route_ref[:, n_experts + 3:n_experts + 4]
    o_ref[...] = g1 * buf_ref[slot, 0] + g2 * buf_ref[slot, 1]


def _moe_combine(ye, route, pos1, pos2, n_experts, tc=128):
    t = route.shape[0]
    d = ye.shape[1]
    tc = min(tc, t)
    assert t % tc == 0
    return pl.pallas_call(
        functools.partial(_combine_body, tc, n_experts),
        out_shape=jax.ShapeDtypeStruct((t, d), F32),
        grid_spec=pltpu.PrefetchScalarGridSpec(
            num_scalar_prefetch=2,
            grid=(t // tc,),
            in_specs=[pl.BlockSpec((tc, LANE), lambda i, p1, p2: (i, 0)),
                      pl.BlockSpec(memory_space=pl.ANY)],
            out_specs=pl.BlockSpec((tc, d), lambda i, p1, p2: (i, 0)),
            scratch_shapes=[pltpu.VMEM((2, 2, tc, d), F32), pltpu.SemaphoreType.DMA((2,))]),
        compiler_params=_params(("arbitrary",), 4 * tc * d * 4 + 4 * tc * d * 4),
        name="moe_combine",
    )(pos1, pos2, route, ye)


def _moe(hp, route, rank, counts, layer_idx, w_gate, w_up, w_down):
    t = hp.shape[0]
    n_e = w_gate.shape[1]
    n_rows = TOP_K * t + n_e * MOE_TILE
    n_tiles = n_rows // MOE_TILE
    i1 = route[:, n_e].astype(jnp.int32)
    i2 = route[:, n_e + 1].astype(jnp.int32)
    cnt = counts[0, :n_e].astype(jnp.int32)
    tiles = (cnt + MOE_TILE - 1) // MOE_TILE
    tile_end = jnp.cumsum(tiles)
    row0 = (tile_end - tiles) * MOE_TILE
    slot = row0[None, :] + rank[:, :n_e].astype(jnp.int32)
    experts = jnp.arange(n_e, dtype=jnp.int32)[None, :]
    pos1 = jnp.sum(jnp.where(experts == i1[:, None], slot, 0), axis=1).astype(jnp.int32)
    pos2 = jnp.sum(jnp.where(experts == i2[:, None], slot, 0), axis=1).astype(jnp.int32)
    n_used = tile_end[-1:].astype(jnp.int32)
    tile_ids = jnp.minimum(jnp.arange(n_tiles, dtype=jnp.int32), n_used[0] - 1)
    tile_expert = jnp.sum(tile_ids[:, None] >= tile_end[None, :], axis=1).astype(jnp.int32)

    xs = _moe_dispatch(hp, pos1, pos2, n_rows)
    f = w_gate.shape[-1]
    a = _grouped_matmul(xs, (w_gate, w_up), layer_idx, tile_expert, n_used, BF16,
                        _pick_tile(f, (256, 128)), "moe_up")
    ye = _grouped_matmul(a, (w_down,), layer_idx, tile_expert, n_used, F32,
                         _pick_tile(w_down.shape[-1], (512, 256, 128)), "moe_down")
    return _moe_combine(ye, route, pos1, pos2, n_e)


def _rel_bucket(dist):
    max_exact = REL_BUCKETS // 2
    d = jnp.maximum(dist, 0)
    df = jnp.maximum(d, max_exact).astype(F32)
    large = max_exact + (jnp.log(df / max_exact) / math.log(REL_MAX_DIST / max_exact)
                         * (REL_BUCKETS - max_exact)).astype(jnp.int32)
    large = jnp.minimum(large, REL_BUCKETS - 1)
    return jnp.where(d < max_exact, d, large)


def _bias_body(n_heads, head0, head_stride, bucket_ref, tab_ref, o_ref):
    bucket = bucket_ref[...]
    base = head0 + head_stride * pl.program_id(0)
    for h in range(n_heads):
        acc = jnp.zeros(bucket.shape, F32)
        for b in range(REL_BUCKETS):
            acc = jnp.where(bucket == b, tab_ref[b, base + h], acc)
        o_ref[h] = acc


def _bias_tiles(rel_bias, buckets, n_heads, head0, head_stride):
    n_tiles, q, k = buckets.shape
    return pl.pallas_call(
        functools.partial(_bias_body, n_heads, head0, head_stride),
        out_shape=jax.ShapeDtypeStruct((n_tiles, n_heads, q, k), F32),
        grid=(n_tiles,),
        in_specs=[pl.BlockSpec((None, q, k), lambda t: (t, 0, 0)),
                  pl.BlockSpec(memory_space=pltpu.SMEM)],
        out_specs=pl.BlockSpec((None, n_heads, q, k), lambda t: (t, 0, 0, 0)),
        compiler_params=_params(("arbitrary",), 4 * (n_heads + 1) * q * k * 4),
        name="bias_tiles",
    )(buckets, rel_bias)


def _bias_tiles_a(rel_bias, n_heads, seq):
    n_blk = seq // A_BLOCK
    qi = jnp.arange(A_BLOCK)[:, None]
    kj = jnp.arange(A_BLOCK)[None, :]
    dist = jnp.arange(n_blk)[:, None, None] * A_BLOCK + (qi - kj)[None]
    return _bias_tiles(rel_bias, _rel_bucket(dist), n_heads, 0, 0)


def _bias_tiles_b(rel_bias, head0, n_heads):
    qi = jnp.arange(B_BLOCK)[:, None]
    kj = jnp.arange(2 * B_BLOCK)[None, :]
    sub_dist = qi + B_BLOCK - kj
    buckets = jnp.stack([_rel_bucket(sub_dist * dil) for _, dil in B_GROUPS])
    return _bias_tiles(rel_bias, buckets, n_heads, head0, n_heads)


def _hybrid_mixer(h, bsz, seq, layer, w_in_t, lat_g, w_uk, w_uv, c_ln_g, c_ln_b, c_w_s, c_b_s,
                  d_conv, w_branch, w_out, bias_a, bias_b):
    t, d = h.shape
    width = d // N_BRANCH
    n_lat = w_uk.shape[0]
    a_heads = width // HEAD_DIM
    n_groups = len(B_GROUPS)
    iq_width = IDX_HEADS * IDX_DIM
    lat_col = width
    iq_col = lat_col + n_lat
    ik_col = iq_col + iq_width
    front = ik_col
    shift = IDX_DIM + IDX_HEADS
    qkv_w = 3 * n_groups * width
    rest_w = qkv_w + 2 * width + 3 * width + N_BRANCH * d
    assert front % 512 == 0 and w_in_t.shape[1] == front + shift + rest_w and shift <= LANE

    pa = _matmul_t(h, w_in_t, layer, row0=0, n_out=front, tm=1024, name="proj_front")
    pik = _matmul_t(h, w_in_t, layer, row0=front, n_out=LANE, name="proj_index_key")
    rest = _matmul_t(h, w_in_t, layer, row0=front + shift, n_out=rest_w,
                     tn=_pick_tile(rest_w, (768, 512, 256, 128)), name="proj_rest")

    pa3 = pa.reshape(bsz, seq, front)
    pik3 = pik.reshape(bsz, seq, LANE)
    rest3 = rest.reshape(bsz, seq, rest_w)

    lat, latt, ikt = _latnorm(pa3, pik3, lat_g, lat_col, n_lat)
    wuk_t = jnp.transpose(w_uk, (1, 2, 0)).astype(BF16)
    wuv_t = jnp.transpose(w_uv, (1, 0, 2)).astype(BF16)
    o_a = _dsa(pa3, pik3, lat, latt, ikt, wuk_t, wuv_t, bias_a,
               q_width=width, iq_col=iq_col, iq_width=iq_width,
               n_variants=DSA_KEY_RANGE_VARIANTS).reshape(t, width)

    outs, lses = [], []
    for g, (window, dil) in enumerate(B_GROUPS):
        o, l = _dilated(rest3, bias_b, g, n_groups, window, dil, width)
        outs.append(o)
        lses.append(l)
    o_b = _mixture(outs, lses)

    o_c = _gmlp(rest, qkv_w, width, c_ln_g, c_ln_b, c_w_s, c_b_s.T)
    o_d = _short_conv(rest3, qkv_w + 2 * width, width, d_conv).reshape(t, width)

    mixed = _merge([o_a, o_b, o_c, o_d], rest, qkv_w + 5 * width, w_branch, layer)
    return _matmul(mixed, w_out, layer, tm=1024, name="proj_out")


def kernel(x, c, w_ada, b_ada, ada_table, rel_bias, norm_pre_mix, norm_post_mix, norm_pre_ffn,
           norm_post_ffn, w_in, a_lat_norm, a_w_uk, a_w_uv, c_ln_g, c_ln_b, c_w_s, c_b_s, d_conv,
           w_branch, w_out, ffn_w_gate, ffn_w_up, ffn_w_down, moe_router, moe_w_gate, moe_w_up,
           moe_w_down):
    bsz, seq, d = x.shape
    depth = w_in.shape[0]
    t = bsz * seq

    c_pad = jnp.pad(c, ((0, 16 - bsz % 16 if bsz % 16 else 0), (0, 0)))
    mod_shared = _ada(c_pad, w_ada, b_ada)[:bsz].reshape(bsz, N_MOD, d)
    mods = [mod_shared + ada_table[layer] for layer in range(depth)]

    w_in_t = jnp.transpose(w_in, (0, 2, 1))
    width = d // N_BRANCH
    bias_a = _bias_tiles_a(rel_bias, width // HEAD_DIM, seq)
    bias_b = _bias_tiles_b(rel_bias, width // HEAD_DIM, width // HEAD_DIM)

    h = _prenorm(x, mods[0], norm_pre_mix[0])
    for layer in range(depth):
        mod = mods[layer]
        y = _hybrid_mixer(h.reshape(t, d), bsz, seq, layer, w_in_t, a_lat_norm[layer],
                          a_w_uk[layer], a_w_uv[layer], c_ln_g[layer], c_ln_b[layer],
                          c_w_s[layer], c_b_s[layer], d_conv[layer], w_branch, w_out,
                          bias_a, bias_b)
        j = layer // 2
        dense = layer % 2 == 0
        res = _postnorm(x, y.reshape(bsz, seq, d), mod, norm_post_mix[layer], 2, nxt=(3, 4),
                        mod_next=mod, g_pre=norm_pre_ffn[layer],
                        router_t=None if dense else moe_router[j].T)
        if dense:
            x, h = res
            a = _swiglu_up(h.reshape(t, d), ffn_w_gate[j][None], ffn_w_up[j][None])
            f = ffn_w_down.shape[1]
            tk = f // 2 if (f // 2) % LANE == 0 and f > 4096 else f
            y = _swiglu_down(a, ffn_w_down[j], tk)
        else:
            x, hp, route, rank, counts = res
            y = _moe(hp.reshape(t, d // 2), route.reshape(t, LANE), rank.reshape(t, LANE), counts,
                     j, moe_w_gate, moe_w_up, moe_w_down)
        y3 = y.reshape(bsz, seq, d)
        if layer + 1 < depth:
            x, h = _postnorm(x, y3, mod, norm_post_ffn[layer], 5, nxt=(0, 1),
                             mod_next=mods[layer + 1], g_pre=norm_pre_mix[layer + 1])
        else:
            (x,) = _postnorm(x, y3, mod, norm_post_ffn[layer], 5)
    return x
```

```python
import functools
import math

import jax
import jax.numpy as jnp
from jax import lax
from jax.experimental import pallas as pl
from jax.experimental.pallas import tpu as pltpu

F32 = jnp.float32
BF16 = jnp.bfloat16

LANE = 128
V7X_VMEM_BYTES = 64 * 1024 * 1024
VMEM_CAP = V7X_VMEM_BYTES - 8 * 1024 * 1024

HEAD_DIM = 128
N_BRANCH = 4
IDX_HEADS = 8
IDX_DIM = 64
IDX_TOPK = 256
A_BLOCK = 128
DSA_KEY_RANGE_VARIANTS = 4
B_GROUPS = ((128, 1), (512, 4), (2048, 16))
B_BLOCK = 128
C_CHUNK = 128
C_GROUPS = 8
D_CONV_WIDTH = 3
REL_BUCKETS = 32
REL_MAX_DIST = 2048
TOP_K = 2
N_MOD = 6
EPS = 1e-6
NEG_INF = float("-inf")


def _params(semantics, vmem_bytes):
    limit = int(min(VMEM_CAP, max(vmem_bytes * 5 // 4 + (4 << 20), 16 << 20)))
    return pltpu.CompilerParams(dimension_semantics=semantics, vmem_limit_bytes=limit)


def _pick_tile(n, candidates):
    for c in candidates:
        if n % c == 0:
            return c
    raise ValueError(f"no tile in {candidates} divides {n}")


def _cast_rows(src_fn, dst_ref, n_rows, chunk):
    def body(r, carry):
        rows = pl.ds(pl.multiple_of(r * chunk, chunk), chunk)
        dst_ref[rows, :] = src_fn(rows).astype(BF16)
        return carry
    lax.fori_loop(0, n_rows // chunk, body, 0)


def _mm_body(chunk, lhs_ref, w_ref, o_ref, wb_ref):
    @pl.when(pl.program_id(1) == 0)
    def _():
        _cast_rows(lambda rows: w_ref[rows, :], wb_ref, w_ref.shape[0], chunk)

    o_ref[...] = jnp.dot(lhs_ref[...], wb_ref[...],
                         preferred_element_type=F32).astype(o_ref.dtype)


def _matmul(lhs, w, layer, *, tn=512, tm=512, out_dtype=F32, name="mm"):
    m, k = lhs.shape
    n_out = w.shape[2]
    tn = min(tn, n_out)
    tm = min(tm, m)
    assert n_out % tn == 0 and m % tm == 0 and tn % LANE == 0
    chunk = _pick_tile(k, (256, 128, 64, 8))
    osz = jnp.dtype(out_dtype).itemsize
    vmem = 2 * tm * k * 2 + 2 * k * tn * 4 + k * tn * 2 + 2 * tm * tn * osz + tm * tn * 4 \
        + chunk * tn * 8
    return pl.pallas_call(
        functools.partial(_mm_body, chunk),
        out_shape=jax.ShapeDtypeStruct((m, n_out), out_dtype),
        grid=(n_out // tn, m // tm),
        in_specs=[pl.BlockSpec((tm, k), lambda j, i: (i, 0)),
                  pl.BlockSpec((None, k, tn), lambda j, i: (layer, 0, j))],
        out_specs=pl.BlockSpec((tm, tn), lambda j, i: (i, j)),
        scratch_shapes=[pltpu.VMEM((k, tn), BF16)],
        compiler_params=_params(("arbitrary", "arbitrary"), vmem),
        name=name,
    )(lhs, w)


def _mm_t_body(chunk, lhs_ref, w_ref, o_ref, wb_ref):
    @pl.when(pl.program_id(1) == 0)
    def _():
        _cast_rows(lambda rows: w_ref[rows, :], wb_ref, w_ref.shape[0], chunk)

    acc = lax.dot_general(lhs_ref[...], wb_ref[...], (((1,), (1,)), ((), ())),
                          preferred_element_type=F32)
    o_ref[...] = acc.astype(o_ref.dtype)


def _matmul_t(lhs, w_t, layer, *, row0, n_out, tn=512, tm=512, out_dtype=F32, name="mm_t"):
    m, k = lhs.shape
    tn = min(tn, n_out)
    tm = min(tm, m)
    assert n_out % tn == 0 and m % tm == 0 and row0 % 8 == 0 and tn % LANE == 0
    chunk = _pick_tile(tn, (256, 128))
    osz = jnp.dtype(out_dtype).itemsize
    vmem = 2 * tm * k * 2 + 2 * k * tn * 4 + k * tn * 2 + 2 * tm * tn * osz + tm * tn * 4 \
        + 4 * chunk * k * 4
    return pl.pallas_call(
        functools.partial(_mm_t_body, chunk),
        out_shape=jax.ShapeDtypeStruct((m, n_out), out_dtype),
        grid=(n_out // tn, m // tm),
        in_specs=[pl.BlockSpec((tm, k), lambda j, i: (i, 0)),
                  pl.BlockSpec((None, pl.Element(tn), pl.Element(k)),
                               lambda j, i: (layer, pl.multiple_of(row0 + j * tn, 8), 0))],
        out_specs=pl.BlockSpec((tm, tn), lambda j, i: (i, j)),
        scratch_shapes=[pltpu.VMEM((tn, k), BF16)],
        compiler_params=_params(("arbitrary", "arbitrary"), vmem),
        name=name,
    )(lhs, w_t)


def _ada_body(c_ref, w_ref, b_ref, o_ref):
    c = c_ref[...]
    a = (c * jax.nn.sigmoid(c)).astype(BF16)
    o_ref[...] = jnp.dot(a, w_ref[...].astype(BF16), preferred_element_type=F32) + b_ref[...]


def _ada(c_pad, w_ada, b_ada, tn=512):
    m, k = c_pad.shape
    n = w_ada.shape[1]
    tn = _pick_tile(n, (tn, 256, 128))
    vmem = 2 * k * tn * 4 + k * tn * 2 + 4 * m * k * 4
    return pl.pallas_call(
        _ada_body,
        out_shape=jax.ShapeDtypeStruct((m, n), F32),
        grid=(n // tn,),
        in_specs=[pl.BlockSpec((m, k), lambda j: (0, 0)),
                  pl.BlockSpec((k, tn), lambda j: (0, j)),
                  pl.BlockSpec((1, tn), lambda j: (0, j))],
        out_specs=pl.BlockSpec((m, tn), lambda j: (0, j)),
        compiler_params=_params(("arbitrary",), vmem),
        name="ada",
    )(c_pad, w_ada, b_ada.reshape(1, n))


def _rms(x, g):
    return x * lax.rsqrt(jnp.mean(x * x, axis=-1, keepdims=True) + EPS) * g


def _route(h, router_ref, n_experts):
    rows = h.shape[0]
    lane = lax.broadcasted_iota(jnp.int32, (rows, LANE), 1)
    logits = jnp.full((rows, LANE), NEG_INF, F32)
    for e in range(n_experts):
        le = jnp.sum(h * router_ref[e:e + 1, :], axis=-1, keepdims=True)
        logits = jnp.where(lane == e, le, logits)
    m1 = jnp.max(logits, axis=-1, keepdims=True)
    i1 = jnp.min(jnp.where(logits == m1, lane, LANE), axis=-1, keepdims=True)
    rest = jnp.where(lane == i1, NEG_INF, logits)
    m2 = jnp.max(rest, axis=-1, keepdims=True)
    i2 = jnp.min(jnp.where(rest == m2, lane, LANE), axis=-1, keepdims=True)
    e2 = jnp.exp(m2 - m1)
    den = 1.0 + e2
    member = jnp.where(lane == i1, 1.0, jnp.where(lane == i2, 1.0, 0.0))
    extra = jnp.where(lane == n_experts, i1.astype(F32),
                      jnp.where(lane == n_experts + 1, i2.astype(F32),
                                jnp.where(lane == n_experts + 2, 1.0 / den, e2 / den)))
    return jnp.where(lane < n_experts, member, jnp.where(lane < n_experts + 4, extra, 0.0))


def _pack_bf16_pairs(h):
    half = h.shape[1] // 2
    lo = pltpu.bitcast(h[:, :half].astype(BF16).astype(F32), jnp.uint32)
    hi = pltpu.bitcast(h[:, half:].astype(BF16).astype(F32), jnp.uint32)
    return (hi & jnp.uint32(0xFFFF0000)) | (lo >> 16)


def _unpack_bf16_pairs(u):
    lo = pltpu.bitcast(u << 16, F32).astype(BF16)
    hi = pltpu.bitcast(u & jnp.uint32(0xFFFF0000), F32).astype(BF16)
    return lo, hi


def _pre_body(x_ref, mod_ref, g_ref, h_ref):
    h = _rms(x_ref[...], g_ref[...]) * (1.0 + mod_ref[1:2, :]) + mod_ref[0:1, :]
    h_ref[...] = h.astype(h_ref.dtype)


def _prenorm(x3, mod, g, tr=256):
    b, s, d = x3.shape
    tr = min(tr, s)
    vmem = 2 * tr * d * 4 + 2 * tr * d * 2 + 4 * tr * d * 4
    return pl.pallas_call(
        _pre_body,
        out_shape=jax.ShapeDtypeStruct((b, s, d), BF16),
        grid=(b, s // tr),
        in_specs=[pl.BlockSpec((None, tr, d), lambda bi, i: (bi, i, 0)),
                  pl.BlockSpec((None, N_MOD, d), lambda bi, i: (bi, 0, 0)),
                  pl.BlockSpec((1, d), lambda bi, i: (0, 0))],
        out_specs=pl.BlockSpec((None, tr, d), lambda bi, i: (bi, i, 0)),
        compiler_params=_params(("arbitrary", "arbitrary"), vmem),
        name="prenorm",
    )(x3, mod, g.reshape(1, d))


def _post_body(gate_row, nxt, n_experts, *refs):
    x_ref, y_ref, mod_ref, gpost_ref = refs[:4]
    pos = 4
    if nxt is not None:
        modn_ref, gpre_ref = refs[pos:pos + 2]
        pos += 2
    if n_experts:
        router_ref = refs[pos]
        pos += 1
    xo_ref = refs[pos]
    pos += 1
    xn = x_ref[...] + mod_ref[gate_row:gate_row + 1, :] * _rms(y_ref[...], gpost_ref[...])
    xo_ref[...] = xn
    if nxt is not None:
        shift_row, scale_row = nxt
        h = _rms(xn, gpre_ref[...]) * (1.0 + modn_ref[scale_row:scale_row + 1, :]) \
            + modn_ref[shift_row:shift_row + 1, :]
        if not n_experts:
            refs[pos][...] = h.astype(BF16)
            return
        hp_ref, route_ref, rank_ref, counts_ref, carry_ref = refs[pos:pos + 5]
        hp_ref[...] = _pack_bf16_pairs(h)
        route = _route(h, router_ref, n_experts)
        route_ref[...] = route

        @pl.when((pl.program_id(0) == 0) & (pl.program_id(1) == 0))
        def _():
            carry_ref[...] = jnp.zeros_like(carry_ref)

        rows = route.shape[0]
        lane = lax.broadcasted_iota(jnp.int32, route.shape, 1)
        member = jnp.where(lane < n_experts, route, 0.0)
        ri = lax.broadcasted_iota(jnp.int32, (rows, rows), 0)
        ci = lax.broadcasted_iota(jnp.int32, (rows, rows), 1)
        earlier = jnp.where(ci < ri, 1.0, 0.0).astype(BF16)
        within = jnp.dot(earlier, member.astype(BF16), preferred_element_type=F32)
        rank_ref[...] = within + carry_ref[0:1, :]
        carry_ref[...] = carry_ref[...] + jnp.sum(member, axis=0, keepdims=True)
        counts_ref[...] = carry_ref[...]


def _postnorm(x3, y3, mod, g_post, gate_row, *, nxt=None, mod_next=None, g_pre=None,
              router_t=None, tr=256):
    b, s, d = x3.shape
    tr = min(tr, s)
    n_experts = 0 if router_t is None else router_t.shape[0]
    row = lambda bi, i: (bi, i, 0)
    mod_spec = pl.BlockSpec((None, N_MOD, d), lambda bi, i: (bi, 0, 0))
    in_specs = [pl.BlockSpec((None, tr, d), row), pl.BlockSpec((None, tr, d), row), mod_spec,
                pl.BlockSpec((1, d), lambda bi, i: (0, 0))]
    args = [x3, y3, mod, g_post.reshape(1, d)]
    out_shape = [jax.ShapeDtypeStruct((b, s, d), F32)]
    out_specs = [pl.BlockSpec((None, tr, d), row)]
    scratch = []
    if nxt is not None:
        in_specs += [mod_spec, pl.BlockSpec((1, d), lambda bi, i: (0, 0))]
        args += [mod_next, g_pre.reshape(1, d)]
    if nxt is not None and not n_experts:
        out_shape.append(jax.ShapeDtypeStruct((b, s, d), BF16))
        out_specs.append(pl.BlockSpec((None, tr, d), row))
    if n_experts:
        assert nxt is not None and n_experts + 4 <= LANE
        in_specs.append(pl.BlockSpec((n_experts, d), lambda bi, i: (0, 0)))
        args.append(router_t)
        out_shape += [jax.ShapeDtypeStruct((b, s, d // 2), jnp.uint32),
                      jax.ShapeDtypeStruct((b, s, LANE), F32),
                      jax.ShapeDtypeStruct((b, s, LANE), F32),
                      jax.ShapeDtypeStruct((8, LANE), F32)]
        out_specs += [pl.BlockSpec((None, tr, d // 2), row), pl.BlockSpec((None, tr, LANE), row),
                      pl.BlockSpec((None, tr, LANE), row),
                      pl.BlockSpec((8, LANE), lambda bi, i: (0, 0))]
        scratch.append(pltpu.VMEM((8, LANE), F32))
    vmem = 2 * 3 * tr * d * 4 + 2 * tr * d * 2 + 6 * tr * d * 4
    return pl.pallas_call(
        functools.partial(_post_body, gate_row, nxt, n_experts),
        out_shape=out_shape,
        grid=(b, s // tr),
        in_specs=in_specs,
        out_specs=out_specs,
        scratch_shapes=scratch,
        compiler_params=_params(("arbitrary", "arbitrary"), vmem),
        name="postnorm",
    )(*args)


def _lat_body(alat_ref, pik_ref, g_ref, lat_ref, latt_ref, ikt_ref):
    lat = _rms(alat_ref[...], g_ref[...])
    lat_ref[...] = lat.astype(BF16)
    latt_ref[...] = lat.T.astype(BF16)
    ikt_ref[...] = pik_ref[...].T.astype(BF16)


def _latnorm(pa3, pik3, g_lat, lat_col, n_lat, tr=512):
    b, s, _ = pa3.shape
    tr = min(tr, s)
    assert lat_col % n_lat == 0
    vmem = 2 * tr * (n_lat + LANE) * 4 + 4 * tr * (n_lat + LANE) * 2 + 4 * tr * n_lat * 4
    return pl.pallas_call(
        _lat_body,
        out_shape=[jax.ShapeDtypeStruct((b, s, n_lat), BF16),
                   jax.ShapeDtypeStruct((b, n_lat, s), BF16),
                   jax.ShapeDtypeStruct((b, LANE, s), BF16)],
        grid=(b, s // tr),
        in_specs=[pl.BlockSpec((None, tr, n_lat), lambda bi, i: (bi, i, lat_col // n_lat)),
                  pl.BlockSpec((None, tr, LANE), lambda bi, i: (bi, i, 0)),
                  pl.BlockSpec((1, n_lat), lambda bi, i: (0, 0))],
        out_specs=[pl.BlockSpec((None, tr, n_lat), lambda bi, i: (bi, i, 0)),
                   pl.BlockSpec((None, n_lat, tr), lambda bi, i: (bi, 0, i)),
                   pl.BlockSpec((None, LANE, tr), lambda bi, i: (bi, 0, i))],
        compiler_params=_params(("arbitrary", "arbitrary"), vmem),
        name="latnorm",
    )(pa3, pik3, g_lat.reshape(1, n_lat))


def _dsa_body(topk, n_variants, *refs):
    i = pl.program_id(1)
    n_blk = refs[4].shape[0] // A_BLOCK
    per = -(-n_blk // n_variants)
    for v in range(n_variants):
        blocks = min(n_blk, (v + 1) * per)

        @pl.when((i >= v * per) & (i < (v + 1) * per))
        def _(blocks=blocks):
            _dsa_compute(topk, blocks * A_BLOCK, i, *refs)


def _dsa_compute(topk, seq, i, q_ref, iq_ref, iw_ref, ikt_ref, lat_ref, latt_ref, wuk_ref, wuv_ref,
                 bias_ref, o_ref, madd_ref):
    n_blk = seq // A_BLOCK
    n_heads = wuk_ref.shape[0]
    int_min = jnp.int32(-2 ** 31)
    qpos = i * A_BLOCK + lax.broadcasted_iota(jnp.int32, (A_BLOCK, 1), 0)
    kpos = lax.broadcasted_iota(jnp.int32, (A_BLOCK, seq), 1)

    iq = iq_ref[...].astype(BF16)
    iw = iw_ref[...]
    ikt = ikt_ref[0:IDX_DIM, 0:seq]
    score = jnp.zeros((A_BLOCK, seq), F32)
    for h in range(IDX_HEADS):
        l = jnp.dot(iq[:, h * IDX_DIM:(h + 1) * IDX_DIM], ikt, preferred_element_type=F32)
        score = score + iw[:, IDX_DIM + h:IDX_DIM + h + 1] * jnp.maximum(l, 0.0)

    score = jnp.where(score == 0.0, 0.0, score)
    key = pltpu.bitcast(score, jnp.int32)
    key = jnp.where(key < 0, key ^ jnp.int32(0x7FFFFFFF), key)
    key = jnp.where(kpos <= qpos, key, int_min)
    k_eff = jnp.minimum(qpos + 1, topk).astype(F32)

    def count(mask):
        return jnp.sum(jnp.where(mask, 1.0, 0.0), axis=1, keepdims=True)

    t0 = jnp.where(count(key >= 0) >= k_eff, jnp.int32(0), int_min)

    def value_step(it, t):
        c = t | (jnp.int32(1) << (30 - it))
        return jnp.where(count(key >= c) >= k_eff, c, t)

    t = lax.fori_loop(0, 31, value_step, t0)

    need = k_eff - count(key > t)
    eq = key == t
    idx_bits = max(1, (seq - 1).bit_length())

    def index_step(it, p):
        c = p | (jnp.int32(1) << (idx_bits - 1 - it))
        below = jnp.sum(jnp.where(eq, jnp.where(kpos < c, 1.0, 0.0), 0.0), axis=1, keepdims=True)
        return jnp.where(below < need, c, p)

    p = lax.fori_loop(0, idx_bits, index_step, jnp.zeros((A_BLOCK, 1), jnp.int32))
    tie_add = jnp.where(eq, jnp.where(kpos <= p, 0.0, NEG_INF), NEG_INF)
    madd_ref[:, 0:seq] = jnp.where(key > t, 0.0, tie_add)

    scale = HEAD_DIM ** -0.5
    for h in range(n_heads):
        cols = slice(h * HEAD_DIM, (h + 1) * HEAD_DIM)
        qa = jnp.dot(q_ref[:, cols].astype(BF16), wuk_ref[h],
                     preferred_element_type=F32).astype(BF16)
        lg = jnp.dot(qa, latt_ref[:, 0:seq], preferred_element_type=F32) * scale
        bias = jnp.concatenate([bias_ref[jnp.maximum(i - j, 0), h] for j in range(n_blk)], axis=1)
        lg = lg + bias + madd_ref[:, 0:seq]
        m = jnp.max(lg, axis=1, keepdims=True)
        e = jnp.exp(lg - m)
        den = jnp.sum(e, axis=1, keepdims=True)
        o_lat = jnp.dot(e.astype(BF16), lat_ref[0:seq, :], preferred_element_type=F32) / den
        o_ref[:, cols] = jnp.dot(o_lat.astype(BF16), wuv_ref[h],
                                 preferred_element_type=F32).astype(o_ref.dtype)


def _dsa(pa3, pik3, lat, latt, ikt, wuk_t, wuv_t, bias_a, *, q_width, iq_col, iq_width,
         n_variants):
    b, s, _ = pa3.shape
    n_lat = lat.shape[-1]
    n_heads = wuk_t.shape[0]
    n_blk = s // A_BLOCK
    topk = min(IDX_TOPK, s // 4)
    assert iq_col % iq_width == 0
    vmem = 2 * (A_BLOCK * (q_width + iq_width + LANE) * 4 + LANE * s * 2 + 2 * s * n_lat * 2
                + 2 * n_heads * HEAD_DIM * n_lat * 2 + n_heads * n_blk * A_BLOCK * A_BLOCK * 4
                + A_BLOCK * q_width * 2) + 10 * A_BLOCK * s * 4
    return pl.pallas_call(
        functools.partial(_dsa_body, topk, n_variants),
        out_shape=jax.ShapeDtypeStruct((b, s, q_width), BF16),
        grid=(b, n_blk),
        in_specs=[pl.BlockSpec((None, A_BLOCK, q_width), lambda bi, i: (bi, i, 0)),
                  pl.BlockSpec((None, A_BLOCK, iq_width), lambda bi, i: (bi, i, iq_col // iq_width)),
                  pl.BlockSpec((None, A_BLOCK, LANE), lambda bi, i: (bi, i, 0)),
                  pl.BlockSpec((None, LANE, s), lambda bi, i: (bi, 0, 0)),
                  pl.BlockSpec((None, s, n_lat), lambda bi, i: (bi, 0, 0)),
                  pl.BlockSpec((None, n_lat, s), lambda bi, i: (bi, 0, 0)),
                  pl.BlockSpec(wuk_t.shape, lambda bi, i: (0, 0, 0)),
                  pl.BlockSpec(wuv_t.shape, lambda bi, i: (0, 0, 0)),
                  pl.BlockSpec(bias_a.shape, lambda bi, i: (0, 0, 0, 0))],
        out_specs=pl.BlockSpec((None, A_BLOCK, q_width), lambda bi, i: (bi, i, 0)),
        scratch_shapes=[pltpu.VMEM((A_BLOCK, s), F32)],
        compiler_params=_params(("arbitrary", "arbitrary"), vmem),
        name="dsa",
    )(pa3, pa3, pik3, ikt, lat, latt, wuk_t, wuv_t, bias_a)


def _dil_body(span, dil, hb, has_prev, *refs):
    if has_prev:
        q_ref, kp_ref, kc_ref, vp_ref, vc_ref, bias_ref, o_ref, lse_ref = refs
    else:
        q_ref, kc_ref, vc_ref, bias_ref, o_ref, lse_ref = refs
    i = pl.program_id(1)
    hblk = pl.program_id(2)
    n_keys = 2 * B_BLOCK if has_prev else B_BLOCK
    key0 = 0 if has_prev else B_BLOCK
    qi = lax.broadcasted_iota(jnp.int32, (B_BLOCK, n_keys), 0)
    kj = lax.broadcasted_iota(jnp.int32, (B_BLOCK, n_keys), 1) + key0
    sub_dist = qi + B_BLOCK - kj
    in_band = jnp.where(sub_dist <= span,
                        jnp.where((i - 1) * B_BLOCK + kj >= 0, 0.0, NEG_INF), NEG_INF)
    madd = jnp.where(sub_dist >= 0, in_band, NEG_INF)
    lane = lax.broadcasted_iota(jnp.int32, (B_BLOCK, LANE), 1)
    scale = HEAD_DIM ** -0.5

    @pl.when(hblk == 0)
    def _():
        lse_ref[...] = jnp.zeros_like(lse_ref)

    for r in range(dil):
        rows = pl.ds(r, B_BLOCK, stride=dil) if dil > 1 else slice(None)
        lse_acc = lse_ref[rows, :]
        for h in range(hb):
            cols = slice(h * HEAD_DIM, (h + 1) * HEAD_DIM)
            q = q_ref[rows, cols].astype(BF16)
            if has_prev:
                k2 = jnp.concatenate([kp_ref[rows, cols], kc_ref[rows, cols]], axis=0).astype(BF16)
                v2 = jnp.concatenate([vp_ref[rows, cols], vc_ref[rows, cols]], axis=0).astype(BF16)
            else:
                k2 = kc_ref[rows, cols].astype(BF16)
                v2 = vc_ref[rows, cols].astype(BF16)
            lg = lax.dot_general(q, k2, (((1,), (1,)), ((), ())), preferred_element_type=F32)
            lg = lg * scale + bias_ref[h, :, key0:] + madd
            m = jnp.max(lg, axis=1, keepdims=True)
            e = jnp.exp(lg - m)
            den = jnp.sum(e, axis=1, keepdims=True)
            o_ref[rows, cols] = jnp.dot((e / den).astype(BF16), v2, preferred_element_type=F32)
            lse_acc = jnp.where(lane == hblk * hb + h, m + jnp.log(den), lse_acc)
        lse_ref[rows, :] = lse_acc


def _dilated(rest3, bias_b, g, n_groups, window, dil, width):
    b, s, w_all = rest3.shape
    n_heads = width // HEAD_DIM
    span = window // dil
    n_sub = s // dil
    assert span == B_BLOCK and n_sub % B_BLOCK == 0 and w_all % width == 0
    nb = n_sub // B_BLOCK
    rows = B_BLOCK * dil
    hb = n_heads if dil == 1 else 1
    assert HEAD_DIM == LANE
    n_hblk = n_heads // hb
    has_prev = nb > 1
    cur = lambda slot: (lambda bi, i, hk: (bi, i, (slot * n_groups + g) * n_hblk + hk))
    prev = lambda slot: (lambda bi, i, hk: (bi, jnp.maximum(i - 1, 0),
                                            (slot * n_groups + g) * n_hblk + hk))
    blk = (None, rows, hb * HEAD_DIM)
    in_specs = [pl.BlockSpec(blk, cur(0))]
    for slot in (1, 2):
        if has_prev:
            in_specs.append(pl.BlockSpec(blk, prev(slot)))
        in_specs.append(pl.BlockSpec(blk, cur(slot)))
    in_specs.append(pl.BlockSpec((None, hb, B_BLOCK, 2 * B_BLOCK), lambda bi, i, hk: (g, hk, 0, 0)))
    vmem = 2 * (len(in_specs) * rows * hb * HEAD_DIM + rows * LANE) * 4 \
        + 2 * hb * B_BLOCK * 2 * B_BLOCK * 4 + (4 << 20)
    o, lse = pl.pallas_call(
        functools.partial(_dil_body, span, dil, hb, has_prev),
        out_shape=[jax.ShapeDtypeStruct((b, s, width), F32),
                   jax.ShapeDtypeStruct((b, s, LANE), F32)],
        grid=(b, nb, n_hblk),
        in_specs=in_specs,
        out_specs=[pl.BlockSpec(blk, lambda bi, i, hk: (bi, i, hk)),
                   pl.BlockSpec((None, rows, LANE), lambda bi, i, hk: (bi, i, 0))],
        compiler_params=_params(("arbitrary", "arbitrary", "arbitrary"), vmem),
        name=f"dilated{g}",
    )(*([rest3] * (len(in_specs) - 1)), bias_b)
    return o.reshape(b * s, width), lse.reshape(b * s, LANE)


def _mix_body(n_groups, n_heads, *refs):
    o_refs = refs[:n_groups]
    l_refs = refs[n_groups:2 * n_groups]
    out_ref = refs[2 * n_groups]
    ls = [r[...] for r in l_refs]
    m = functools.reduce(jnp.maximum, ls)
    es = [jnp.exp(l - m) for l in ls]
    tot = functools.reduce(jnp.add, es)
    ws = [e / tot for e in es]
    for h in range(n_heads):
        cols = slice(h * HEAD_DIM, (h + 1) * HEAD_DIM)
        acc = ws[0][:, h:h + 1] * o_refs[0][:, cols]
        for gi in range(1, n_groups):
            acc = acc + ws[gi][:, h:h + 1] * o_refs[gi][:, cols]
        out_ref[:, cols] = acc.astype(out_ref.dtype)


def _mixture(outs, lses, tr=512):
    t, width = outs[0].shape
    tr = min(tr, t)
    n_groups = len(outs)
    vmem = 2 * n_groups * tr * (width + LANE) * 4 + 2 * tr * width * 2 + 4 * tr * width * 4
    return pl.pallas_call(
        functools.partial(_mix_body, n_groups, width // HEAD_DIM),
        out_shape=jax.ShapeDtypeStruct((t, width), BF16),
        grid=(t // tr,),
        in_specs=[pl.BlockSpec((tr, width), lambda i: (i, 0))] * n_groups
        + [pl.BlockSpec((tr, LANE), lambda i: (i, 0))] * n_groups,
        out_specs=pl.BlockSpec((tr, width), lambda i: (i, 0)),
        compiler_params=_params(("arbitrary",), vmem),
        name="mixture",
    )(*outs, *lses)


def _gmlp_body(u_ref, v_ref, g_ref, b_ref, ws_ref, bs_ref, o_ref):
    u = jax.nn.gelu(u_ref[...])
    v = jax.nn.gelu(v_ref[...])
    mu = jnp.mean(v, axis=-1, keepdims=True)
    var = jnp.mean(jnp.square(v - mu), axis=-1, keepdims=True)
    v = (v - mu) * lax.rsqrt(var + EPS) * g_ref[...] + b_ref[...]
    ti = lax.broadcasted_iota(jnp.int32, (C_CHUNK, C_CHUNK), 0)
    si = lax.broadcasted_iota(jnp.int32, (C_CHUNK, C_CHUNK), 1)
    causal = si <= ti
    gw = u.shape[1] // C_GROUPS
    for g in range(C_GROUPS):
        cols = slice(g * gw, (g + 1) * gw)
        w = jnp.where(causal, ws_ref[g], 0.0).astype(BF16)
        mixed = jnp.dot(w, v[:, cols].astype(BF16), preferred_element_type=F32) + bs_ref[:, g:g + 1]
        o_ref[:, cols] = (u[:, cols] * mixed).astype(o_ref.dtype)


def _gmlp(rest, u_col, width, ln_g, ln_b, w_s, b_s_t):
    t = rest.shape[0]
    assert u_col % width == 0 and (width // C_GROUPS) % LANE == 0
    vmem = 2 * 2 * C_CHUNK * width * 4 + 2 * C_CHUNK * width * 2 + 2 * C_GROUPS * C_CHUNK * C_CHUNK * 4 \
        + 8 * C_CHUNK * width * 4
    return pl.pallas_call(
        _gmlp_body,
        out_shape=jax.ShapeDtypeStruct((t, width), BF16),
        grid=(t // C_CHUNK,),
        in_specs=[pl.BlockSpec((C_CHUNK, width), lambda i: (i, u_col // width)),
                  pl.BlockSpec((C_CHUNK, width), lambda i: (i, u_col // width + 1)),
                  pl.BlockSpec((1, width), lambda i: (0, 0)),
                  pl.BlockSpec((1, width), lambda i: (0, 0)),
                  pl.BlockSpec(w_s.shape, lambda i: (0, 0, 0)),
                  pl.BlockSpec(b_s_t.shape, lambda i: (0, 0))],
        out_specs=pl.BlockSpec((C_CHUNK, width), lambda i: (i, 0)),
        compiler_params=_params(("arbitrary",), vmem),
        name="gmlp",
    )(rest, rest, ln_g.reshape(1, width), ln_b.reshape(1, width), w_s, b_s_t)


def _conv_body(h_ref, bg_ref, cg_ref, w_ref, o_ref, carry_ref):
    @pl.when(pl.program_id(1) == 0)
    def _():
        carry_ref[...] = jnp.zeros_like(carry_ref)

    ch = cg_ref[...] * h_ref[...]
    rows = ch.shape[0]
    row = lax.broadcasted_iota(jnp.int32, ch.shape, 0)
    pm1 = carry_ref[7:8, :]
    pm2 = carry_ref[6:7, :]
    s1 = jnp.where(row == 0, pm1, pltpu.roll(ch, 1, 0))
    s2 = jnp.where(row == 0, pm2, jnp.where(row == 1, pm1, pltpu.roll(ch, 2, 0)))
    z = w_ref[0:1, :] * s2 + w_ref[1:2, :] * s1 + w_ref[2:3, :] * ch
    o_ref[...] = (bg_ref[...] * z).astype(o_ref.dtype)
    carry_ref[...] = ch[rows - 8:, :]


def _short_conv(rest3, h_col, width, conv_w, tr=256):
    b, s, _ = rest3.shape
    tr = min(tr, s)
    assert h_col % width == 0 and conv_w.shape[0] == D_CONV_WIDTH
    cb = h_col // width
    vmem = 2 * 3 * tr * width * 4 + 2 * tr * width * 2 + 6 * tr * width * 4
    return pl.pallas_call(
        _conv_body,
        out_shape=jax.ShapeDtypeStruct((b, s, width), BF16),
        grid=(b, s // tr),
        in_specs=[pl.BlockSpec((None, tr, width), lambda bi, i: (bi, i, cb)),
                  pl.BlockSpec((None, tr, width), lambda bi, i: (bi, i, cb + 1)),
                  pl.BlockSpec((None, tr, width), lambda bi, i: (bi, i, cb + 2)),
                  pl.BlockSpec(conv_w.shape, lambda bi, i: (0, 0))],
        out_specs=pl.BlockSpec((None, tr, width), lambda bi, i: (bi, i, 0)),
        scratch_shapes=[pltpu.VMEM((8, width), F32)],
        compiler_params=_params(("arbitrary", "arbitrary"), vmem),
        name="short_conv",
    )(rest3, rest3, rest3, conv_w)


def _merge_body(chunk, *refs):
    br_refs = refs[:N_BRANCH]
    gate_refs = refs[N_BRANCH:2 * N_BRANCH]
    w_ref, o_ref, wb_ref = refs[2 * N_BRANCH:]

    @pl.when(pl.program_id(1) == 0)
    def _():
        for n in range(N_BRANCH):
            _cast_rows(lambda rows, n=n: w_ref[n, rows, :], wb_ref.at[n], w_ref.shape[1], chunk)

    acc = None
    for n in range(N_BRANCH):
        proj = jnp.dot(br_refs[n][...], wb_ref[n], preferred_element_type=F32)
        term = jax.nn.sigmoid(gate_refs[n][...]) * proj
        acc = term if acc is None else acc + term
    o_ref[...] = acc.astype(o_ref.dtype)


def _merge(branches, rest, gate_col, w_branch, layer, tn=512, tm=512):
    t, width = branches[0].shape
    d = w_branch.shape[-1]
    tn = min(tn, d)
    tm = min(tm, t)
    assert d % tn == 0 and t % tm == 0 and gate_col % tn == 0
    chunk = _pick_tile(width, (256, 128, 8))
    gate_spec = lambda n: pl.BlockSpec((tm, tn), lambda j, i: (i, (gate_col + n * d) // tn + j))
    vmem = 2 * N_BRANCH * (tm * width * 2 + tm * tn * 4 + width * tn * 4) + N_BRANCH * width * tn * 2 \
        + 2 * tm * tn * 2 + 4 * tm * tn * 4
    return pl.pallas_call(
        functools.partial(_merge_body, chunk),
        out_shape=jax.ShapeDtypeStruct((t, d), BF16),
        grid=(d // tn, t // tm),
        in_specs=[pl.BlockSpec((tm, width), lambda j, i: (i, 0))] * N_BRANCH
        + [gate_spec(n) for n in range(N_BRANCH)]
        + [pl.BlockSpec((None, N_BRANCH, width, tn), lambda j, i: (layer, 0, 0, j))],
        out_specs=pl.BlockSpec((tm, tn), lambda j, i: (i, j)),
        scratch_shapes=[pltpu.VMEM((N_BRANCH, width, tn), BF16)],
        compiler_params=_params(("arbitrary", "arbitrary"), vmem),
        name="merge",
    )(*branches, *([rest] * N_BRANCH), w_branch)


def _up_body(chunk, h_ref, wg_ref, wu_ref, o_ref, wgb_ref, wub_ref):
    @pl.when(pl.program_id(1) == 0)
    def _():
        _cast_rows(lambda rows: wg_ref[rows, :], wgb_ref, wg_ref.shape[0], chunk)
        _cast_rows(lambda rows: wu_ref[rows, :], wub_ref, wu_ref.shape[0], chunk)

    h = h_ref[...]
    g = jnp.dot(h, wgb_ref[...], preferred_element_type=F32)
    u = jnp.dot(h, wub_ref[...], preferred_element_type=F32)
    o_ref[...] = (g * jax.nn.sigmoid(g) * u).astype(o_ref.dtype)


def _swiglu_up(h, wg, wu, tm=512):
    t, k = h.shape
    n_e, _, f = wg.shape
    tn = _pick_tile(f, (256, 128))
    tm = min(tm, t)
    nj = f // tn
    chunk = _pick_tile(k, (256, 128, 8))
    w_spec = pl.BlockSpec((None, k, tn), lambda j, i: (j // nj, 0, j % nj))
    vmem = 2 * tm * k * 2 + 4 * k * tn * 4 + 2 * k * tn * 2 + 2 * tm * tn * 2 + 4 * tm * tn * 4
    return pl.pallas_call(
        functools.partial(_up_body, chunk),
        out_shape=jax.ShapeDtypeStruct((t, n_e * f), BF16),
        grid=(n_e * nj, t // tm),
        in_specs=[pl.BlockSpec((tm, k), lambda j, i: (i, 0)), w_spec, w_spec],
        out_specs=pl.BlockSpec((tm, tn), lambda j, i: (i, j)),
        scratch_shapes=[pltpu.VMEM((k, tn), BF16), pltpu.VMEM((k, tn), BF16)],
        compiler_params=_params(("arbitrary", "arbitrary"), vmem),
        name="swiglu_up",
    )(h, wg, wu)


def _down_body(tm, chunk, a_ref, w_ref, o_ref, wb_ref):
    kk = pl.program_id(1)
    i = pl.program_id(2)

    @pl.when(i == 0)
    def _():
        _cast_rows(lambda rows: w_ref[rows, :], wb_ref, w_ref.shape[0], chunk)

    part = jnp.dot(a_ref[...], wb_ref[...], preferred_element_type=F32)
    rows = pl.ds(pl.multiple_of(i * tm, tm), tm)

    @pl.when(kk == 0)
    def _():
        o_ref[rows, :] = part

    @pl.when(kk > 0)
    def _():
        o_ref[rows, :] += part


def _swiglu_down(a, wd, tk, tn=256, tm=512):
    t, k = a.shape
    n = wd.shape[1]
    tn = min(tn, n)
    tm = min(tm, t)
    assert k % tk == 0 and n % tn == 0 and t % tm == 0
    chunk = _pick_tile(tk, (256, 128, 8))
    in_specs = [pl.BlockSpec((tm, tk), lambda j, kk, i: (i, kk)),
                pl.BlockSpec((tk, tn), lambda j, kk, i: (kk, j))]
    args = [a, wd]
    vmem = 2 * tm * tk * 2 + 2 * tk * tn * 4 + tk * tn * 2 + 2 * t * tn * 4 + 4 * tm * tn * 4
    return pl.pallas_call(
        functools.partial(_down_body, tm, chunk),
        out_shape=jax.ShapeDtypeStruct((t, n), F32),
        grid=(n // tn, k // tk, t // tm),
        in_specs=in_specs,
        out_specs=pl.BlockSpec((t, tn), lambda j, kk, i: (0, j)),
        scratch_shapes=[pltpu.VMEM((tk, tn), BF16)],
        compiler_params=_params(("arbitrary", "arbitrary", "arbitrary"), vmem),
        name="swiglu_down",
    )(*args)


MOE_TILE = 256


def _dispatch_body(n_tok, pos1_ref, pos2_ref, h_ref, o_ref, src_ref, buf_ref, sem):
    r = pl.program_id(0)

    def row_copy(tile, slot, k):
        return pltpu.make_async_copy(h_ref.at[pl.ds(src_ref[tile * MOE_TILE + k], 1), :],
                                     buf_ref.at[slot, pl.ds(k, 1), :], sem.at[slot])

    def start_tile(tile, slot):
        lax.fori_loop(0, MOE_TILE, lambda k, c: (row_copy(tile, slot, k).start(), c)[1], 0,
                      unroll=8)

    @pl.when(r == 0)
    def _():
        def clear(p, c):
            src_ref[p] = 0
            return c

        def invert(t, c):
            src_ref[pos1_ref[t]] = t
            src_ref[pos2_ref[t]] = t
            return c

        lax.fori_loop(0, src_ref.shape[0], clear, 0)
        lax.fori_loop(0, n_tok, invert, 0)
        start_tile(0, 0)

    @pl.when(r + 1 < pl.num_programs(0))
    def _():
        start_tile(r + 1, (r + 1) % 2)

    slot = r % 2
    pltpu.make_async_copy(h_ref.at[pl.ds(0, MOE_TILE), :], buf_ref.at[slot], sem.at[slot]).wait()
    o_ref[...] = buf_ref[slot]


def _moe_dispatch(hp, pos1, pos2, n_rows):
    t, w = hp.shape
    assert n_rows % MOE_TILE == 0
    return pl.pallas_call(
        functools.partial(_dispatch_body, t),
        out_shape=jax.ShapeDtypeStruct((n_rows, w), hp.dtype),
        grid_spec=pltpu.PrefetchScalarGridSpec(
            num_scalar_prefetch=2,
            grid=(n_rows // MOE_TILE,),
            in_specs=[pl.BlockSpec(memory_space=pl.ANY)],
            out_specs=pl.BlockSpec((MOE_TILE, w), lambda r, p1, p2: (r, 0)),
            scratch_shapes=[pltpu.SMEM((n_rows,), jnp.int32),
                            pltpu.VMEM((2, MOE_TILE, w), hp.dtype),
                            pltpu.SemaphoreType.DMA((2,))]),
        compiler_params=_params(("arbitrary",), 6 * MOE_TILE * w * 4),
        name="moe_dispatch",
    )(pos1, pos2, hp)


def _group_body(swiglu, chunk, te_ref, nused_ref, *refs):
    if swiglu:
        x_ref, wg_ref, wu_ref, o_ref, wgb_ref, wub_ref = refs
        pairs = ((wg_ref, wgb_ref), (wu_ref, wub_ref))
    else:
        x_ref, w_ref, o_ref, wb_ref = refs
        pairs = ((w_ref, wb_ref),)
    r = pl.program_id(1)
    expert = te_ref[r]
    prev = te_ref[jnp.maximum(r - 1, 0)]

    @pl.when(jnp.logical_or(r == 0, expert != prev))
    def _():
        for src_ref, dst_ref in pairs:
            _cast_rows(lambda rows, s=src_ref: s[rows, :], dst_ref, src_ref.shape[0], chunk)

    @pl.when(r < nused_ref[0])
    def _():
        if swiglu:
            lo, hi = _unpack_bf16_pairs(x_ref[...])
            half = lo.shape[1]

            def proj(wb):
                return jnp.dot(lo, wb[0:half, :], preferred_element_type=F32) \
                    + jnp.dot(hi, wb[half:, :], preferred_element_type=F32)

            g = proj(wgb_ref)
            o_ref[...] = (g * jax.nn.sigmoid(g) * proj(wub_ref)).astype(o_ref.dtype)
        else:
            o_ref[...] = jnp.dot(x_ref[...], wb_ref[...],
                                 preferred_element_type=F32).astype(o_ref.dtype)

    @pl.when(r >= nused_ref[0])
    def _():
        o_ref[...] = jnp.zeros_like(o_ref)


def _grouped_matmul(xs, weights, layer, tile_expert, n_used, out_dtype, tn, name):
    p, kx = xs.shape
    swiglu = len(weights) == 2
    _, _, k, n = weights[0].shape
    assert k == (2 * kx if swiglu else kx) and n % tn == 0 and p % MOE_TILE == 0
    chunk = _pick_tile(k, (256, 128, 8))
    w_spec = pl.BlockSpec((None, None, k, tn), lambda j, r, te, nu: (layer, te[r], 0, j))
    osz = jnp.dtype(out_dtype).itemsize
    vmem = 2 * MOE_TILE * kx * xs.dtype.itemsize + len(weights) * (2 * k * tn * 4 + k * tn * 2) \
        + 2 * MOE_TILE * tn * osz + 6 * MOE_TILE * tn * 4 + 2 * MOE_TILE * k * 2
    return pl.pallas_call(
        functools.partial(_group_body, swiglu, chunk),
        out_shape=jax.ShapeDtypeStruct((p, n), out_dtype),
        grid_spec=pltpu.PrefetchScalarGridSpec(
            num_scalar_prefetch=2,
            grid=(n // tn, p // MOE_TILE),
            in_specs=[pl.BlockSpec((MOE_TILE, kx), lambda j, r, te, nu: (r, 0))]
            + [w_spec] * len(weights),
            out_specs=pl.BlockSpec((MOE_TILE, tn), lambda j, r, te, nu: (r, j)),
            scratch_shapes=[pltpu.VMEM((k, tn), BF16)] * len(weights)),
        compiler_params=_params(("arbitrary", "arbitrary"), vmem),
        name=name,
    )(tile_expert, n_used, xs, *weights)


def _combine_body(tc, n_experts, pos1_ref, pos2_ref, route_ref, ye_ref, o_ref, buf_ref, sem):
    i = pl.program_id(0)

    def row_copies(tile, slot, k):
        t = tile * tc + k
        return (pltpu.make_async_copy(ye_ref.at[pl.ds(pos1_ref[t], 1), :],
                                      buf_ref.at[slot, 0, pl.ds(k, 1), :], sem.at[slot]),
                pltpu.make_async_copy(ye_ref.at[pl.ds(pos2_ref[t], 1), :],
                                      buf_ref.at[slot, 1, pl.ds(k, 1), :], sem.at[slot]))

    def start_tile(tile, slot):
        def body(k, c):
            for cp in row_copies(tile, slot, k):
                cp.start()
            return c
        lax.fori_loop(0, tc, body, 0, unroll=8)

    def wait_tile(slot):
        for half in range(2):
            pltpu.make_async_copy(ye_ref.at[pl.ds(0, tc), :], buf_ref.at[slot, half],
                                  sem.at[slot]).wait()

    @pl.when(i == 0)
    def _():
        start_tile(0, 0)

    @pl.when(i + 1 < pl.num_programs(0))
    def _():
        start_tile(i + 1, (i + 1) % 2)

    slot = i % 2
    wait_tile(slot)
    g1 = route_ref[:, n_experts + 2:n_experts + 3]
    g2 = route_ref[:, n_experts + 3:n_experts + 4]
    o_ref[...] = g1 * buf_ref[slot, 0] + g2 * buf_ref[slot, 1]


def _moe_combine(ye, route, pos1, pos2, n_experts, tc=128):
    t = route.shape[0]
    d = ye.shape[1]
    tc = min(tc, t)
    assert t % tc == 0
    return pl.pallas_call(
        functools.partial(_combine_body, tc, n_experts),
        out_shape=jax.ShapeDtypeStruct((t, d), F32),
        grid_spec=pltpu.PrefetchScalarGridSpec(
            num_scalar_prefetch=2,
            grid=(t // tc,),
            in_specs=[pl.BlockSpec((tc, LANE), lambda i, p1, p2: (i, 0)),
                      pl.BlockSpec(memory_space=pl.ANY)],
            out_specs=pl.BlockSpec((tc, d), lambda i, p1, p2: (i, 0)),
            scratch_shapes=[pltpu.VMEM((2, 2, tc, d), F32), pltpu.SemaphoreType.DMA((2,))]),
        compiler_params=_params(("arbitrary",), 4 * tc * d * 4 + 4 * tc * d * 4),
        name="moe_combine",
    )(pos1, pos2, route, ye)


def _moe(hp, route, rank, counts, layer_idx, w_gate, w_up, w_down):
    t = hp.shape[0]
    n_e = w_gate.shape[1]
    n_rows = TOP_K * t + n_e * MOE_TILE
    n_tiles = n_rows // MOE_TILE
    i1 = route[:, n_e].astype(jnp.int32)
    i2 = route[:, n_e + 1].astype(jnp.int32)
    cnt = counts[0, :n_e].astype(jnp.int32)
    tiles = (cnt + MOE_TILE - 1) // MOE_TILE
    tile_end = jnp.cumsum(tiles)
    row0 = (tile_end - tiles) * MOE_TILE
    slot = row0[None, :] + rank[:, :n_e].astype(jnp.int32)
    experts = jnp.arange(n_e, dtype=jnp.int32)[None, :]
    pos1 = jnp.sum(jnp.where(experts == i1[:, None], slot, 0), axis=1).astype(jnp.int32)
    pos2 = jnp.sum(jnp.where(experts == i2[:, None], slot, 0), axis=1).astype(jnp.int32)
    n_used = tile_end[-1:].astype(jnp.int32)
    tile_ids = jnp.minimum(jnp.arange(n_tiles, dtype=jnp.int32), n_used[0] - 1)
    tile_expert = jnp.sum(tile_ids[:, None] >= tile_end[None, :], axis=1).astype(jnp.int32)

    xs = _moe_dispatch(hp, pos1, pos2, n_rows)
    f = w_gate.shape[-1]
    a = _grouped_matmul(xs, (w_gate, w_up), layer_idx, tile_expert, n_used, BF16,
                        _pick_tile(f, (512, 256, 128)), "moe_up")
    ye = _grouped_matmul(a, (w_down,), layer_idx, tile_expert, n_used, F32,
                         _pick_tile(w_down.shape[-1], (1024, 512, 256, 128)), "moe_down")
    return _moe_combine(ye, route, pos1, pos2, n_e)


def _rel_bucket(dist):
    max_exact = REL_BUCKETS // 2
    d = jnp.maximum(dist, 0)
    df = jnp.maximum(d, max_exact).astype(F32)
    large = max_exact + (jnp.log(df / max_exact) / math.log(REL_MAX_DIST / max_exact)
                         * (REL_BUCKETS - max_exact)).astype(jnp.int32)
    large = jnp.minimum(large, REL_BUCKETS - 1)
    return jnp.where(d < max_exact, d, large)


def _bias_body(n_heads, head0, head_stride, bucket_ref, tab_ref, o_ref):
    bucket = bucket_ref[...]
    base = head0 + head_stride * pl.program_id(0)
    for h in range(n_heads):
        acc = jnp.zeros(bucket.shape, F32)
        for b in range(REL_BUCKETS):
            acc = jnp.where(bucket == b, tab_ref[b, base + h], acc)
        o_ref[h] = acc


def _bias_tiles(rel_bias, buckets, n_heads, head0, head_stride):
    n_tiles, q, k = buckets.shape
    return pl.pallas_call(
        functools.partial(_bias_body, n_heads, head0, head_stride),
        out_shape=jax.ShapeDtypeStruct((n_tiles, n_heads, q, k), F32),
        grid=(n_tiles,),
        in_specs=[pl.BlockSpec((None, q, k), lambda t: (t, 0, 0)),
                  pl.BlockSpec(memory_space=pltpu.SMEM)],
        out_specs=pl.BlockSpec((None, n_heads, q, k), lambda t: (t, 0, 0, 0)),
        compiler_params=_params(("arbitrary",), 4 * (n_heads + 1) * q * k * 4),
        name="bias_tiles",
    )(buckets, rel_bias)


def _bias_tiles_a(rel_bias, n_heads, seq):
    n_blk = seq // A_BLOCK
    qi = jnp.arange(A_BLOCK)[:, None]
    kj = jnp.arange(A_BLOCK)[None, :]
    dist = jnp.arange(n_blk)[:, None, None] * A_BLOCK + (qi - kj)[None]
    return _bias_tiles(rel_bias, _rel_bucket(dist), n_heads, 0, 0)


def _bias_tiles_b(rel_bias, head0, n_heads):
    qi = jnp.arange(B_BLOCK)[:, None]
    kj = jnp.arange(2 * B_BLOCK)[None, :]
    sub_dist = qi + B_BLOCK - kj
    buckets = jnp.stack([_rel_bucket(sub_dist * dil) for _, dil in B_GROUPS])
    return _bias_tiles(rel_bias, buckets, n_heads, head0, n_heads)


def _hybrid_mixer(h, bsz, seq, layer, w_in_t, lat_g, w_uk, w_uv, c_ln_g, c_ln_b, c_w_s, c_b_s,
                  d_conv, w_branch, w_out, bias_a, bias_b):
    t, d = h.shape
    width = d // N_BRANCH
    n_lat = w_uk.shape[0]
    a_heads = width // HEAD_DIM
    n_groups = len(B_GROUPS)
    iq_width = IDX_HEADS * IDX_DIM
    lat_col = width
    iq_col = lat_col + n_lat
    ik_col = iq_col + iq_width
    front = ik_col
    shift = IDX_DIM + IDX_HEADS
    qkv_w = 3 * n_groups * width
    rest_w = qkv_w + 2 * width + 3 * width + N_BRANCH * d
    assert front % 512 == 0 and w_in_t.shape[1] == front + shift + rest_w and shift <= LANE

    pa = _matmul_t(h, w_in_t, layer, row0=0, n_out=front, tm=1024, name="proj_front")
    pik = _matmul_t(h, w_in_t, layer, row0=front, n_out=LANE, name="proj_index_key")
    rest = _matmul_t(h, w_in_t, layer, row0=front + shift, n_out=rest_w,
                     tn=_pick_tile(rest_w, (768, 512, 256, 128)), name="proj_rest")

    pa3 = pa.reshape(bsz, seq, front)
    pik3 = pik.reshape(bsz, seq, LANE)
    rest3 = rest.reshape(bsz, seq, rest_w)

    lat, latt, ikt = _latnorm(pa3, pik3, lat_g, lat_col, n_lat)
    wuk_t = jnp.transpose(w_uk, (1, 2, 0)).astype(BF16)
    wuv_t = jnp.transpose(w_uv, (1, 0, 2)).astype(BF16)
    o_a = _dsa(pa3, pik3, lat, latt, ikt, wuk_t, wuv_t, bias_a,
               q_width=width, iq_col=iq_col, iq_width=iq_width,
               n_variants=DSA_KEY_RANGE_VARIANTS).reshape(t, width)

    outs, lses = [], []
    for g, (window, dil) in enumerate(B_GROUPS):
        o, l = _dilated(rest3, bias_b, g, n_groups, window, dil, width)
        outs.append(o)
        lses.append(l)
    o_b = _mixture(outs, lses)

    o_c = _gmlp(rest, qkv_w, width, c_ln_g, c_ln_b, c_w_s, c_b_s.T)
    o_d = _short_conv(rest3, qkv_w + 2 * width, width, d_conv).reshape(t, width)

    mixed = _merge([o_a, o_b, o_c, o_d], rest, qkv_w + 5 * width, w_branch, layer)
    return _matmul(mixed, w_out, layer, tm=1024, name="proj_out")


def kernel(x, c, w_ada, b_ada, ada_table, rel_bias, norm_pre_mix, norm_post_mix, norm_pre_ffn,
           norm_post_ffn, w_in, a_lat_norm, a_w_uk, a_w_uv, c_ln_g, c_ln_b, c_w_s, c_b_s, d_conv,
           w_branch, w_out, ffn_w_gate, ffn_w_up, ffn_w_down, moe_router, moe_w_gate, moe_w_up,
           moe_w_down):
    bsz, seq, d = x.shape
    depth = w_in.shape[0]
    t = bsz * seq

    c_pad = jnp.pad(c, ((0, 16 - bsz % 16 if bsz % 16 else 0), (0, 0)))
    mod_shared = _ada(c_pad, w_ada, b_ada)[:bsz].reshape(bsz, N_MOD, d)
    mods = [mod_shared + ada_table[layer] for layer in range(depth)]

    w_in_t = jnp.transpose(w_in, (0, 2, 1))
    width = d // N_BRANCH
    bias_a = _bias_tiles_a(rel_bias, width // HEAD_DIM, seq)
    bias_b = _bias_tiles_b(rel_bias, width // HEAD_DIM, width // HEAD_DIM)

    h = _prenorm(x, mods[0], norm_pre_mix[0])
    for layer in range(depth):
        mod = mods[layer]
        y = _hybrid_mixer(h.reshape(t, d), bsz, seq, layer, w_in_t, a_lat_norm[layer],
                          a_w_uk[layer], a_w_uv[layer], c_ln_g[layer], c_ln_b[layer],
                          c_w_s[layer], c_b_s[layer], d_conv[layer], w_branch, w_out,
                          bias_a, bias_b)
        j = layer // 2
        dense = layer % 2 == 0
        res = _postnorm(x, y.reshape(bsz, seq, d), mod, norm_post_mix[layer], 2, nxt=(3, 4),
                        mod_next=mod, g_pre=norm_pre_ffn[layer],
                        router_t=None if dense else moe_router[j].T)
        if dense:
            x, h = res
            a = _swiglu_up(h.reshape(t, d), ffn_w_gate[j][None], ffn_w_up[j][None])
            f = ffn_w_down.shape[1]
            tk = f // 2 if (f // 2) % LANE == 0 and f > 4096 else f
            y = _swiglu_down(a, ffn_w_down[j], tk)
        else:
            x, hp, route, rank, counts = res
            y = _moe(hp.reshape(t, d // 2), route.reshape(t, LANE), rank.reshape(t, LANE), counts,
                     j, moe_w_gate, moe_w_up, moe_w_down)
        y3 = y.reshape(bsz, seq, d)
        if layer + 1 < depth:
            x, h = _postnorm(x, y3, mod, norm_post_ffn[layer], 5, nxt=(0, 1),
                             mod_next=mods[layer + 1], g_pre=norm_pre_mix[layer + 1])
        else:
            (x,) = _postnorm(x, y3, mod, norm_post_ffn[layer], 5)
    return x
```

```python
import functools
import math

import jax
import jax.numpy as jnp
from jax import lax
from jax.experimental import pallas as pl
from jax.experimental.pallas import tpu as pltpu

F32 = jnp.float32
BF16 = jnp.bfloat16

LANE = 128
V7X_VMEM_BYTES = 64 * 1024 * 1024
VMEM_CAP = V7X_VMEM_BYTES - 8 * 1024 * 1024

HEAD_DIM = 128
N_BRANCH = 4
IDX_HEADS = 8
IDX_DIM = 64
IDX_TOPK = 256
A_BLOCK = 128
DSA_KEY_RANGE_VARIANTS = 4
B_GROUPS = ((128, 1), (512, 4), (2048, 16))
B_BLOCK = 128
C_CHUNK = 128
C_GROUPS = 8
D_CONV_WIDTH = 3
REL_BUCKETS = 32
REL_MAX_DIST = 2048
TOP_K = 2
N_MOD = 6
EPS = 1e-6
NEG_INF = float("-inf")


def _params(semantics, vmem_bytes):
    limit = int(min(VMEM_CAP, max(vmem_bytes * 5 // 4 + (4 << 20), 16 << 20)))
    return pltpu.CompilerParams(dimension_semantics=semantics, vmem_limit_bytes=limit)


def _pick_tile(n, candidates):
    for c in candidates:
        if n % c == 0:
            return c
    raise ValueError(f"no tile in {candidates} divides {n}")


def _cast_rows(src_fn, dst_ref, n_rows, chunk):
    def body(r, carry):
        rows = pl.ds(pl.multiple_of(r * chunk, chunk), chunk)
        dst_ref[rows, :] = src_fn(rows).astype(BF16)
        return carry
    lax.fori_loop(0, n_rows // chunk, body, 0)


def _mm_body(chunk, lhs_ref, w_ref, o_ref, wb_ref):
    @pl.when(pl.program_id(1) == 0)
    def _():
        _cast_rows(lambda rows: w_ref[rows, :], wb_ref, w_ref.shape[0], chunk)

    o_ref[...] = jnp.dot(lhs_ref[...], wb_ref[...],
                         preferred_element_type=F32).astype(o_ref.dtype)


def _matmul(lhs, w, layer, *, tn=512, tm=512, out_dtype=F32, name="mm"):
    m, k = lhs.shape
    n_out = w.shape[2]
    tn = min(tn, n_out)
    tm = min(tm, m)
    assert n_out % tn == 0 and m % tm == 0 and tn % LANE == 0
    chunk = _pick_tile(k, (256, 128, 64, 8))
    osz = jnp.dtype(out_dtype).itemsize
    vmem = 2 * tm * k * 2 + 2 * k * tn * 4 + k * tn * 2 + 2 * tm * tn * osz + tm * tn * 4 \
        + chunk * tn * 8
    return pl.pallas_call(
        functools.partial(_mm_body, chunk),
        out_shape=jax.ShapeDtypeStruct((m, n_out), out_dtype),
        grid=(n_out // tn, m // tm),
        in_specs=[pl.BlockSpec((tm, k), lambda j, i: (i, 0)),
                  pl.BlockSpec((None, k, tn), lambda j, i: (layer, 0, j))],
        out_specs=pl.BlockSpec((tm, tn), lambda j, i: (i, j)),
        scratch_shapes=[pltpu.VMEM((k, tn), BF16)],
        compiler_params=_params(("arbitrary", "arbitrary"), vmem),
        name=name,
    )(lhs, w)


def _mm_t_body(chunk, lhs_ref, w_ref, o_ref, wb_ref):
    @pl.when(pl.program_id(1) == 0)
    def _():
        _cast_rows(lambda rows: w_ref[rows, :], wb_ref, w_ref.shape[0], chunk)

    acc = lax.dot_general(lhs_ref[...], wb_ref[...], (((1,), (1,)), ((), ())),
                          preferred_element_type=F32)
    o_ref[...] = acc.astype(o_ref.dtype)


def _matmul_t(lhs, w_t, layer, *, row0, n_out, tn=512, tm=512, out_dtype=F32, name="mm_t"):
    m, k = lhs.shape
    tn = min(tn, n_out)
    tm = min(tm, m)
    assert n_out % tn == 0 and m % tm == 0 and row0 % 8 == 0 and tn % LANE == 0
    chunk = _pick_tile(tn, (256, 128))
    osz = jnp.dtype(out_dtype).itemsize
    vmem = 2 * tm * k * 2 + 2 * k * tn * 4 + k * tn * 2 + 2 * tm * tn * osz + tm * tn * 4 \
        + 4 * chunk * k * 4
    return pl.pallas_call(
        functools.partial(_mm_t_body, chunk),
        out_shape=jax.ShapeDtypeStruct((m, n_out), out_dtype),
        grid=(n_out // tn, m // tm),
        in_specs=[pl.BlockSpec((tm, k), lambda j, i: (i, 0)),
                  pl.BlockSpec((None, pl.Element(tn), pl.Element(k)),
                               lambda j, i: (layer, pl.multiple_of(row0 + j * tn, 8), 0))],
        out_specs=pl.BlockSpec((tm, tn), lambda j, i: (i, j)),
        scratch_shapes=[pltpu.VMEM((tn, k), BF16)],
        compiler_params=_params(("arbitrary", "arbitrary"), vmem),
        name=name,
    )(lhs, w_t)


def _ada_body(c_ref, w_ref, b_ref, o_ref):
    c = c_ref[...]
    a = (c * jax.nn.sigmoid(c)).astype(BF16)
    o_ref[...] = jnp.dot(a, w_ref[...].astype(BF16), preferred_element_type=F32) + b_ref[...]


def _ada(c_pad, w_ada, b_ada, tn=512):
    m, k = c_pad.shape
    n = w_ada.shape[1]
    tn = _pick_tile(n, (tn, 256, 128))
    vmem = 2 * k * tn * 4 + k * tn * 2 + 4 * m * k * 4
    return pl.pallas_call(
        _ada_body,
        out_shape=jax.ShapeDtypeStruct((m, n), F32),
        grid=(n // tn,),
        in_specs=[pl.BlockSpec((m, k), lambda j: (0, 0)),
                  pl.BlockSpec((k, tn), lambda j: (0, j)),
                  pl.BlockSpec((1, tn), lambda j: (0, j))],
        out_specs=pl.BlockSpec((m, tn), lambda j: (0, j)),
        compiler_params=_params(("arbitrary",), vmem),
        name="ada",
    )(c_pad, w_ada, b_ada.reshape(1, n))


def _rms(x, g):
    return x * lax.rsqrt(jnp.mean(x * x, axis=-1, keepdims=True) + EPS) * g


def _route(h, router_ref, n_experts):
    rows = h.shape[0]
    lane = lax.broadcasted_iota(jnp.int32, (rows, LANE), 1)
    logits = jnp.full((rows, LANE), NEG_INF, F32)
    for e in range(n_experts):
        le = jnp.sum(h * router_ref[e:e + 1, :], axis=-1, keepdims=True)
        logits = jnp.where(lane == e, le, logits)
    m1 = jnp.max(logits, axis=-1, keepdims=True)
    i1 = jnp.min(jnp.where(logits == m1, lane, LANE), axis=-1, keepdims=True)
    rest = jnp.where(lane == i1, NEG_INF, logits)
    m2 = jnp.max(rest, axis=-1, keepdims=True)
    i2 = jnp.min(jnp.where(rest == m2, lane, LANE), axis=-1, keepdims=True)
    e2 = jnp.exp(m2 - m1)
    den = 1.0 + e2
    member = jnp.where(lane == i1, 1.0, jnp.where(lane == i2, 1.0, 0.0))
    extra = jnp.where(lane == n_experts, i1.astype(F32),
                      jnp.where(lane == n_experts + 1, i2.astype(F32),
                                jnp.where(lane == n_experts + 2, 1.0 / den, e2 / den)))
    return jnp.where(lane < n_experts, member, jnp.where(lane < n_experts + 4, extra, 0.0))


def _pack_bf16_pairs(h):
    half = h.shape[1] // 2
    lo = pltpu.bitcast(h[:, :half].astype(BF16).astype(F32), jnp.uint32)
    hi = pltpu.bitcast(h[:, half:].astype(BF16).astype(F32), jnp.uint32)
    return (hi & jnp.uint32(0xFFFF0000)) | (lo >> 16)


def _unpack_bf16_pairs(u):
    lo = pltpu.bitcast(u << 16, F32).astype(BF16)
    hi = pltpu.bitcast(u & jnp.uint32(0xFFFF0000), F32).astype(BF16)
    return lo, hi


def _pre_body(x_ref, mod_ref, g_ref, h_ref):
    h = _rms(x_ref[...], g_ref[...]) * (1.0 + mod_ref[1:2, :]) + mod_ref[0:1, :]
    h_ref[...] = h.astype(h_ref.dtype)


def _prenorm(x3, mod, g, tr=256):
    b, s, d = x3.shape
    tr = min(tr, s)
    vmem = 2 * tr * d * 4 + 2 * tr * d * 2 + 4 * tr * d * 4
    return pl.pallas_call(
        _pre_body,
        out_shape=jax.ShapeDtypeStruct((b, s, d), BF16),
        grid=(b, s // tr),
        in_specs=[pl.BlockSpec((None, tr, d), lambda bi, i: (bi, i, 0)),
                  pl.BlockSpec((None, N_MOD, d), lambda bi, i: (bi, 0, 0)),
                  pl.BlockSpec((1, d), lambda bi, i: (0, 0))],
        out_specs=pl.BlockSpec((None, tr, d), lambda bi, i: (bi, i, 0)),
        compiler_params=_params(("arbitrary", "arbitrary"), vmem),
        name="prenorm",
    )(x3, mod, g.reshape(1, d))


def _post_body(gate_row, nxt, n_experts, *refs):
    x_ref, y_ref, mod_ref, gpost_ref = refs[:4]
    pos = 4
    if nxt is not None:
        modn_ref, gpre_ref = refs[pos:pos + 2]
        pos += 2
    if n_experts:
        router_ref = refs[pos]
        pos += 1
    xo_ref = refs[pos]
    pos += 1
    xn = x_ref[...] + mod_ref[gate_row:gate_row + 1, :] * _rms(y_ref[...], gpost_ref[...])
    xo_ref[...] = xn
    if nxt is not None:
        shift_row, scale_row = nxt
        h = _rms(xn, gpre_ref[...]) * (1.0 + modn_ref[scale_row:scale_row + 1, :]) \
            + modn_ref[shift_row:shift_row + 1, :]
        if not n_experts:
            refs[pos][...] = h.astype(BF16)
            return
        hp_ref, route_ref, rank_ref, counts_ref, carry_ref = refs[pos:pos + 5]
        hp_ref[...] = _pack_bf16_pairs(h)
        route = _route(h, router_ref, n_experts)
        route_ref[...] = route

        @pl.when((pl.program_id(0) == 0) & (pl.program_id(1) == 0))
        def _():
            carry_ref[...] = jnp.zeros_like(carry_ref)

        rows = route.shape[0]
        lane = lax.broadcasted_iota(jnp.int32, route.shape, 1)
        member = jnp.where(lane < n_experts, route, 0.0)
        ri = lax.broadcasted_iota(jnp.int32, (rows, rows), 0)
        ci = lax.broadcasted_iota(jnp.int32, (rows, rows), 1)
        earlier = jnp.where(ci < ri, 1.0, 0.0).astype(BF16)
        within = jnp.dot(earlier, member.astype(BF16), preferred_element_type=F32)
        rank_ref[...] = within + carry_ref[0:1, :]
        carry_ref[...] = carry_ref[...] + jnp.sum(member, axis=0, keepdims=True)
        counts_ref[...] = carry_ref[...]


def _postnorm(x3, y3, mod, g_post, gate_row, *, nxt=None, mod_next=None, g_pre=None,
              router_t=None, tr=256):
    b, s, d = x3.shape
    tr = min(tr, s)
    n_experts = 0 if router_t is None else router_t.shape[0]
    row = lambda bi, i: (bi, i, 0)
    mod_spec = pl.BlockSpec((None, N_MOD, d), lambda bi, i: (bi, 0, 0))
    in_specs = [pl.BlockSpec((None, tr, d), row), pl.BlockSpec((None, tr, d), row), mod_spec,
                pl.BlockSpec((1, d), lambda bi, i: (0, 0))]
    args = [x3, y3, mod, g_post.reshape(1, d)]
    out_shape = [jax.ShapeDtypeStruct((b, s, d), F32)]
    out_specs = [pl.BlockSpec((None, tr, d), row)]
    scratch = []
    if nxt is not None:
        in_specs += [mod_spec, pl.BlockSpec((1, d), lambda bi, i: (0, 0))]
        args += [mod_next, g_pre.reshape(1, d)]
    if nxt is not None and not n_experts:
        out_shape.append(jax.ShapeDtypeStruct((b, s, d), BF16))
        out_specs.append(pl.BlockSpec((None, tr, d), row))
    if n_experts:
        assert nxt is not None and n_experts + 4 <= LANE
        in_specs.append(pl.BlockSpec((n_experts, d), lambda bi, i: (0, 0)))
        args.append(router_t)
        out_shape += [jax.ShapeDtypeStruct((b, s, d // 2), jnp.uint32),
                      jax.ShapeDtypeStruct((b, s, LANE), F32),
                      jax.ShapeDtypeStruct((b, s, LANE), F32),
                      jax.ShapeDtypeStruct((8, LANE), F32)]
        out_specs += [pl.BlockSpec((None, tr, d // 2), row), pl.BlockSpec((None, tr, LANE), row),
                      pl.BlockSpec((None, tr, LANE), row),
                      pl.BlockSpec((8, LANE), lambda bi, i: (0, 0))]
        scratch.append(pltpu.VMEM((8, LANE), F32))
    vmem = 2 * 3 * tr * d * 4 + 2 * tr * d * 2 + 6 * tr * d * 4
    return pl.pallas_call(
        functools.partial(_post_body, gate_row, nxt, n_experts),
        out_shape=out_shape,
        grid=(b, s // tr),
        in_specs=in_specs,
        out_specs=out_specs,
        scratch_shapes=scratch,
        compiler_params=_params(("arbitrary", "arbitrary"), vmem),
        name="postnorm",
    )(*args)


def _lat_body(alat_ref, pik_ref, g_ref, lat_ref, latt_ref, ikt_ref):
    lat = _rms(alat_ref[...], g_ref[...])
    lat_ref[...] = lat.astype(BF16)
    latt_ref[...] = lat.T.astype(BF16)
    ikt_ref[...] = pik_ref[...].T.astype(BF16)


def _latnorm(pa3, pik3, g_lat, lat_col, n_lat, tr=512):
    b, s, _ = pa3.shape
    tr = min(tr, s)
    assert lat_col % n_lat == 0
    vmem = 2 * tr * (n_lat + LANE) * 4 + 4 * tr * (n_lat + LANE) * 2 + 4 * tr * n_lat * 4
    return pl.pallas_call(
        _lat_body,
        out_shape=[jax.ShapeDtypeStruct((b, s, n_lat), BF16),
                   jax.ShapeDtypeStruct((b, n_lat, s), BF16),
                   jax.ShapeDtypeStruct((b, LANE, s), BF16)],
        grid=(b, s // tr),
        in_specs=[pl.BlockSpec((None, tr, n_lat), lambda bi, i: (bi, i, lat_col // n_lat)),
                  pl.BlockSpec((None, tr, LANE), lambda bi, i: (bi, i, 0)),
                  pl.BlockSpec((1, n_lat), lambda bi, i: (0, 0))],
        out_specs=[pl.BlockSpec((None, tr, n_lat), lambda bi, i: (bi, i, 0)),
                   pl.BlockSpec((None, n_lat, tr), lambda bi, i: (bi, 0, i)),
                   pl.BlockSpec((None, LANE, tr), lambda bi, i: (bi, 0, i))],
        compiler_params=_params(("arbitrary", "arbitrary"), vmem),
        name="latnorm",
    )(pa3, pik3, g_lat.reshape(1, n_lat))


def _dsa_body(topk, n_variants, *refs):
    i = pl.program_id(1)
    n_blk = refs[4].shape[0] // A_BLOCK
    per = -(-n_blk // n_variants)
    for v in range(n_variants):
        blocks = min(n_blk, (v + 1) * per)

        @pl.when((i >= v * per) & (i < (v + 1) * per))
        def _(blocks=blocks):
            _dsa_compute(topk, blocks * A_BLOCK, i, *refs)


def _dsa_compute(topk, seq, i, q_ref, iq_ref, iw_ref, ikt_ref, lat_ref, latt_ref, wuk_ref, wuv_ref,
                 bias_ref, o_ref, madd_ref):
    n_blk = seq // A_BLOCK
    n_heads = wuk_ref.shape[0]
    int_min = jnp.int32(-2 ** 31)
    qpos = i * A_BLOCK + lax.broadcasted_iota(jnp.int32, (A_BLOCK, 1), 0)
    kpos = lax.broadcasted_iota(jnp.int32, (A_BLOCK, seq), 1)

    iq = iq_ref[...].astype(BF16)
    iw = iw_ref[...]
    ikt = ikt_ref[0:IDX_DIM, 0:seq]
    score = jnp.zeros((A_BLOCK, seq), F32)
    for h in range(IDX_HEADS):
        l = jnp.dot(iq[:, h * IDX_DIM:(h + 1) * IDX_DIM], ikt, preferred_element_type=F32)
        score = score + iw[:, IDX_DIM + h:IDX_DIM + h + 1] * jnp.maximum(l, 0.0)

    score = jnp.where(score == 0.0, 0.0, score)
    key = pltpu.bitcast(score, jnp.int32)
    key = jnp.where(key < 0, key ^ jnp.int32(0x7FFFFFFF), key)
    key = jnp.where(kpos <= qpos, key, int_min)
    k_eff = jnp.minimum(qpos + 1, topk).astype(F32)

    def count(mask):
        return jnp.sum(jnp.where(mask, 1.0, 0.0), axis=1, keepdims=True)

    t0 = jnp.where(count(key >= 0) >= k_eff, jnp.int32(0), int_min)

    def value_step(it, t):
        c = t | (jnp.int32(1) << (30 - it))
        return jnp.where(count(key >= c) >= k_eff, c, t)

    t = lax.fori_loop(0, 31, value_step, t0)

    need = k_eff - count(key > t)
    eq = key == t
    idx_bits = max(1, (seq - 1).bit_length())

    def index_step(it, p):
        c = p | (jnp.int32(1) << (idx_bits - 1 - it))
        below = jnp.sum(jnp.where(eq, jnp.where(kpos < c, 1.0, 0.0), 0.0), axis=1, keepdims=True)
        return jnp.where(below < need, c, p)

    p = lax.fori_loop(0, idx_bits, index_step, jnp.zeros((A_BLOCK, 1), jnp.int32))
    tie_add = jnp.where(eq, jnp.where(kpos <= p, 0.0, NEG_INF), NEG_INF)
    madd_ref[:, 0:seq] = jnp.where(key > t, 0.0, tie_add)

    scale = HEAD_DIM ** -0.5
    for h in range(n_heads):
        cols = slice(h * HEAD_DIM, (h + 1) * HEAD_DIM)
        qa = jnp.dot(q_ref[:, cols].astype(BF16), wuk_ref[h],
                     preferred_element_type=F32).astype(BF16)
        lg = jnp.dot(qa, latt_ref[:, 0:seq], preferred_element_type=F32) * scale
        bias = jnp.concatenate([bias_ref[jnp.maximum(i - j, 0), h] for j in range(n_blk)], axis=1)
        lg = lg + bias + madd_ref[:, 0:seq]
        m = jnp.max(lg, axis=1, keepdims=True)
        e = jnp.exp(lg - m)
        den = jnp.sum(e, axis=1, keepdims=True)
        o_lat = jnp.dot(e.astype(BF16), lat_ref[0:seq, :], preferred_element_type=F32) / den
        o_ref[:, cols] = jnp.dot(o_lat.astype(BF16), wuv_ref[h],
                                 preferred_element_type=F32).astype(o_ref.dtype)


def _dsa(pa3, pik3, lat, latt, ikt, wuk_t, wuv_t, bias_a, *, q_width, iq_col, iq_width,
         n_variants):
    b, s, _ = pa3.shape
    n_lat = lat.shape[-1]
    n_heads = wuk_t.shape[0]
    n_blk = s // A_BLOCK
    topk = min(IDX_TOPK, s // 4)
    assert iq_col % iq_width == 0
    vmem = 2 * (A_BLOCK * (q_width + iq_width + LANE) * 4 + LANE * s * 2 + 2 * s * n_lat * 2
                + 2 * n_heads * HEAD_DIM * n_lat * 2 + n_heads * n_blk * A_BLOCK * A_BLOCK * 4
                + A_BLOCK * q_width * 2) + 10 * A_BLOCK * s * 4
    return pl.pallas_call(
        functools.partial(_dsa_body, topk, n_variants),
        out_shape=jax.ShapeDtypeStruct((b, s, q_width), BF16),
        grid=(b, n_blk),
        in_specs=[pl.BlockSpec((None, A_BLOCK, q_width), lambda bi, i: (bi, i, 0)),
                  pl.BlockSpec((None, A_BLOCK, iq_width), lambda bi, i: (bi, i, iq_col // iq_width)),
                  pl.BlockSpec((None, A_BLOCK, LANE), lambda bi, i: (bi, i, 0)),
                  pl.BlockSpec((None, LANE, s), lambda bi, i: (bi, 0, 0)),
                  pl.BlockSpec((None, s, n_lat), lambda bi, i: (bi, 0, 0)),
                  pl.BlockSpec((None, n_lat, s), lambda bi, i: (bi, 0, 0)),
                  pl.BlockSpec(wuk_t.shape, lambda bi, i: (0, 0, 0)),
                  pl.BlockSpec(wuv_t.shape, lambda bi, i: (0, 0, 0)),
                  pl.BlockSpec(bias_a.shape, lambda bi, i: (0, 0, 0, 0))],
        out_specs=pl.BlockSpec((None, A_BLOCK, q_width), lambda bi, i: (bi, i, 0)),
        scratch_shapes=[pltpu.VMEM((A_BLOCK, s), F32)],
        compiler_params=_params(("arbitrary", "arbitrary"), vmem),
        name="dsa",
    )(pa3, pa3, pik3, ikt, lat, latt, wuk_t, wuv_t, bias_a)


def _dil_body(span, dil, hb, has_prev, *refs):
    if has_prev:
        q_ref, kp_ref, kc_ref, vp_ref, vc_ref, bias_ref, o_ref, lse_ref = refs
    else:
        q_ref, kc_ref, vc_ref, bias_ref, o_ref, lse_ref = refs
    i = pl.program_id(1)
    hblk = pl.program_id(2)
    n_keys = 2 * B_BLOCK if has_prev else B_BLOCK
    key0 = 0 if has_prev else B_BLOCK
    qi = lax.broadcasted_iota(jnp.int32, (B_BLOCK, n_keys), 0)
    kj = lax.broadcasted_iota(jnp.int32, (B_BLOCK, n_keys), 1) + key0
    sub_dist = qi + B_BLOCK - kj
    in_band = jnp.where(sub_dist <= span,
                        jnp.where((i - 1) * B_BLOCK + kj >= 0, 0.0, NEG_INF), NEG_INF)
    madd = jnp.where(sub_dist >= 0, in_band, NEG_INF)
    lane = lax.broadcasted_iota(jnp.int32, (B_BLOCK, LANE), 1)
    scale = HEAD_DIM ** -0.5

    @pl.when(hblk == 0)
    def _():
        lse_ref[...] = jnp.zeros_like(lse_ref)

    for r in range(dil):
        rows = pl.ds(r, B_BLOCK, stride=dil) if dil > 1 else slice(None)
        lse_acc = lse_ref[rows, :]
        for h in range(hb):
            cols = slice(h * HEAD_DIM, (h + 1) * HEAD_DIM)
            q = q_ref[rows, cols].astype(BF16)
            if has_prev:
                k2 = jnp.concatenate([kp_ref[rows, cols], kc_ref[rows, cols]], axis=0).astype(BF16)
                v2 = jnp.concatenate([vp_ref[rows, cols], vc_ref[rows, cols]], axis=0).astype(BF16)
            else:
                k2 = kc_ref[rows, cols].astype(BF16)
                v2 = vc_ref[rows, cols].astype(BF16)
            lg = lax.dot_general(q, k2, (((1,), (1,)), ((), ())), preferred_element_type=F32)
            lg = lg * scale + bias_ref[h, :, key0:] + madd
            m = jnp.max(lg, axis=1, keepdims=True)
            e = jnp.exp(lg - m)
            den = jnp.sum(e, axis=1, keepdims=True)
            o_ref[rows, cols] = jnp.dot((e / den).astype(BF16), v2, preferred_element_type=F32)
            lse_acc = jnp.where(lane == hblk * hb + h, m + jnp.log(den), lse_acc)
        lse_ref[rows, :] = lse_acc


def _dilated(rest3, bias_b, g, n_groups, window, dil, width):
    b, s, w_all = rest3.shape
    n_heads = width // HEAD_DIM
    span = window // dil
    n_sub = s // dil
    assert span == B_BLOCK and n_sub % B_BLOCK == 0 and w_all % width == 0
    nb = n_sub // B_BLOCK
    rows = B_BLOCK * dil
    hb = n_heads if dil == 1 else 1
    assert HEAD_DIM == LANE
    n_hblk = n_heads // hb
    has_prev = nb > 1
    cur = lambda slot: (lambda bi, i, hk: (bi, i, (slot * n_groups + g) * n_hblk + hk))
    prev = lambda slot: (lambda bi, i, hk: (bi, jnp.maximum(i - 1, 0),
                                            (slot * n_groups + g) * n_hblk + hk))
    blk = (None, rows, hb * HEAD_DIM)
    in_specs = [pl.BlockSpec(blk, cur(0))]
    for slot in (1, 2):
        if has_prev:
            in_specs.append(pl.BlockSpec(blk, prev(slot)))
        in_specs.append(pl.BlockSpec(blk, cur(slot)))
    in_specs.append(pl.BlockSpec((None, hb, B_BLOCK, 2 * B_BLOCK), lambda bi, i, hk: (g, hk, 0, 0)))
    vmem = 2 * (len(in_specs) * rows * hb * HEAD_DIM + rows * LANE) * 4 \
        + 2 * hb * B_BLOCK * 2 * B_BLOCK * 4 + (4 << 20)
    o, lse = pl.pallas_call(
        functools.partial(_dil_body, span, dil, hb, has_prev),
        out_shape=[jax.ShapeDtypeStruct((b, s, width), F32),
                   jax.ShapeDtypeStruct((b, s, LANE), F32)],
        grid=(b, nb, n_hblk),
        in_specs=in_specs,
        out_specs=[pl.BlockSpec(blk, lambda bi, i, hk: (bi, i, hk)),
                   pl.BlockSpec((None, rows, LANE), lambda bi, i, hk: (bi, i, 0))],
        compiler_params=_params(("arbitrary", "arbitrary", "arbitrary"), vmem),
        name=f"dilated{g}",
    )(*([rest3] * (len(in_specs) - 1)), bias_b)
    return o.reshape(b * s, width), lse.reshape(b * s, LANE)


def _mix_body(n_groups, n_heads, *refs):
    o_refs = refs[:n_groups]
    l_refs = refs[n_groups:2 * n_groups]
    out_ref = refs[2 * n_groups]
    ls = [r[...] for r in l_refs]
    m = functools.reduce(jnp.maximum, ls)
    es = [jnp.exp(l - m) for l in ls]
    tot = functools.reduce(jnp.add, es)
    ws = [e / tot for e in es]
    for h in range(n_heads):
        cols = slice(h * HEAD_DIM, (h + 1) * HEAD_DIM)
        acc = ws[0][:, h:h + 1] * o_refs[0][:, cols]
        for gi in range(1, n_groups):
            acc = acc + ws[gi][:, h:h + 1] * o_refs[gi][:, cols]
        out_ref[:, cols] = acc.astype(out_ref.dtype)


def _mixture(outs, lses, tr=512):
    t, width = outs[0].shape
    tr = min(tr, t)
    n_groups = len(outs)
    vmem = 2 * n_groups * tr * (width + LANE) * 4 + 2 * tr * width * 2 + 4 * tr * width * 4
    return pl.pallas_call(
        functools.partial(_mix_body, n_groups, width // HEAD_DIM),
        out_shape=jax.ShapeDtypeStruct((t, width), BF16),
        grid=(t // tr,),
        in_specs=[pl.BlockSpec((tr, width), lambda i: (i, 0))] * n_groups
        + [pl.BlockSpec((tr, LANE), lambda i: (i, 0))] * n_groups,
        out_specs=pl.BlockSpec((tr, width), lambda i: (i, 0)),
        compiler_params=_params(("arbitrary",), vmem),
        name="mixture",
    )(*outs, *lses)


def _gmlp_body(u_ref, v_ref, g_ref, b_ref, ws_ref, bs_ref, o_ref):
    u = jax.nn.gelu(u_ref[...])
    v = jax.nn.gelu(v_ref[...])
    mu = jnp.mean(v, axis=-1, keepdims=True)
    var = jnp.mean(jnp.square(v - mu), axis=-1, keepdims=True)
    v = (v - mu) * lax.rsqrt(var + EPS) * g_ref[...] + b_ref[...]
    ti = lax.broadcasted_iota(jnp.int32, (C_CHUNK, C_CHUNK), 0)
    si = lax.broadcasted_iota(jnp.int32, (C_CHUNK, C_CHUNK), 1)
    causal = si <= ti
    gw = u.shape[1] // C_GROUPS
    for g in range(C_GROUPS):
        cols = slice(g * gw, (g + 1) * gw)
        w = jnp.where(causal, ws_ref[g], 0.0).astype(BF16)
        mixed = jnp.dot(w, v[:, cols].astype(BF16), preferred_element_type=F32) + bs_ref[:, g:g + 1]
        o_ref[:, cols] = (u[:, cols] * mixed).astype(o_ref.dtype)


def _gmlp(rest, u_col, width, ln_g, ln_b, w_s, b_s_t):
    t = rest.shape[0]
    assert u_col % width == 0 and (width // C_GROUPS) % LANE == 0
    vmem = 2 * 2 * C_CHUNK * width * 4 + 2 * C_CHUNK * width * 2 + 2 * C_GROUPS * C_CHUNK * C_CHUNK * 4 \
        + 8 * C_CHUNK * width * 4
    return pl.pallas_call(
        _gmlp_body,
        out_shape=jax.ShapeDtypeStruct((t, width), BF16),
        grid=(t // C_CHUNK,),
        in_specs=[pl.BlockSpec((C_CHUNK, width), lambda i: (i, u_col // width)),
                  pl.BlockSpec((C_CHUNK, width), lambda i: (i, u_col // width + 1)),
                  pl.BlockSpec((1, width), lambda i: (0, 0)),
                  pl.BlockSpec((1, width), lambda i: (0, 0)),
                  pl.BlockSpec(w_s.shape, lambda i: (0, 0, 0)),
                  pl.BlockSpec(b_s_t.shape, lambda i: (0, 0))],
        out_specs=pl.BlockSpec((C_CHUNK, width), lambda i: (i, 0)),
        compiler_params=_params(("arbitrary",), vmem),
        name="gmlp",
    )(rest, rest, ln_g.reshape(1, width), ln_b.reshape(1, width), w_s, b_s_t)


def _conv_body(h_ref, bg_ref, cg_ref, w_ref, o_ref, carry_ref):
    @pl.when(pl.program_id(1) == 0)
    def _():
        carry_ref[...] = jnp.zeros_like(carry_ref)

    ch = cg_ref[...] * h_ref[...]
    rows = ch.shape[0]
    row = lax.broadcasted_iota(jnp.int32, ch.shape, 0)
    pm1 = carry_ref[7:8, :]
    pm2 = carry_ref[6:7, :]
    s1 = jnp.where(row == 0, pm1, pltpu.roll(ch, 1, 0))
    s2 = jnp.where(row == 0, pm2, jnp.where(row == 1, pm1, pltpu.roll(ch, 2, 0)))
    z = w_ref[0:1, :] * s2 + w_ref[1:2, :] * s1 + w_ref[2:3, :] * ch
    o_ref[...] = (bg_ref[...] * z).astype(o_ref.dtype)
    carry_ref[...] = ch[rows - 8:, :]


def _short_conv(rest3, h_col, width, conv_w, tr=256):
    b, s, _ = rest3.shape
    tr = min(tr, s)
    assert h_col % width == 0 and conv_w.shape[0] == D_CONV_WIDTH
    cb = h_col // width
    vmem = 2 * 3 * tr * width * 4 + 2 * tr * width * 2 + 6 * tr * width * 4
    return pl.pallas_call(
        _conv_body,
        out_shape=jax.ShapeDtypeStruct((b, s, width), BF16),
        grid=(b, s // tr),
        in_specs=[pl.BlockSpec((None, tr, width), lambda bi, i: (bi, i, cb)),
                  pl.BlockSpec((None, tr, width), lambda bi, i: (bi, i, cb + 1)),
                  pl.BlockSpec((None, tr, width), lambda bi, i: (bi, i, cb + 2)),
                  pl.BlockSpec(conv_w.shape, lambda bi, i: (0, 0))],
        out_specs=pl.BlockSpec((None, tr, width), lambda bi, i: (bi, i, 0)),
        scratch_shapes=[pltpu.VMEM((8, width), F32)],
        compiler_params=_params(("arbitrary", "arbitrary"), vmem),
        name="short_conv",
    )(rest3, rest3, rest3, conv_w)


def _merge_body(chunk, *refs):
    br_refs = refs[:N_BRANCH]
    gate_refs = refs[N_BRANCH:2 * N_BRANCH]
    w_ref, o_ref, wb_ref = refs[2 * N_BRANCH:]

    @pl.when(pl.program_id(1) == 0)
    def _():
        for n in range(N_BRANCH):
            _cast_rows(lambda rows, n=n: w_ref[n, rows, :], wb_ref.at[n], w_ref.shape[1], chunk)

    acc = None
    for n in range(N_BRANCH):
        proj = jnp.dot(br_refs[n][...], wb_ref[n], preferred_element_type=F32)
        term = jax.nn.sigmoid(gate_refs[n][...]) * proj
        acc = term if acc is None else acc + term
    o_ref[...] = acc.astype(o_ref.dtype)


def _merge(branches, rest, gate_col, w_branch, layer, tn=512, tm=512):
    t, width = branches[0].shape
    d = w_branch.shape[-1]
    tn = min(tn, d)
    tm = min(tm, t)
    assert d % tn == 0 and t % tm == 0 and gate_col % tn == 0
    chunk = _pick_tile(width, (256, 128, 8))
    gate_spec = lambda n: pl.BlockSpec((tm, tn), lambda j, i: (i, (gate_col + n * d) // tn + j))
    vmem = 2 * N_BRANCH * (tm * width * 2 + tm * tn * 4 + width * tn * 4) + N_BRANCH * width * tn * 2 \
        + 2 * tm * tn * 2 + 4 * tm * tn * 4
    return pl.pallas_call(
        functools.partial(_merge_body, chunk),
        out_shape=jax.ShapeDtypeStruct((t, d), BF16),
        grid=(d // tn, t // tm),
        in_specs=[pl.BlockSpec((tm, width), lambda j, i: (i, 0))] * N_BRANCH
        + [gate_spec(n) for n in range(N_BRANCH)]
        + [pl.BlockSpec((None, N_BRANCH, width, tn), lambda j, i: (layer, 0, 0, j))],
        out_specs=pl.BlockSpec((tm, tn), lambda j, i: (i, j)),
        scratch_shapes=[pltpu.VMEM((N_BRANCH, width, tn), BF16)],
        compiler_params=_params(("arbitrary", "arbitrary"), vmem),
        name="merge",
    )(*branches, *([rest] * N_BRANCH), w_branch)


def _up_body(chunk, h_ref, wg_ref, wu_ref, o_ref, wgb_ref, wub_ref):
    @pl.when(pl.program_id(1) == 0)
    def _():
        _cast_rows(lambda rows: wg_ref[rows, :], wgb_ref, wg_ref.shape[0], chunk)
        _cast_rows(lambda rows: wu_ref[rows, :], wub_ref, wu_ref.shape[0], chunk)

    h = h_ref[...]
    g = jnp.dot(h, wgb_ref[...], preferred_element_type=F32)
    u = jnp.dot(h, wub_ref[...], preferred_element_type=F32)
    o_ref[...] = (g * jax.nn.sigmoid(g) * u).astype(o_ref.dtype)


def _swiglu_up(h, wg, wu, tm=1024):
    t, k = h.shape
    n_e, _, f = wg.shape
    tn = _pick_tile(f, (256, 128))
    tm = min(tm, t)
    nj = f // tn
    chunk = _pick_tile(k, (256, 128, 8))
    w_spec = pl.BlockSpec((None, k, tn), lambda j, i: (j // nj, 0, j % nj))
    vmem = 2 * tm * k * 2 + 4 * k * tn * 4 + 2 * k * tn * 2 + 2 * tm * tn * 2 + 4 * tm * tn * 4
    return pl.pallas_call(
        functools.partial(_up_body, chunk),
        out_shape=jax.ShapeDtypeStruct((t, n_e * f), BF16),
        grid=(n_e * nj, t // tm),
        in_specs=[pl.BlockSpec((tm, k), lambda j, i: (i, 0)), w_spec, w_spec],
        out_specs=pl.BlockSpec((tm, tn), lambda j, i: (i, j)),
        scratch_shapes=[pltpu.VMEM((k, tn), BF16), pltpu.VMEM((k, tn), BF16)],
        compiler_params=_params(("arbitrary", "arbitrary"), vmem),
        name="swiglu_up",
    )(h, wg, wu)


def _round_body(w_ref, o_ref):
    o_ref[...] = w_ref[...].astype(o_ref.dtype)


def _round_bf16(w, layer, tr=512):
    _, k, n = w.shape
    tr = _pick_tile(k, (tr, 256, 128, 8))
    return pl.pallas_call(
        _round_body,
        out_shape=jax.ShapeDtypeStruct((k, n), BF16),
        grid=(k // tr,),
        in_specs=[pl.BlockSpec((None, tr, n), lambda i: (layer, i, 0))],
        out_specs=pl.BlockSpec((tr, n), lambda i: (i, 0)),
        compiler_params=_params(("arbitrary",), 2 * tr * n * 6),
        name="round_bf16",
    )(w)


def _down_body(a_ref, w_ref, o_ref):
    o_ref[...] = jnp.dot(a_ref[...], w_ref[...], preferred_element_type=F32)


def _swiglu_down(a, wd, layer, tn=512, tm=512):
    t, k = a.shape
    n = wd.shape[2]
    tn = min(tn, n)
    tm = min(tm, t)
    assert n % tn == 0 and t % tm == 0
    wb = _round_bf16(wd, layer)
    vmem = 2 * tm * k * 2 + 2 * k * tn * 2 + 2 * tm * tn * 4 + tm * tn * 4
    return pl.pallas_call(
        _down_body,
        out_shape=jax.ShapeDtypeStruct((t, n), F32),
        grid=(n // tn, t // tm),
        in_specs=[pl.BlockSpec((tm, k), lambda j, i: (i, 0)),
                  pl.BlockSpec((k, tn), lambda j, i: (0, j))],
        out_specs=pl.BlockSpec((tm, tn), lambda j, i: (i, j)),
        compiler_params=_params(("arbitrary", "arbitrary"), vmem),
        name="swiglu_down",
    )(a, wb)


MOE_TILE = 256


def _dispatch_body(n_tok, pos1_ref, pos2_ref, h_ref, o_ref, src_ref, buf_ref, sem):
    r = pl.program_id(0)

    def row_copy(tile, slot, k):
        return pltpu.make_async_copy(h_ref.at[pl.ds(src_ref[tile * MOE_TILE + k], 1), :],
                                     buf_ref.at[slot, pl.ds(k, 1), :], sem.at[slot])

    def start_tile(tile, slot):
        lax.fori_loop(0, MOE_TILE, lambda k, c: (row_copy(tile, slot, k).start(), c)[1], 0,
                      unroll=8)

    @pl.when(r == 0)
    def _():
        def clear(p, c):
            src_ref[p] = 0
            return c

        def invert(t, c):
            src_ref[pos1_ref[t]] = t
            src_ref[pos2_ref[t]] = t
            return c

        lax.fori_loop(0, src_ref.shape[0], clear, 0)
        lax.fori_loop(0, n_tok, invert, 0)
        start_tile(0, 0)

    @pl.when(r + 1 < pl.num_programs(0))
    def _():
        start_tile(r + 1, (r + 1) % 2)

    slot = r % 2
    pltpu.make_async_copy(h_ref.at[pl.ds(0, MOE_TILE), :], buf_ref.at[slot], sem.at[slot]).wait()
    o_ref[...] = buf_ref[slot]


def _moe_dispatch(hp, pos1, pos2, n_rows):
    t, w = hp.shape
    assert n_rows % MOE_TILE == 0
    return pl.pallas_call(
        functools.partial(_dispatch_body, t),
        out_shape=jax.ShapeDtypeStruct((n_rows, w), hp.dtype),
        grid_spec=pltpu.PrefetchScalarGridSpec(
            num_scalar_prefetch=2,
            grid=(n_rows // MOE_TILE,),
            in_specs=[pl.BlockSpec(memory_space=pl.ANY)],
            out_specs=pl.BlockSpec((MOE_TILE, w), lambda r, p1, p2: (r, 0)),
            scratch_shapes=[pltpu.SMEM((n_rows,), jnp.int32),
                            pltpu.VMEM((2, MOE_TILE, w), hp.dtype),
                            pltpu.SemaphoreType.DMA((2,))]),
        compiler_params=_params(("arbitrary",), 6 * MOE_TILE * w * 4),
        name="moe_dispatch",
    )(pos1, pos2, hp)


def _group_body(swiglu, chunk, te_ref, nused_ref, *refs):
    if swiglu:
        x_ref, wg_ref, wu_ref, o_ref, wgb_ref, wub_ref = refs
        pairs = ((wg_ref, wgb_ref), (wu_ref, wub_ref))
    else:
        x_ref, w_ref, o_ref, wb_ref = refs
        pairs = ((w_ref, wb_ref),)
    r = pl.program_id(1)
    expert = te_ref[r]
    prev = te_ref[jnp.maximum(r - 1, 0)]

    @pl.when(jnp.logical_or(r == 0, expert != prev))
    def _():
        for src_ref, dst_ref in pairs:
            _cast_rows(lambda rows, s=src_ref: s[rows, :], dst_ref, src_ref.shape[0], chunk)

    @pl.when(r < nused_ref[0])
    def _():
        if swiglu:
            lo, hi = _unpack_bf16_pairs(x_ref[...])
            half = lo.shape[1]

            def proj(wb):
                return jnp.dot(lo, wb[0:half, :], preferred_element_type=F32) \
                    + jnp.dot(hi, wb[half:, :], preferred_element_type=F32)

            g = proj(wgb_ref)
            o_ref[...] = (g * jax.nn.sigmoid(g) * proj(wub_ref)).astype(o_ref.dtype)
        else:
            o_ref[...] = jnp.dot(x_ref[...], wb_ref[...],
                                 preferred_element_type=F32).astype(o_ref.dtype)

    @pl.when(r >= nused_ref[0])
    def _():
        o_ref[...] = jnp.zeros_like(o_ref)


def _grouped_matmul(xs, weights, layer, tile_expert, n_used, out_dtype, tn, name):
    p, kx = xs.shape
    swiglu = len(weights) == 2
    _, _, k, n = weights[0].shape
    assert k == (2 * kx if swiglu else kx) and n % tn == 0 and p % MOE_TILE == 0
    chunk = _pick_tile(k, (256, 128, 8))
    w_spec = pl.BlockSpec((None, None, k, tn), lambda j, r, te, nu: (layer, te[r], 0, j))
    osz = jnp.dtype(out_dtype).itemsize
    vmem = 2 * MOE_TILE * kx * xs.dtype.itemsize + len(weights) * (2 * k * tn * 4 + k * tn * 2) \
        + 2 * MOE_TILE * tn * osz + 6 * MOE_TILE * tn * 4 + 2 * MOE_TILE * k * 2
    return pl.pallas_call(
        functools.partial(_group_body, swiglu, chunk),
        out_shape=jax.ShapeDtypeStruct((p, n), out_dtype),
        grid_spec=pltpu.PrefetchScalarGridSpec(
            num_scalar_prefetch=2,
            grid=(n // tn, p // MOE_TILE),
            in_specs=[pl.BlockSpec((MOE_TILE, kx), lambda j, r, te, nu: (r, 0))]
            + [w_spec] * len(weights),
            out_specs=pl.BlockSpec((MOE_TILE, tn), lambda j, r, te, nu: (r, j)),
            scratch_shapes=[pltpu.VMEM((k, tn), BF16)] * len(weights)),
        compiler_params=_params(("arbitrary", "arbitrary"), vmem),
        name=name,
    )(tile_expert, n_used, xs, *weights)


def _combine_body(tc, n_experts, pos1_ref, pos2_ref, route_ref, ye_ref, o_ref, buf_ref, sem):
    i = pl.program_id(0)

    def row_copies(tile, slot, k):
        t = tile * tc + k
        return (pltpu.make_async_copy(ye_ref.at[pl.ds(pos1_ref[t], 1), :],
                                      buf_ref.at[slot, 0, pl.ds(k, 1), :], sem.at[slot]),
                pltpu.make_async_copy(ye_ref.at[pl.ds(pos2_ref[t], 1), :],
                                      buf_ref.at[slot, 1, pl.ds(k, 1), :], sem.at[slot]))

    def start_tile(tile, slot):
        def body(k, c):
            for cp in row_copies(tile, slot, k):
                cp.start()
            return c
        lax.fori_loop(0, tc, body, 0, unroll=8)

    def wait_tile(slot):
        for half in range(2):
            pltpu.make_async_copy(ye_ref.at[pl.ds(0, tc), :], buf_ref.at[slot, half],
                                  sem.at[slot]).wait()

    @pl.when(i == 0)
    def _():
        start_tile(0, 0)

    @pl.when(i + 1 < pl.num_programs(0))
    def _():
        start_tile(i + 1, (i + 1) % 2)

    slot = i % 2
    wait_tile(slot)
    g1 = route_ref[:, n_experts + 2:n_experts + 3]
    g2 = route_ref[:, n_experts + 3:n_experts + 4]
    o_ref[...] = g1 * buf_ref[slot, 0] + g2 * buf_ref[slot, 1]


def _moe_combine(ye, route, pos1, pos2, n_experts, tc=128):
    t = route.shape[0]
    d = ye.shape[1]
    tc = min(tc, t)
    assert t % tc == 0
    return pl.pallas_call(
        functools.partial(_combine_body, tc, n_experts),
        out_shape=jax.ShapeDtypeStruct((t, d), F32),
        grid_spec=pltpu.PrefetchScalarGridSpec(
            num_scalar_prefetch=2,
            grid=(t // tc,),
            in_specs=[pl.BlockSpec((tc, LANE), lambda i, p1, p2: (i, 0)),
                      pl.BlockSpec(memory_space=pl.ANY)],
            out_specs=pl.BlockSpec((tc, d), lambda i, p1, p2: (i, 0)),
            scratch_shapes=[pltpu.VMEM((2, 2, tc, d), F32), pltpu.SemaphoreType.DMA((2,))]),
        compiler_params=_params(("arbitrary",), 4 * tc * d * 4 + 4 * tc * d * 4),
        name="moe_combine",
    )(pos1, pos2, route, ye)


def _moe(hp, route, rank, counts, layer_idx, w_gate, w_up, w_down):
    t = hp.shape[0]
    n_e = w_gate.shape[1]
    n_rows = TOP_K * t + n_e * MOE_TILE
    n_tiles = n_rows // MOE_TILE
    i1 = route[:, n_e].astype(jnp.int32)
    i2 = route[:, n_e + 1].astype(jnp.int32)
    cnt = counts[0, :n_e].astype(jnp.int32)
    tiles = (cnt + MOE_TILE - 1) // MOE_TILE
    tile_end = jnp.cumsum(tiles)
    row0 = (tile_end - tiles) * MOE_TILE
    slot = row0[None, :] + rank[:, :n_e].astype(jnp.int32)
    experts = jnp.arange(n_e, dtype=jnp.int32)[None, :]
    pos1 = jnp.sum(jnp.where(experts == i1[:, None], slot, 0), axis=1).astype(jnp.int32)
    pos2 = jnp.sum(jnp.where(experts == i2[:, None], slot, 0), axis=1).astype(jnp.int32)
    n_used = tile_end[-1:].astype(jnp.int32)
    tile_ids = jnp.minimum(jnp.arange(n_tiles, dtype=jnp.int32), n_used[0] - 1)
    tile_expert = jnp.sum(tile_ids[:, None] >= tile_end[None, :], axis=1).astype(jnp.int32)

    xs = _moe_dispatch(hp, pos1, pos2, n_rows)
    f = w_gate.shape[-1]
    a = _grouped_matmul(xs, (w_gate, w_up), layer_idx, tile_expert, n_used, BF16,
                        _pick_tile(f, (512, 256, 128)), "moe_up")
    ye = _grouped_matmul(a, (w_down,), layer_idx, tile_expert, n_used, F32,
                         _pick_tile(w_down.shape[-1], (1024, 512, 256, 128)), "moe_down")
    return _moe_combine(ye, route, pos1, pos2, n_e)


def _rel_bucket(dist):
    max_exact = REL_BUCKETS // 2
    d = jnp.maximum(dist, 0)
    df = jnp.maximum(d, max_exact).astype(F32)
    large = max_exact + (jnp.log(df / max_exact) / math.log(REL_MAX_DIST / max_exact)
                         * (REL_BUCKETS - max_exact)).astype(jnp.int32)
    large = jnp.minimum(large, REL_BUCKETS - 1)
    return jnp.where(d < max_exact, d, large)


def _bias_body(n_heads, head0, head_stride, bucket_ref, tab_ref, o_ref):
    bucket = bucket_ref[...]
    base = head0 + head_stride * pl.program_id(0)
    for h in range(n_heads):
        acc = jnp.zeros(bucket.shape, F32)
        for b in range(REL_BUCKETS):
            acc = jnp.where(bucket == b, tab_ref[b, base + h], acc)
        o_ref[h] = acc


def _bias_tiles(rel_bias, buckets, n_heads, head0, head_stride):
    n_tiles, q, k = buckets.shape
    return pl.pallas_call(
        functools.partial(_bias_body, n_heads, head0, head_stride),
        out_shape=jax.ShapeDtypeStruct((n_tiles, n_heads, q, k), F32),
        grid=(n_tiles,),
        in_specs=[pl.BlockSpec((None, q, k), lambda t: (t, 0, 0)),
                  pl.BlockSpec(memory_space=pltpu.SMEM)],
        out_specs=pl.BlockSpec((None, n_heads, q, k), lambda t: (t, 0, 0, 0)),
        compiler_params=_params(("arbitrary",), 4 * (n_heads + 1) * q * k * 4),
        name="bias_tiles",
    )(buckets, rel_bias)


def _bias_tiles_a(rel_bias, n_heads, seq):
    n_blk = seq // A_BLOCK
    qi = jnp.arange(A_BLOCK)[:, None]
    kj = jnp.arange(A_BLOCK)[None, :]
    dist = jnp.arange(n_blk)[:, None, None] * A_BLOCK + (qi - kj)[None]
    return _bias_tiles(rel_bias, _rel_bucket(dist), n_heads, 0, 0)


def _bias_tiles_b(rel_bias, head0, n_heads):
    qi = jnp.arange(B_BLOCK)[:, None]
    kj = jnp.arange(2 * B_BLOCK)[None, :]
    sub_dist = qi + B_BLOCK - kj
    buckets = jnp.stack([_rel_bucket(sub_dist * dil) for _, dil in B_GROUPS])
    return _bias_tiles(rel_bias, buckets, n_heads, head0, n_heads)


def _hybrid_mixer(h, bsz, seq, layer, w_in_t, lat_g, w_uk, w_uv, c_ln_g, c_ln_b, c_w_s, c_b_s,
                  d_conv, w_branch, w_out, bias_a, bias_b):
    t, d = h.shape
    width = d // N_BRANCH
    n_lat = w_uk.shape[0]
    a_heads = width // HEAD_DIM
    n_groups = len(B_GROUPS)
    iq_width = IDX_HEADS * IDX_DIM
    lat_col = width
    iq_col = lat_col + n_lat
    ik_col = iq_col + iq_width
    front = ik_col
    shift = IDX_DIM + IDX_HEADS
    qkv_w = 3 * n_groups * width
    rest_w = qkv_w + 2 * width + 3 * width + N_BRANCH * d
    assert front % 512 == 0 and w_in_t.shape[1] == front + shift + rest_w and shift <= LANE

    pa = _matmul_t(h, w_in_t, layer, row0=0, n_out=front, tm=1024, name="proj_front")
    pik = _matmul_t(h, w_in_t, layer, row0=front, n_out=LANE, name="proj_index_key")
    rest = _matmul_t(h, w_in_t, layer, row0=front + shift, n_out=rest_w,
                     tn=_pick_tile(rest_w, (768, 512, 256, 128)), name="proj_rest")

    pa3 = pa.reshape(bsz, seq, front)
    pik3 = pik.reshape(bsz, seq, LANE)
    rest3 = rest.reshape(bsz, seq, rest_w)

    lat, latt, ikt = _latnorm(pa3, pik3, lat_g, lat_col, n_lat)
    wuk_t = jnp.transpose(w_uk, (1, 2, 0)).astype(BF16)
    wuv_t = jnp.transpose(w_uv, (1, 0, 2)).astype(BF16)
    o_a = _dsa(pa3, pik3, lat, latt, ikt, wuk_t, wuv_t, bias_a,
               q_width=width, iq_col=iq_col, iq_width=iq_width,
               n_variants=DSA_KEY_RANGE_VARIANTS).reshape(t, width)

    outs, lses = [], []
    for g, (window, dil) in enumerate(B_GROUPS):
        o, l = _dilated(rest3, bias_b, g, n_groups, window, dil, width)
        outs.append(o)
        lses.append(l)
    o_b = _mixture(outs, lses)

    o_c = _gmlp(rest, qkv_w, width, c_ln_g, c_ln_b, c_w_s, c_b_s.T)
    o_d = _short_conv(rest3, qkv_w + 2 * width, width, d_conv).reshape(t, width)

    mixed = _merge([o_a, o_b, o_c, o_d], rest, qkv_w + 5 * width, w_branch, layer)
    return _matmul(mixed, w_out, layer, tm=1024, name="proj_out")


def kernel(x, c, w_ada, b_ada, ada_table, rel_bias, norm_pre_mix, norm_post_mix, norm_pre_ffn,
           norm_post_ffn, w_in, a_lat_norm, a_w_uk, a_w_uv, c_ln_g, c_ln_b, c_w_s, c_b_s, d_conv,
           w_branch, w_out, ffn_w_gate, ffn_w_up, ffn_w_down, moe_router, moe_w_gate, moe_w_up,
           moe_w_down):
    bsz, seq, d = x.shape
    depth = w_in.shape[0]
    t = bsz * seq

    c_pad = jnp.pad(c, ((0, 16 - bsz % 16 if bsz % 16 else 0), (0, 0)))
    mod_shared = _ada(c_pad, w_ada, b_ada)[:bsz].reshape(bsz, N_MOD, d)
    mods = [mod_shared + ada_table[layer] for layer in range(depth)]

    w_in_t = jnp.transpose(w_in, (0, 2, 1))
    width = d // N_BRANCH
    bias_a = _bias_tiles_a(rel_bias, width // HEAD_DIM, seq)
    bias_b = _bias_tiles_b(rel_bias, width // HEAD_DIM, width // HEAD_DIM)

    h = _prenorm(x, mods[0], norm_pre_mix[0])
    for layer in range(depth):
        mod = mods[layer]
        y = _hybrid_mixer(h.reshape(t, d), bsz, seq, layer, w_in_t, a_lat_norm[layer],
                          a_w_uk[layer], a_w_uv[layer], c_ln_g[layer], c_ln_b[layer],
                          c_w_s[layer], c_b_s[layer], d_conv[layer], w_branch, w_out,
                          bias_a, bias_b)
        j = layer // 2
        dense = layer % 2 == 0
        res = _postnorm(x, y.reshape(bsz, seq, d), mod, norm_post_mix[layer], 2, nxt=(3, 4),
                        mod_next=mod, g_pre=norm_pre_ffn[layer],
                        router_t=None if dense else moe_router[j].T)
        if dense:
            x, h = res
            a = _swiglu_up(h.reshape(t, d), ffn_w_gate[j][None], ffn_w_up[j][None])
            y = _swiglu_down(a, ffn_w_down, j)
        else:
            x, hp, route, rank, counts = res
            y = _moe(hp.reshape(t, d // 2), route.reshape(t, LANE), rank.reshape(t, LANE), counts,
                     j, moe_w_gate, moe_w_up, moe_w_down)
        y3 = y.reshape(bsz, seq, d)
        if layer + 1 < depth:
            x, h = _postnorm(x, y3, mod, norm_post_ffn[layer], 5, nxt=(0, 1),
                             mod_next=mods[layer + 1], g_pre=norm_pre_mix[layer + 1])
        else:
            (x,) = _postnorm(x, y3, mod, norm_post_ffn[layer], 5)
    return x
```

```python
import functools
import math

import jax
import jax.numpy as jnp
from jax import lax
from jax.experimental import pallas as pl
from jax.experimental.pallas import tpu as pltpu

F32 = jnp.float32
BF16 = jnp.bfloat16

LANE = 128
V7X_VMEM_BYTES = 64 * 1024 * 1024
VMEM_CAP = V7X_VMEM_BYTES - 8 * 1024 * 1024

HEAD_DIM = 128
N_BRANCH = 4
IDX_HEADS = 8
IDX_DIM = 64
IDX_TOPK = 256
A_BLOCK = 128
DSA_KEY_RANGE_VARIANTS = 4
B_GROUPS = ((128, 1), (512, 4), (2048, 16))
B_BLOCK = 128
C_CHUNK = 128
C_GROUPS = 8
D_CONV_WIDTH = 3
REL_BUCKETS = 32
REL_MAX_DIST = 2048
TOP_K = 2
N_MOD = 6
EPS = 1e-6
NEG_INF = float("-inf")


def _params(semantics, vmem_bytes):
    limit = int(min(VMEM_CAP, max(vmem_bytes * 5 // 4 + (4 << 20), 16 << 20)))
    return pltpu.CompilerParams(dimension_semantics=semantics, vmem_limit_bytes=limit)


def _pick_tile(n, candidates):
    for c in candidates:
        if n % c == 0:
            return c
    raise ValueError(f"no tile in {candidates} divides {n}")


def _cast_rows(src_fn, dst_ref, n_rows, chunk):
    def body(r, carry):
        rows = pl.ds(pl.multiple_of(r * chunk, chunk), chunk)
        dst_ref[rows, :] = src_fn(rows).astype(BF16)
        return carry
    lax.fori_loop(0, n_rows // chunk, body, 0)


def _mm_body(chunk, lhs_ref, w_ref, o_ref, wb_ref):
    @pl.when(pl.program_id(1) == 0)
    def _():
        _cast_rows(lambda rows: w_ref[rows, :], wb_ref, w_ref.shape[0], chunk)

    o_ref[...] = jnp.dot(lhs_ref[...], wb_ref[...],
                         preferred_element_type=F32).astype(o_ref.dtype)


def _matmul(lhs, w, layer, *, tn=512, tm=512, out_dtype=F32, name="mm"):
    m, k = lhs.shape
    n_out = w.shape[2]
    tn = min(tn, n_out)
    tm = min(tm, m)
    assert n_out % tn == 0 and m % tm == 0 and tn % LANE == 0
    chunk = _pick_tile(k, (256, 128, 64, 8))
    osz = jnp.dtype(out_dtype).itemsize
    vmem = 2 * tm * k * 2 + 2 * k * tn * 4 + k * tn * 2 + 2 * tm * tn * osz + tm * tn * 4 \
        + chunk * tn * 8
    return pl.pallas_call(
        functools.partial(_mm_body, chunk),
        out_shape=jax.ShapeDtypeStruct((m, n_out), out_dtype),
        grid=(n_out // tn, m // tm),
        in_specs=[pl.BlockSpec((tm, k), lambda j, i: (i, 0)),
                  pl.BlockSpec((None, k, tn), lambda j, i: (layer, 0, j))],
        out_specs=pl.BlockSpec((tm, tn), lambda j, i: (i, j)),
        scratch_shapes=[pltpu.VMEM((k, tn), BF16)],
        compiler_params=_params(("arbitrary", "arbitrary"), vmem),
        name=name,
    )(lhs, w)


def _mm_t_body(chunk, lhs_ref, w_ref, o_ref, wb_ref):
    @pl.when(pl.program_id(1) == 0)
    def _():
        _cast_rows(lambda rows: w_ref[rows, :], wb_ref, w_ref.shape[0], chunk)

    acc = lax.dot_general(lhs_ref[...], wb_ref[...], (((1,), (1,)), ((), ())),
                          preferred_element_type=F32)
    o_ref[...] = acc.astype(o_ref.dtype)


def _matmul_t(lhs, w_t, layer, *, row0, n_out, tn=512, tm=512, out_dtype=F32, name="mm_t"):
    m, k = lhs.shape
    tn = min(tn, n_out)
    tm = min(tm, m)
    assert n_out % tn == 0 and m % tm == 0 and row0 % 8 == 0 and tn % LANE == 0
    chunk = _pick_tile(tn, (256, 128))
    osz = jnp.dtype(out_dtype).itemsize
    vmem = 2 * tm * k * 2 + 2 * k * tn * 4 + k * tn * 2 + 2 * tm * tn * osz + tm * tn * 4 \
        + 4 * chunk * k * 4
    return pl.pallas_call(
        functools.partial(_mm_t_body, chunk),
        out_shape=jax.ShapeDtypeStruct((m, n_out), out_dtype),
        grid=(n_out // tn, m // tm),
        in_specs=[pl.BlockSpec((tm, k), lambda j, i: (i, 0)),
                  pl.BlockSpec((None, pl.Element(tn), pl.Element(k)),
                               lambda j, i: (layer, pl.multiple_of(row0 + j * tn, 8), 0))],
        out_specs=pl.BlockSpec((tm, tn), lambda j, i: (i, j)),
        scratch_shapes=[pltpu.VMEM((tn, k), BF16)],
        compiler_params=_params(("arbitrary", "arbitrary"), vmem),
        name=name,
    )(lhs, w_t)


def _ada_body(c_ref, w_ref, b_ref, o_ref):
    c = c_ref[...]
    a = (c * jax.nn.sigmoid(c)).astype(BF16)
    o_ref[...] = jnp.dot(a, w_ref[...].astype(BF16), preferred_element_type=F32) + b_ref[...]


def _ada(c_pad, w_ada, b_ada, tn=512):
    m, k = c_pad.shape
    n = w_ada.shape[1]
    tn = _pick_tile(n, (tn, 256, 128))
    vmem = 2 * k * tn * 4 + k * tn * 2 + 4 * m * k * 4
    return pl.pallas_call(
        _ada_body,
        out_shape=jax.ShapeDtypeStruct((m, n), F32),
        grid=(n // tn,),
        in_specs=[pl.BlockSpec((m, k), lambda j: (0, 0)),
                  pl.BlockSpec((k, tn), lambda j: (0, j)),
                  pl.BlockSpec((1, tn), lambda j: (0, j))],
        out_specs=pl.BlockSpec((m, tn), lambda j: (0, j)),
        compiler_params=_params(("arbitrary",), vmem),
        name="ada",
    )(c_pad, w_ada, b_ada.reshape(1, n))


def _rms(x, g):
    return x * lax.rsqrt(jnp.mean(x * x, axis=-1, keepdims=True) + EPS) * g


def _route(h, router_ref, n_experts):
    rows = h.shape[0]
    lane = lax.broadcasted_iota(jnp.int32, (rows, LANE), 1)
    logits = jnp.full((rows, LANE), NEG_INF, F32)
    for e in range(n_experts):
        le = jnp.sum(h * router_ref[e:e + 1, :], axis=-1, keepdims=True)
        logits = jnp.where(lane == e, le, logits)
    m1 = jnp.max(logits, axis=-1, keepdims=True)
    i1 = jnp.min(jnp.where(logits == m1, lane, LANE), axis=-1, keepdims=True)
    rest = jnp.where(lane == i1, NEG_INF, logits)
    m2 = jnp.max(rest, axis=-1, keepdims=True)
    i2 = jnp.min(jnp.where(rest == m2, lane, LANE), axis=-1, keepdims=True)
    e2 = jnp.exp(m2 - m1)
    den = 1.0 + e2
    member = jnp.where(lane == i1, 1.0, jnp.where(lane == i2, 1.0, 0.0))
    extra = jnp.where(lane == n_experts, i1.astype(F32),
                      jnp.where(lane == n_experts + 1, i2.astype(F32),
                                jnp.where(lane == n_experts + 2, 1.0 / den, e2 / den)))
    return jnp.where(lane < n_experts, member, jnp.where(lane < n_experts + 4, extra, 0.0))


def _pack_bf16_pairs(h):
    half = h.shape[1] // 2
    lo = pltpu.bitcast(h[:, :half].astype(BF16).astype(F32), jnp.uint32)
    hi = pltpu.bitcast(h[:, half:].astype(BF16).astype(F32), jnp.uint32)
    return (hi & jnp.uint32(0xFFFF0000)) | (lo >> 16)


def _unpack_bf16_pairs(u):
    lo = pltpu.bitcast(u << 16, F32).astype(BF16)
    hi = pltpu.bitcast(u & jnp.uint32(0xFFFF0000), F32).astype(BF16)
    return lo, hi


def _pre_body(x_ref, mod_ref, g_ref, h_ref):
    h = _rms(x_ref[...], g_ref[...]) * (1.0 + mod_ref[1:2, :]) + mod_ref[0:1, :]
    h_ref[...] = h.astype(h_ref.dtype)


def _prenorm(x3, mod, g, tr=256):
    b, s, d = x3.shape
    tr = min(tr, s)
    vmem = 2 * tr * d * 4 + 2 * tr * d * 2 + 4 * tr * d * 4
    return pl.pallas_call(
        _pre_body,
        out_shape=jax.ShapeDtypeStruct((b, s, d), BF16),
        grid=(b, s // tr),
        in_specs=[pl.BlockSpec((None, tr, d), lambda bi, i: (bi, i, 0)),
                  pl.BlockSpec((None, N_MOD, d), lambda bi, i: (bi, 0, 0)),
                  pl.BlockSpec((1, d), lambda bi, i: (0, 0))],
        out_specs=pl.BlockSpec((None, tr, d), lambda bi, i: (bi, i, 0)),
        compiler_params=_params(("arbitrary", "arbitrary"), vmem),
        name="prenorm",
    )(x3, mod, g.reshape(1, d))


def _post_body(gate_row, nxt, n_experts, *refs):
    x_ref, y_ref, mod_ref, gpost_ref = refs[:4]
    pos = 4
    if nxt is not None:
        modn_ref, gpre_ref = refs[pos:pos + 2]
        pos += 2
    if n_experts:
        router_ref = refs[pos]
        pos += 1
    xo_ref = refs[pos]
    pos += 1
    xn = x_ref[...] + mod_ref[gate_row:gate_row + 1, :] * _rms(y_ref[...], gpost_ref[...])
    xo_ref[...] = xn
    if nxt is not None:
        shift_row, scale_row = nxt
        h = _rms(xn, gpre_ref[...]) * (1.0 + modn_ref[scale_row:scale_row + 1, :]) \
            + modn_ref[shift_row:shift_row + 1, :]
        if not n_experts:
            refs[pos][...] = h.astype(BF16)
            return
        hp_ref, route_ref, rank_ref, counts_ref, carry_ref = refs[pos:pos + 5]
        hp_ref[...] = _pack_bf16_pairs(h)
        route = _route(h, router_ref, n_experts)
        route_ref[...] = route

        @pl.when((pl.program_id(0) == 0) & (pl.program_id(1) == 0))
        def _():
            carry_ref[...] = jnp.zeros_like(carry_ref)

        rows = route.shape[0]
        lane = lax.broadcasted_iota(jnp.int32, route.shape, 1)
        member = jnp.where(lane < n_experts, route, 0.0)
        ri = lax.broadcasted_iota(jnp.int32, (rows, rows), 0)
        ci = lax.broadcasted_iota(jnp.int32, (rows, rows), 1)
        earlier = jnp.where(ci < ri, 1.0, 0.0).astype(BF16)
        within = jnp.dot(earlier, member.astype(BF16), preferred_element_type=F32)
        rank_ref[...] = within + carry_ref[0:1, :]
        carry_ref[...] = carry_ref[...] + jnp.sum(member, axis=0, keepdims=True)
        counts_ref[...] = carry_ref[...]


def _postnorm(x3, y3, mod, g_post, gate_row, *, nxt=None, mod_next=None, g_pre=None,
              router_t=None, tr=256):
    b, s, d = x3.shape
    tr = min(tr, s)
    n_experts = 0 if router_t is None else router_t.shape[0]
    row = lambda bi, i: (bi, i, 0)
    mod_spec = pl.BlockSpec((None, N_MOD, d), lambda bi, i: (bi, 0, 0))
    in_specs = [pl.BlockSpec((None, tr, d), row), pl.BlockSpec((None, tr, d), row), mod_spec,
                pl.BlockSpec((1, d), lambda bi, i: (0, 0))]
    args = [x3, y3, mod, g_post.reshape(1, d)]
    out_shape = [jax.ShapeDtypeStruct((b, s, d), F32)]
    out_specs = [pl.BlockSpec((None, tr, d), row)]
    scratch = []
    if nxt is not None:
        in_specs += [mod_spec, pl.BlockSpec((1, d), lambda bi, i: (0, 0))]
        args += [mod_next, g_pre.reshape(1, d)]
    if nxt is not None and not n_experts:
        out_shape.append(jax.ShapeDtypeStruct((b, s, d), BF16))
        out_specs.append(pl.BlockSpec((None, tr, d), row))
    if n_experts:
        assert nxt is not None and n_experts + 4 <= LANE
        in_specs.append(pl.BlockSpec((n_experts, d), lambda bi, i: (0, 0)))
        args.append(router_t)
        out_shape += [jax.ShapeDtypeStruct((b, s, d // 2), jnp.uint32),
                      jax.ShapeDtypeStruct((b, s, LANE), F32),
                      jax.ShapeDtypeStruct((b, s, LANE), F32),
                      jax.ShapeDtypeStruct((8, LANE), F32)]
        out_specs += [pl.BlockSpec((None, tr, d // 2), row), pl.BlockSpec((None, tr, LANE), row),
                      pl.BlockSpec((None, tr, LANE), row),
                      pl.BlockSpec((8, LANE), lambda bi, i: (0, 0))]
        scratch.append(pltpu.VMEM((8, LANE), F32))
    vmem = 2 * 3 * tr * d * 4 + 2 * tr * d * 2 + 6 * tr * d * 4
    return pl.pallas_call(
        functools.partial(_post_body, gate_row, nxt, n_experts),
        out_shape=out_shape,
        grid=(b, s // tr),
        in_specs=in_specs,
        out_specs=out_specs,
        scratch_shapes=scratch,
        compiler_params=_params(("arbitrary", "arbitrary"), vmem),
        name="postnorm",
    )(*args)


def _lat_body(alat_ref, pik_ref, g_ref, lat_ref, latt_ref, ikt_ref):
    lat = _rms(alat_ref[...], g_ref[...])
    lat_ref[...] = lat.astype(BF16)
    latt_ref[...] = lat.T.astype(BF16)
    ikt_ref[...] = pik_ref[...].T.astype(BF16)


def _latnorm(pa3, pik3, g_lat, lat_col, n_lat, tr=512):
    b, s, _ = pa3.shape
    tr = min(tr, s)
    assert lat_col % n_lat == 0
    vmem = 2 * tr * (n_lat + LANE) * 4 + 4 * tr * (n_lat + LANE) * 2 + 4 * tr * n_lat * 4
    return pl.pallas_call(
        _lat_body,
        out_shape=[jax.ShapeDtypeStruct((b, s, n_lat), BF16),
                   jax.ShapeDtypeStruct((b, n_lat, s), BF16),
                   jax.ShapeDtypeStruct((b, LANE, s), BF16)],
        grid=(b, s // tr),
        in_specs=[pl.BlockSpec((None, tr, n_lat), lambda bi, i: (bi, i, lat_col // n_lat)),
                  pl.BlockSpec((None, tr, LANE), lambda bi, i: (bi, i, 0)),
                  pl.BlockSpec((1, n_lat), lambda bi, i: (0, 0))],
        out_specs=[pl.BlockSpec((None, tr, n_lat), lambda bi, i: (bi, i, 0)),
                   pl.BlockSpec((None, n_lat, tr), lambda bi, i: (bi, 0, i)),
                   pl.BlockSpec((None, LANE, tr), lambda bi, i: (bi, 0, i))],
        compiler_params=_params(("arbitrary", "arbitrary"), vmem),
        name="latnorm",
    )(pa3, pik3, g_lat.reshape(1, n_lat))


def _dsa_body(topk, n_variants, *refs):
    i = pl.program_id(1)
    n_blk = refs[4].shape[0] // A_BLOCK
    per = -(-n_blk // n_variants)
    for v in range(n_variants):
        blocks = min(n_blk, (v + 1) * per)

        @pl.when((i >= v * per) & (i < (v + 1) * per))
        def _(blocks=blocks):
            _dsa_compute(topk, blocks * A_BLOCK, i, *refs)


def _dsa_compute(topk, seq, i, q_ref, iq_ref, iw_ref, ikt_ref, lat_ref, latt_ref, wuk_ref, wuv_ref,
                 bias_ref, o_ref, madd_ref):
    n_blk = seq // A_BLOCK
    n_heads = wuk_ref.shape[0]
    int_min = jnp.int32(-2 ** 31)
    qpos = i * A_BLOCK + lax.broadcasted_iota(jnp.int32, (A_BLOCK, 1), 0)
    kpos = lax.broadcasted_iota(jnp.int32, (A_BLOCK, seq), 1)

    iq = iq_ref[...].astype(BF16)
    iw = iw_ref[...]
    ikt = ikt_ref[0:IDX_DIM, 0:seq]
    score = jnp.zeros((A_BLOCK, seq), F32)
    for h in range(IDX_HEADS):
        l = jnp.dot(iq[:, h * IDX_DIM:(h + 1) * IDX_DIM], ikt, preferred_element_type=F32)
        score = score + iw[:, IDX_DIM + h:IDX_DIM + h + 1] * jnp.maximum(l, 0.0)

    score = jnp.where(score == 0.0, 0.0, score)
    key = pltpu.bitcast(score, jnp.int32)
    key = jnp.where(key < 0, key ^ jnp.int32(0x7FFFFFFF), key)
    key = jnp.where(kpos <= qpos, key, int_min)
    k_eff = jnp.minimum(qpos + 1, topk).astype(F32)

    def count(mask):
        return jnp.sum(jnp.where(mask, 1.0, 0.0), axis=1, keepdims=True)

    t0 = jnp.where(count(key >= 0) >= k_eff, jnp.int32(0), int_min)

    def value_step(it, t):
        c = t | (jnp.int32(1) << (30 - it))
        return jnp.where(count(key >= c) >= k_eff, c, t)

    t = lax.fori_loop(0, 31, value_step, t0)

    need = k_eff - count(key > t)
    eq = key == t
    idx_bits = max(1, (seq - 1).bit_length())

    def index_step(it, p):
        c = p | (jnp.int32(1) << (idx_bits - 1 - it))
        below = jnp.sum(jnp.where(eq, jnp.where(kpos < c, 1.0, 0.0), 0.0), axis=1, keepdims=True)
        return jnp.where(below < need, c, p)

    p = lax.fori_loop(0, idx_bits, index_step, jnp.zeros((A_BLOCK, 1), jnp.int32))
    tie_add = jnp.where(eq, jnp.where(kpos <= p, 0.0, NEG_INF), NEG_INF)
    madd_ref[:, 0:seq] = jnp.where(key > t, 0.0, tie_add)

    scale = HEAD_DIM ** -0.5
    for h in range(n_heads):
        cols = slice(h * HEAD_DIM, (h + 1) * HEAD_DIM)
        qa = jnp.dot(q_ref[:, cols].astype(BF16), wuk_ref[h],
                     preferred_element_type=F32).astype(BF16)
        lg = jnp.dot(qa, latt_ref[:, 0:seq], preferred_element_type=F32) * scale
        bias = jnp.concatenate([bias_ref[jnp.maximum(i - j, 0), h] for j in range(n_blk)], axis=1)
        lg = lg + bias + madd_ref[:, 0:seq]
        m = jnp.max(lg, axis=1, keepdims=True)
        e = jnp.exp(lg - m)
        den = jnp.sum(e, axis=1, keepdims=True)
        o_lat = jnp.dot(e.astype(BF16), lat_ref[0:seq, :], preferred_element_type=F32) / den
        o_ref[:, cols] = jnp.dot(o_lat.astype(BF16), wuv_ref[h],
                                 preferred_element_type=F32).astype(o_ref.dtype)


def _dsa(pa3, pik3, lat, latt, ikt, wuk_t, wuv_t, bias_a, *, q_width, iq_col, iq_width,
         n_variants):
    b, s, _ = pa3.shape
    n_lat = lat.shape[-1]
    n_heads = wuk_t.shape[0]
    n_blk = s // A_BLOCK
    topk = min(IDX_TOPK, s // 4)
    assert iq_col % iq_width == 0
    vmem = 2 * (A_BLOCK * (q_width + iq_width + LANE) * 4 + LANE * s * 2 + 2 * s * n_lat * 2
                + 2 * n_heads * HEAD_DIM * n_lat * 2 + n_heads * n_blk * A_BLOCK * A_BLOCK * 4
                + A_BLOCK * q_width * 2) + 10 * A_BLOCK * s * 4
    return pl.pallas_call(
        functools.partial(_dsa_body, topk, n_variants),
        out_shape=jax.ShapeDtypeStruct((b, s, q_width), BF16),
        grid=(b, n_blk),
        in_specs=[pl.BlockSpec((None, A_BLOCK, q_width), lambda bi, i: (bi, i, 0)),
                  pl.BlockSpec((None, A_BLOCK, iq_width), lambda bi, i: (bi, i, iq_col // iq_width)),
                  pl.BlockSpec((None, A_BLOCK, LANE), lambda bi, i: (bi, i, 0)),
                  pl.BlockSpec((None, LANE, s), lambda bi, i: (bi, 0, 0)),
                  pl.BlockSpec((None, s, n_lat), lambda bi, i: (bi, 0, 0)),
                  pl.BlockSpec((None, n_lat, s), lambda bi, i: (bi, 0, 0)),
                  pl.BlockSpec(wuk_t.shape, lambda bi, i: (0, 0, 0)),
                  pl.BlockSpec(wuv_t.shape, lambda bi, i: (0, 0, 0)),
                  pl.BlockSpec(bias_a.shape, lambda bi, i: (0, 0, 0, 0))],
        out_specs=pl.BlockSpec((None, A_BLOCK, q_width), lambda bi, i: (bi, i, 0)),
        scratch_shapes=[pltpu.VMEM((A_BLOCK, s), F32)],
        compiler_params=_params(("arbitrary", "arbitrary"), vmem),
        name="dsa",
    )(pa3, pa3, pik3, ikt, lat, latt, wuk_t, wuv_t, bias_a)


def _dil_body(span, dil, hb, has_prev, *refs):
    if has_prev:
        q_ref, kp_ref, kc_ref, vp_ref, vc_ref, bias_ref, o_ref, lse_ref = refs
    else:
        q_ref, kc_ref, vc_ref, bias_ref, o_ref, lse_ref = refs
    i = pl.program_id(1)
    hblk = pl.program_id(2)
    n_keys = 2 * B_BLOCK if has_prev else B_BLOCK
    key0 = 0 if has_prev else B_BLOCK
    qi = lax.broadcasted_iota(jnp.int32, (B_BLOCK, n_keys), 0)
    kj = lax.broadcasted_iota(jnp.int32, (B_BLOCK, n_keys), 1) + key0
    sub_dist = qi + B_BLOCK - kj
    in_band = jnp.where(sub_dist <= span,
                        jnp.where((i - 1) * B_BLOCK + kj >= 0, 0.0, NEG_INF), NEG_INF)
    madd = jnp.where(sub_dist >= 0, in_band, NEG_INF)
    lane = lax.broadcasted_iota(jnp.int32, (B_BLOCK, LANE), 1)
    scale = HEAD_DIM ** -0.5

    @pl.when(hblk == 0)
    def _():
        lse_ref[...] = jnp.zeros_like(lse_ref)

    for r in range(dil):
        rows = pl.ds(r, B_BLOCK, stride=dil) if dil > 1 else slice(None)
        lse_acc = lse_ref[rows, :]
        for h in range(hb):
            cols = slice(h * HEAD_DIM, (h + 1) * HEAD_DIM)
            q = q_ref[rows, cols].astype(BF16)
            if has_prev:
                k2 = jnp.concatenate([kp_ref[rows, cols], kc_ref[rows, cols]], axis=0).astype(BF16)
                v2 = jnp.concatenate([vp_ref[rows, cols], vc_ref[rows, cols]], axis=0).astype(BF16)
            else:
                k2 = kc_ref[rows, cols].astype(BF16)
                v2 = vc_ref[rows, cols].astype(BF16)
            lg = lax.dot_general(q, k2, (((1,), (1,)), ((), ())), preferred_element_type=F32)
            lg = lg * scale + bias_ref[h, :, key0:] + madd
            m = jnp.max(lg, axis=1, keepdims=True)
            e = jnp.exp(lg - m)
            den = jnp.sum(e, axis=1, keepdims=True)
            o_ref[rows, cols] = jnp.dot((e / den).astype(BF16), v2, preferred_element_type=F32)
            lse_acc = jnp.where(lane == hblk * hb + h, m + jnp.log(den), lse_acc)
        lse_ref[rows, :] = lse_acc


def _dilated(rest3, bias_b, g, n_groups, window, dil, width):
    b, s, w_all = rest3.shape
    n_heads = width // HEAD_DIM
    span = window // dil
    n_sub = s // dil
    assert span == B_BLOCK and n_sub % B_BLOCK == 0 and w_all % width == 0
    nb = n_sub // B_BLOCK
    rows = B_BLOCK * dil
    hb = n_heads if dil == 1 else 1
    assert HEAD_DIM == LANE
    n_hblk = n_heads // hb
    has_prev = nb > 1
    cur = lambda slot: (lambda bi, i, hk: (bi, i, (slot * n_groups + g) * n_hblk + hk))
    prev = lambda slot: (lambda bi, i, hk: (bi, jnp.maximum(i - 1, 0),
                                            (slot * n_groups + g) * n_hblk + hk))
    blk = (None, rows, hb * HEAD_DIM)
    in_specs = [pl.BlockSpec(blk, cur(0))]
    for slot in (1, 2):
        if has_prev:
            in_specs.append(pl.BlockSpec(blk, prev(slot)))
        in_specs.append(pl.BlockSpec(blk, cur(slot)))
    in_specs.append(pl.BlockSpec((None, hb, B_BLOCK, 2 * B_BLOCK), lambda bi, i, hk: (g, hk, 0, 0)))
    vmem = 2 * (len(in_specs) * rows * hb * HEAD_DIM + rows * LANE) * 4 \
        + 2 * hb * B_BLOCK * 2 * B_BLOCK * 4 + (4 << 20)
    o, lse = pl.pallas_call(
        functools.partial(_dil_body, span, dil, hb, has_prev),
        out_shape=[jax.ShapeDtypeStruct((b, s, width), F32),
                   jax.ShapeDtypeStruct((b, s, LANE), F32)],
        grid=(b, nb, n_hblk),
        in_specs=in_specs,
        out_specs=[pl.BlockSpec(blk, lambda bi, i, hk: (bi, i, hk)),
                   pl.BlockSpec((None, rows, LANE), lambda bi, i, hk: (bi, i, 0))],
        compiler_params=_params(("arbitrary", "arbitrary", "arbitrary"), vmem),
        name=f"dilated{g}",
    )(*([rest3] * (len(in_specs) - 1)), bias_b)
    return o.reshape(b * s, width), lse.reshape(b * s, LANE)


def _mix_body(n_groups, n_heads, *refs):
    o_refs = refs[:n_groups]
    l_refs = refs[n_groups:2 * n_groups]
    out_ref = refs[2 * n_groups]
    ls = [r[...] for r in l_refs]
    m = functools.reduce(jnp.maximum, ls)
    es = [jnp.exp(l - m) for l in ls]
    tot = functools.reduce(jnp.add, es)
    ws = [e / tot for e in es]
    for h in range(n_heads):
        cols = slice(h * HEAD_DIM, (h + 1) * HEAD_DIM)
        acc = ws[0][:, h:h + 1] * o_refs[0][:, cols]
        for gi in range(1, n_groups):
            acc = acc + ws[gi][:, h:h + 1] * o_refs[gi][:, cols]
        out_ref[:, cols] = acc.astype(out_ref.dtype)


def _mixture(outs, lses, tr=512):
    t, width = outs[0].shape
    tr = min(tr, t)
    n_groups = len(outs)
    vmem = 2 * n_groups * tr * (width + LANE) * 4 + 2 * tr * width * 2 + 4 * tr * width * 4
    return pl.pallas_call(
        functools.partial(_mix_body, n_groups, width // HEAD_DIM),
        out_shape=jax.ShapeDtypeStruct((t, width), BF16),
        grid=(t // tr,),
        in_specs=[pl.BlockSpec((tr, width), lambda i: (i, 0))] * n_groups
        + [pl.BlockSpec((tr, LANE), lambda i: (i, 0))] * n_groups,
        out_specs=pl.BlockSpec((tr, width), lambda i: (i, 0)),
        compiler_params=_params(("arbitrary",), vmem),
        name="mixture",
    )(*outs, *lses)


def _gmlp_body(u_ref, v_ref, g_ref, b_ref, ws_ref, bs_ref, o_ref):
    u = jax.nn.gelu(u_ref[...])
    v = jax.nn.gelu(v_ref[...])
    mu = jnp.mean(v, axis=-1, keepdims=True)
    var = jnp.mean(jnp.square(v - mu), axis=-1, keepdims=True)
    v = (v - mu) * lax.rsqrt(var + EPS) * g_ref[...] + b_ref[...]
    ti = lax.broadcasted_iota(jnp.int32, (C_CHUNK, C_CHUNK), 0)
    si = lax.broadcasted_iota(jnp.int32, (C_CHUNK, C_CHUNK), 1)
    causal = si <= ti
    gw = u.shape[1] // C_GROUPS
    for g in range(C_GROUPS):
        cols = slice(g * gw, (g + 1) * gw)
        w = jnp.where(causal, ws_ref[g], 0.0).astype(BF16)
        mixed = jnp.dot(w, v[:, cols].astype(BF16), preferred_element_type=F32) + bs_ref[:, g:g + 1]
        o_ref[:, cols] = (u[:, cols] * mixed).astype(o_ref.dtype)


def _gmlp(rest, u_col, width, ln_g, ln_b, w_s, b_s_t):
    t = rest.shape[0]
    assert u_col % width == 0 and (width // C_GROUPS) % LANE == 0
    vmem = 2 * 2 * C_CHUNK * width * 4 + 2 * C_CHUNK * width * 2 + 2 * C_GROUPS * C_CHUNK * C_CHUNK * 4 \
        + 8 * C_CHUNK * width * 4
    return pl.pallas_call(
        _gmlp_body,
        out_shape=jax.ShapeDtypeStruct((t, width), BF16),
        grid=(t // C_CHUNK,),
        in_specs=[pl.BlockSpec((C_CHUNK, width), lambda i: (i, u_col // width)),
                  pl.BlockSpec((C_CHUNK, width), lambda i: (i, u_col // width + 1)),
                  pl.BlockSpec((1, width), lambda i: (0, 0)),
                  pl.BlockSpec((1, width), lambda i: (0, 0)),
                  pl.BlockSpec(w_s.shape, lambda i: (0, 0, 0)),
                  pl.BlockSpec(b_s_t.shape, lambda i: (0, 0))],
        out_specs=pl.BlockSpec((C_CHUNK, width), lambda i: (i, 0)),
        compiler_params=_params(("arbitrary",), vmem),
        name="gmlp",
    )(rest, rest, ln_g.reshape(1, width), ln_b.reshape(1, width), w_s, b_s_t)


def _conv_body(h_ref, bg_ref, cg_ref, w_ref, o_ref, carry_ref):
    @pl.when(pl.program_id(1) == 0)
    def _():
        carry_ref[...] = jnp.zeros_like(carry_ref)

    ch = cg_ref[...] * h_ref[...]
    rows = ch.shape[0]
    row = lax.broadcasted_iota(jnp.int32, ch.shape, 0)
    pm1 = carry_ref[7:8, :]
    pm2 = carry_ref[6:7, :]
    s1 = jnp.where(row == 0, pm1, pltpu.roll(ch, 1, 0))
    s2 = jnp.where(row == 0, pm2, jnp.where(row == 1, pm1, pltpu.roll(ch, 2, 0)))
    z = w_ref[0:1, :] * s2 + w_ref[1:2, :] * s1 + w_ref[2:3, :] * ch
    o_ref[...] = (bg_ref[...] * z).astype(o_ref.dtype)
    carry_ref[...] = ch[rows - 8:, :]


def _short_conv(rest3, h_col, width, conv_w, tr=256):
    b, s, _ = rest3.shape
    tr = min(tr, s)
    assert h_col % width == 0 and conv_w.shape[0] == D_CONV_WIDTH
    cb = h_col // width
    vmem = 2 * 3 * tr * width * 4 + 2 * tr * width * 2 + 6 * tr * width * 4
    return pl.pallas_call(
        _conv_body,
        out_shape=jax.ShapeDtypeStruct((b, s, width), BF16),
        grid=(b, s // tr),
        in_specs=[pl.BlockSpec((None, tr, width), lambda bi, i: (bi, i, cb)),
                  pl.BlockSpec((None, tr, width), lambda bi, i: (bi, i, cb + 1)),
                  pl.BlockSpec((None, tr, width), lambda bi, i: (bi, i, cb + 2)),
                  pl.BlockSpec(conv_w.shape, lambda bi, i: (0, 0))],
        out_specs=pl.BlockSpec((None, tr, width), lambda bi, i: (bi, i, 0)),
        scratch_shapes=[pltpu.VMEM((8, width), F32)],
        compiler_params=_params(("arbitrary", "arbitrary"), vmem),
        name="short_conv",
    )(rest3, rest3, rest3, conv_w)


def _merge_body(chunk, *refs):
    br_refs = refs[:N_BRANCH]
    gate_refs = refs[N_BRANCH:2 * N_BRANCH]
    w_ref, o_ref, wb_ref = refs[2 * N_BRANCH:]

    @pl.when(pl.program_id(1) == 0)
    def _():
        for n in range(N_BRANCH):
            _cast_rows(lambda rows, n=n: w_ref[n, rows, :], wb_ref.at[n], w_ref.shape[1], chunk)

    acc = None
    for n in range(N_BRANCH):
        proj = jnp.dot(br_refs[n][...], wb_ref[n], preferred_element_type=F32)
        term = jax.nn.sigmoid(gate_refs[n][...].astype(F32)) * proj
        acc = term if acc is None else acc + term
    o_ref[...] = acc.astype(o_ref.dtype)


def _merge(branches, gates, w_branch, layer, tn=512, tm=1024):
    t, width = branches[0].shape
    d = w_branch.shape[-1]
    tn = min(tn, d)
    tm = min(tm, t)
    assert d % tn == 0 and t % tm == 0 and gates.shape == (t, N_BRANCH * d)
    chunk = _pick_tile(width, (256, 128, 8))
    gate_spec = lambda n: pl.BlockSpec((tm, tn), lambda j, i: (i, (n * d) // tn + j))
    vmem = 2 * N_BRANCH * (tm * width * 2 + tm * tn * 2 + width * tn * 4) + N_BRANCH * width * tn * 2 \
        + 2 * tm * tn * 2 + 4 * tm * tn * 4
    return pl.pallas_call(
        functools.partial(_merge_body, chunk),
        out_shape=jax.ShapeDtypeStruct((t, d), BF16),
        grid=(d // tn, t // tm),
        in_specs=[pl.BlockSpec((tm, width), lambda j, i: (i, 0))] * N_BRANCH
        + [gate_spec(n) for n in range(N_BRANCH)]
        + [pl.BlockSpec((None, N_BRANCH, width, tn), lambda j, i: (layer, 0, 0, j))],
        out_specs=pl.BlockSpec((tm, tn), lambda j, i: (i, j)),
        scratch_shapes=[pltpu.VMEM((N_BRANCH, width, tn), BF16)],
        compiler_params=_params(("arbitrary", "arbitrary"), vmem),
        name="merge",
    )(*branches, *([gates] * N_BRANCH), w_branch)


def _up_body(chunk, h_ref, wg_ref, wu_ref, o_ref, wgb_ref, wub_ref):
    @pl.when(pl.program_id(1) == 0)
    def _():
        _cast_rows(lambda rows: wg_ref[rows, :], wgb_ref, wg_ref.shape[0], chunk)
        _cast_rows(lambda rows: wu_ref[rows, :], wub_ref, wu_ref.shape[0], chunk)

    h = h_ref[...]
    g = jnp.dot(h, wgb_ref[...], preferred_element_type=F32)
    u = jnp.dot(h, wub_ref[...], preferred_element_type=F32)
    o_ref[...] = (g * jax.nn.sigmoid(g) * u).astype(o_ref.dtype)


def _swiglu_up(h, wg, wu, tm=1024):
    t, k = h.shape
    n_e, _, f = wg.shape
    tn = _pick_tile(f, (256, 128))
    tm = min(tm, t)
    nj = f // tn
    chunk = _pick_tile(k, (256, 128, 8))
    w_spec = pl.BlockSpec((None, k, tn), lambda j, i: (j // nj, 0, j % nj))
    vmem = 2 * tm * k * 2 + 4 * k * tn * 4 + 2 * k * tn * 2 + 2 * tm * tn * 2 + 4 * tm * tn * 4
    return pl.pallas_call(
        functools.partial(_up_body, chunk),
        out_shape=jax.ShapeDtypeStruct((t, n_e * f), BF16),
        grid=(n_e * nj, t // tm),
        in_specs=[pl.BlockSpec((tm, k), lambda j, i: (i, 0)), w_spec, w_spec],
        out_specs=pl.BlockSpec((tm, tn), lambda j, i: (i, j)),
        scratch_shapes=[pltpu.VMEM((k, tn), BF16), pltpu.VMEM((k, tn), BF16)],
        compiler_params=_params(("arbitrary", "arbitrary"), vmem),
        name="swiglu_up",
    )(h, wg, wu)


def _round_body(w_ref, o_ref):
    o_ref[...] = w_ref[...].astype(o_ref.dtype)


def _round_bf16(w, layer, tr=512):
    _, k, n = w.shape
    tr = _pick_tile(k, (tr, 256, 128, 8))
    return pl.pallas_call(
        _round_body,
        out_shape=jax.ShapeDtypeStruct((k, n), BF16),
        grid=(k // tr,),
        in_specs=[pl.BlockSpec((None, tr, n), lambda i: (layer, i, 0))],
        out_specs=pl.BlockSpec((tr, n), lambda i: (i, 0)),
        compiler_params=_params(("arbitrary",), 2 * tr * n * 6),
        name="round_bf16",
    )(w)


def _down_body(a_ref, w_ref, o_ref):
    o_ref[...] = jnp.dot(a_ref[...], w_ref[...], preferred_element_type=F32)


def _swiglu_down(a, wd, layer, tn=512, tm=512):
    t, k = a.shape
    n = wd.shape[2]
    tn = min(tn, n)
    tm = min(tm, t)
    assert n % tn == 0 and t % tm == 0
    wb = _round_bf16(wd, layer)
    vmem = 2 * tm * k * 2 + 2 * k * tn * 2 + 2 * tm * tn * 4 + tm * tn * 4
    return pl.pallas_call(
        _down_body,
        out_shape=jax.ShapeDtypeStruct((t, n), F32),
        grid=(n // tn, t // tm),
        in_specs=[pl.BlockSpec((tm, k), lambda j, i: (i, 0)),
                  pl.BlockSpec((k, tn), lambda j, i: (0, j))],
        out_specs=pl.BlockSpec((tm, tn), lambda j, i: (i, j)),
        compiler_params=_params(("arbitrary", "arbitrary"), vmem),
        name="swiglu_down",
    )(a, wb)


MOE_TILE = 256


def _dispatch_body(n_tok, pos1_ref, pos2_ref, h_ref, o_ref, src_ref, buf_ref, sem):
    r = pl.program_id(0)

    def row_copy(tile, slot, k):
        return pltpu.make_async_copy(h_ref.at[pl.ds(src_ref[tile * MOE_TILE + k], 1), :],
                                     buf_ref.at[slot, pl.ds(k, 1), :], sem.at[slot])

    def start_tile(tile, slot):
        lax.fori_loop(0, MOE_TILE, lambda k, c: (row_copy(tile, slot, k).start(), c)[1], 0,
                      unroll=8)

    @pl.when(r == 0)
    def _():
        def clear(p, c):
            src_ref[p] = 0
            return c

        def invert(t, c):
            src_ref[pos1_ref[t]] = t
            src_ref[pos2_ref[t]] = t
            return c

        lax.fori_loop(0, src_ref.shape[0], clear, 0)
        lax.fori_loop(0, n_tok, invert, 0)
        start_tile(0, 0)

    @pl.when(r + 1 < pl.num_programs(0))
    def _():
        start_tile(r + 1, (r + 1) % 2)

    slot = r % 2
    pltpu.make_async_copy(h_ref.at[pl.ds(0, MOE_TILE), :], buf_ref.at[slot], sem.at[slot]).wait()
    o_ref[...] = buf_ref[slot]


def _moe_dispatch(hp, pos1, pos2, n_rows):
    t, w = hp.shape
    assert n_rows % MOE_TILE == 0
    return pl.pallas_call(
        functools.partial(_dispatch_body, t),
        out_shape=jax.ShapeDtypeStruct((n_rows, w), hp.dtype),
        grid_spec=pltpu.PrefetchScalarGridSpec(
            num_scalar_prefetch=2,
            grid=(n_rows // MOE_TILE,),
            in_specs=[pl.BlockSpec(memory_space=pl.ANY)],
            out_specs=pl.BlockSpec((MOE_TILE, w), lambda r, p1, p2: (r, 0)),
            scratch_shapes=[pltpu.SMEM((n_rows,), jnp.int32),
                            pltpu.VMEM((2, MOE_TILE, w), hp.dtype),
                            pltpu.SemaphoreType.DMA((2,))]),
        compiler_params=_params(("arbitrary",), 6 * MOE_TILE * w * 4),
        name="moe_dispatch",
    )(pos1, pos2, hp)


def _group_body(swiglu, chunk, te_ref, nused_ref, *refs):
    if swiglu:
        x_ref, wg_ref, wu_ref, o_ref, wgb_ref, wub_ref = refs
        pairs = ((wg_ref, wgb_ref), (wu_ref, wub_ref))
    else:
        x_ref, w_ref, o_ref, wb_ref = refs
        pairs = ((w_ref, wb_ref),)
    r = pl.program_id(1)
    expert = te_ref[r]
    prev = te_ref[jnp.maximum(r - 1, 0)]

    @pl.when(jnp.logical_or(r == 0, expert != prev))
    def _():
        for src_ref, dst_ref in pairs:
            _cast_rows(lambda rows, s=src_ref: s[rows, :], dst_ref, src_ref.shape[0], chunk)

    @pl.when(r < nused_ref[0])
    def _():
        if swiglu:
            lo, hi = _unpack_bf16_pairs(x_ref[...])
            half = lo.shape[1]

            def proj(wb):
                return jnp.dot(lo, wb[0:half, :], preferred_element_type=F32) \
                    + jnp.dot(hi, wb[half:, :], preferred_element_type=F32)

            g = proj(wgb_ref)
            o_ref[...] = (g * jax.nn.sigmoid(g) * proj(wub_ref)).astype(o_ref.dtype)
        else:
            o_ref[...] = jnp.dot(x_ref[...], wb_ref[...],
                                 preferred_element_type=F32).astype(o_ref.dtype)

    @pl.when(r >= nused_ref[0])
    def _():
        o_ref[...] = jnp.zeros_like(o_ref)


def _grouped_matmul(xs, weights, layer, tile_expert, n_used, out_dtype, tn, name):
    p, kx = xs.shape
    swiglu = len(weights) == 2
    _, _, k, n = weights[0].shape
    assert k == (2 * kx if swiglu else kx) and n % tn == 0 and p % MOE_TILE == 0
    chunk = _pick_tile(k, (256, 128, 8))
    w_spec = pl.BlockSpec((None, None, k, tn), lambda j, r, te, nu: (layer, te[r], 0, j))
    osz = jnp.dtype(out_dtype).itemsize
    vmem = 2 * MOE_TILE * kx * xs.dtype.itemsize + len(weights) * (2 * k * tn * 4 + k * tn * 2) \
        + 2 * MOE_TILE * tn * osz + 6 * MOE_TILE * tn * 4 + 2 * MOE_TILE * k * 2
    return pl.pallas_call(
        functools.partial(_group_body, swiglu, chunk),
        out_shape=jax.ShapeDtypeStruct((p, n), out_dtype),
        grid_spec=pltpu.PrefetchScalarGridSpec(
            num_scalar_prefetch=2,
            grid=(n // tn, p // MOE_TILE),
            in_specs=[pl.BlockSpec((MOE_TILE, kx), lambda j, r, te, nu: (r, 0))]
            + [w_spec] * len(weights),
            out_specs=pl.BlockSpec((MOE_TILE, tn), lambda j, r, te, nu: (r, j)),
            scratch_shapes=[pltpu.VMEM((k, tn), BF16)] * len(weights)),
        compiler_params=_params(("arbitrary", "arbitrary"), vmem),
        name=name,
    )(tile_expert, n_used, xs, *weights)


def _combine_body(tc, n_experts, pos1_ref, pos2_ref, route_ref, ye_ref, o_ref, buf_ref, sem):
    i = pl.program_id(0)

    def row_copies(tile, slot, k):
        t = tile * tc + k
        return (pltpu.make_async_copy(ye_ref.at[pl.ds(pos1_ref[t], 1), :],
                                      buf_ref.at[slot, 0, pl.ds(k, 1), :], sem.at[slot]),
                pltpu.make_async_copy(ye_ref.at[pl.ds(pos2_ref[t], 1), :],
                                      buf_ref.at[slot, 1, pl.ds(k, 1), :], sem.at[slot]))

    def start_tile(tile, slot):
        def body(k, c):
            for cp in row_copies(tile, slot, k):
                cp.start()
            return c
        lax.fori_loop(0, tc, body, 0, unroll=8)

    def wait_tile(slot):
        for half in range(2):
            pltpu.make_async_copy(ye_ref.at[pl.ds(0, tc), :], buf_ref.at[slot, half],
                                  sem.at[slot]).wait()

    @pl.when(i == 0)
    def _():
        start_tile(0, 0)

    @pl.when(i + 1 < pl.num_programs(0))
    def _():
        start_tile(i + 1, (i + 1) % 2)

    slot = i % 2
    wait_tile(slot)
    g1 = route_ref[:, n_experts + 2:n_experts + 3]
    g2 = route_ref[:, n_experts + 3:n_experts + 4]
    o_ref[...] = g1 * buf_ref[slot, 0] + g2 * buf_ref[slot, 1]


def _moe_combine(ye, route, pos1, pos2, n_experts, tc=128):
    t = route.shape[0]
    d = ye.shape[1]
    tc = min(tc, t)
    assert t % tc == 0
    return pl.pallas_call(
        functools.partial(_combine_body, tc, n_experts),
        out_shape=jax.ShapeDtypeStruct((t, d), F32),
        grid_spec=pltpu.PrefetchScalarGridSpec(
            num_scalar_prefetch=2,
            grid=(t // tc,),
            in_specs=[pl.BlockSpec((tc, LANE), lambda i, p1, p2: (i, 0)),
                      pl.BlockSpec(memory_space=pl.ANY)],
            out_specs=pl.BlockSpec((tc, d), lambda i, p1, p2: (i, 0)),
            scratch_shapes=[pltpu.VMEM((2, 2, tc, d), F32), pltpu.SemaphoreType.DMA((2,))]),
        compiler_params=_params(("arbitrary",), 4 * tc * d * 4 + 4 * tc * d * 4),
        name="moe_combine",
    )(pos1, pos2, route, ye)


def _moe(hp, route, rank, counts, layer_idx, w_gate, w_up, w_down):
    t = hp.shape[0]
    n_e = w_gate.shape[1]
    n_rows = TOP_K * t + n_e * MOE_TILE
    n_tiles = n_rows // MOE_TILE
    i1 = route[:, n_e].astype(jnp.int32)
    i2 = route[:, n_e + 1].astype(jnp.int32)
    cnt = counts[0, :n_e].astype(jnp.int32)
    tiles = (cnt + MOE_TILE - 1) // MOE_TILE
    tile_end = jnp.cumsum(tiles)
    row0 = (tile_end - tiles) * MOE_TILE
    slot = row0[None, :] + rank[:, :n_e].astype(jnp.int32)
    experts = jnp.arange(n_e, dtype=jnp.int32)[None, :]
    pos1 = jnp.sum(jnp.where(experts == i1[:, None], slot, 0), axis=1).astype(jnp.int32)
    pos2 = jnp.sum(jnp.where(experts == i2[:, None], slot, 0), axis=1).astype(jnp.int32)
    n_used = tile_end[-1:].astype(jnp.int32)
    tile_ids = jnp.minimum(jnp.arange(n_tiles, dtype=jnp.int32), n_used[0] - 1)
    tile_expert = jnp.sum(tile_ids[:, None] >= tile_end[None, :], axis=1).astype(jnp.int32)

    xs = _moe_dispatch(hp, pos1, pos2, n_rows)
    f = w_gate.shape[-1]
    a = _grouped_matmul(xs, (w_gate, w_up), layer_idx, tile_expert, n_used, BF16,
                        _pick_tile(f, (512, 256, 128)), "moe_up")
    ye = _grouped_matmul(a, (w_down,), layer_idx, tile_expert, n_used, F32,
                         _pick_tile(w_down.shape[-1], (1024, 512, 256, 128)), "moe_down")
    return _moe_combine(ye, route, pos1, pos2, n_e)


def _rel_bucket(dist):
    max_exact = REL_BUCKETS // 2
    d = jnp.maximum(dist, 0)
    df = jnp.maximum(d, max_exact).astype(F32)
    large = max_exact + (jnp.log(df / max_exact) / math.log(REL_MAX_DIST / max_exact)
                         * (REL_BUCKETS - max_exact)).astype(jnp.int32)
    large = jnp.minimum(large, REL_BUCKETS - 1)
    return jnp.where(d < max_exact, d, large)


def _bias_body(n_heads, head0, head_stride, bucket_ref, tab_ref, o_ref):
    bucket = bucket_ref[...]
    base = head0 + head_stride * pl.program_id(0)
    for h in range(n_heads):
        acc = jnp.zeros(bucket.shape, F32)
        for b in range(REL_BUCKETS):
            acc = jnp.where(bucket == b, tab_ref[b, base + h], acc)
        o_ref[h] = acc


def _bias_tiles(rel_bias, buckets, n_heads, head0, head_stride):
    n_tiles, q, k = buckets.shape
    return pl.pallas_call(
        functools.partial(_bias_body, n_heads, head0, head_stride),
        out_shape=jax.ShapeDtypeStruct((n_tiles, n_heads, q, k), F32),
        grid=(n_tiles,),
        in_specs=[pl.BlockSpec((None, q, k), lambda t: (t, 0, 0)),
                  pl.BlockSpec(memory_space=pltpu.SMEM)],
        out_specs=pl.BlockSpec((None, n_heads, q, k), lambda t: (t, 0, 0, 0)),
        compiler_params=_params(("arbitrary",), 4 * (n_heads + 1) * q * k * 4),
        name="bias_tiles",
    )(buckets, rel_bias)


def _bias_tiles_a(rel_bias, n_heads, seq):
    n_blk = seq // A_BLOCK
    qi = jnp.arange(A_BLOCK)[:, None]
    kj = jnp.arange(A_BLOCK)[None, :]
    dist = jnp.arange(n_blk)[:, None, None] * A_BLOCK + (qi - kj)[None]
    return _bias_tiles(rel_bias, _rel_bucket(dist), n_heads, 0, 0)


def _bias_tiles_b(rel_bias, head0, n_heads):
    qi = jnp.arange(B_BLOCK)[:, None]
    kj = jnp.arange(2 * B_BLOCK)[None, :]
    sub_dist = qi + B_BLOCK - kj
    buckets = jnp.stack([_rel_bucket(sub_dist * dil) for _, dil in B_GROUPS])
    return _bias_tiles(rel_bias, buckets, n_heads, head0, n_heads)


def _hybrid_mixer(h, bsz, seq, layer, w_in_t, lat_g, w_uk, w_uv, c_ln_g, c_ln_b, c_w_s, c_b_s,
                  d_conv, w_branch, w_out, bias_a, bias_b):
    t, d = h.shape
    width = d // N_BRANCH
    n_lat = w_uk.shape[0]
    a_heads = width // HEAD_DIM
    n_groups = len(B_GROUPS)
    iq_width = IDX_HEADS * IDX_DIM
    lat_col = width
    iq_col = lat_col + n_lat
    ik_col = iq_col + iq_width
    front = ik_col
    shift = IDX_DIM + IDX_HEADS
    qkv_w = 3 * n_groups * width
    rest_w = qkv_w + 2 * width + 3 * width + N_BRANCH * d
    assert front % 512 == 0 and w_in_t.shape[1] == front + shift + rest_w and shift <= LANE

    pa = _matmul_t(h, w_in_t, layer, row0=0, n_out=front, tm=1024, name="proj_front")
    pik = _matmul_t(h, w_in_t, layer, row0=front, n_out=LANE, name="proj_index_key")
    gate_col = qkv_w + 5 * width
    rest = _matmul_t(h, w_in_t, layer, row0=front + shift, n_out=gate_col, tm=1024,
                     name="proj_rest")
    gates = _matmul_t(h, w_in_t, layer, row0=front + shift + gate_col, n_out=N_BRANCH * d,
                      tm=1024, out_dtype=BF16, name="proj_gates")

    pa3 = pa.reshape(bsz, seq, front)
    pik3 = pik.reshape(bsz, seq, LANE)
    rest3 = rest.reshape(bsz, seq, gate_col)

    lat, latt, ikt = _latnorm(pa3, pik3, lat_g, lat_col, n_lat)
    wuk_t = jnp.transpose(w_uk, (1, 2, 0)).astype(BF16)
    wuv_t = jnp.transpose(w_uv, (1, 0, 2)).astype(BF16)
    o_a = _dsa(pa3, pik3, lat, latt, ikt, wuk_t, wuv_t, bias_a,
               q_width=width, iq_col=iq_col, iq_width=iq_width,
               n_variants=DSA_KEY_RANGE_VARIANTS).reshape(t, width)

    outs, lses = [], []
    for g, (window, dil) in enumerate(B_GROUPS):
        o, l = _dilated(rest3, bias_b, g, n_groups, window, dil, width)
        outs.append(o)
        lses.append(l)
    o_b = _mixture(outs, lses)

    o_c = _gmlp(rest, qkv_w, width, c_ln_g, c_ln_b, c_w_s, c_b_s.T)
    o_d = _short_conv(rest3, qkv_w + 2 * width, width, d_conv).reshape(t, width)

    mixed = _merge([o_a, o_b, o_c, o_d], gates, w_branch, layer)
    return _matmul(mixed, w_out, layer, tm=1024, name="proj_out")


def kernel(x, c, w_ada, b_ada, ada_table, rel_bias, norm_pre_mix, norm_post_mix, norm_pre_ffn,
           norm_post_ffn, w_in, a_lat_norm, a_w_uk, a_w_uv, c_ln_g, c_ln_b, c_w_s, c_b_s, d_conv,
           w_branch, w_out, ffn_w_gate, ffn_w_up, ffn_w_down, moe_router, moe_w_gate, moe_w_up,
           moe_w_down):
    bsz, seq, d = x.shape
    depth = w_in.shape[0]
    t = bsz * seq

    c_pad = jnp.pad(c, ((0, 16 - bsz % 16 if bsz % 16 else 0), (0, 0)))
    mod_shared = _ada(c_pad, w_ada, b_ada)[:bsz].reshape(bsz, N_MOD, d)
    mods = [mod_shared + ada_table[layer] for layer in range(depth)]

    w_in_t = jnp.transpose(w_in, (0, 2, 1))
    width = d // N_BRANCH
    bias_a = _bias_tiles_a(rel_bias, width // HEAD_DIM, seq)
    bias_b = _bias_tiles_b(rel_bias, width // HEAD_DIM, width // HEAD_DIM)

    h = _prenorm(x, mods[0], norm_pre_mix[0])
    for layer in range(depth):
        mod = mods[layer]
        y = _hybrid_mixer(h.reshape(t, d), bsz, seq, layer, w_in_t, a_lat_norm[layer],
                          a_w_uk[layer], a_w_uv[layer], c_ln_g[layer], c_ln_b[layer],
                          c_w_s[layer], c_b_s[layer], d_conv[layer], w_branch, w_out,
                          bias_a, bias_b)
        j = layer // 2
        dense = layer % 2 == 0
        res = _postnorm(x, y.reshape(bsz, seq, d), mod, norm_post_mix[layer], 2, nxt=(3, 4),
                        mod_next=mod, g_pre=norm_pre_ffn[layer],
                        router_t=None if dense else moe_router[j].T)
        if dense:
            x, h = res
            a = _swiglu_up(h.reshape(t, d), ffn_w_gate[j][None], ffn_w_up[j][None])
            y = _swiglu_down(a, ffn_w_down, j)
        else:
            x, hp, route, rank, counts = res
            y = _moe(hp.reshape(t, d // 2), route.reshape(t, LANE), rank.reshape(t, LANE), counts,
                     j, moe_w_gate, moe_w_up, moe_w_down)
        y3 = y.reshape(bsz, seq, d)
        if layer + 1 < depth:
            x, h = _postnorm(x, y3, mod, norm_post_ffn[layer], 5, nxt=(0, 1),
                             mod_next=mods[layer + 1], g_pre=norm_pre_mix[layer + 1])
        else:
            (x,) = _postnorm(x, y3, mod, norm_post_ffn[layer], 5)
    return x
```

```python
import functools
import math

import jax
import jax.numpy as jnp
from jax import lax
from jax.experimental import pallas as pl
from jax.experimental.pallas import tpu as pltpu

F32 = jnp.float32
BF16 = jnp.bfloat16

LANE = 128
V7X_VMEM_BYTES = 64 * 1024 * 1024
VMEM_CAP = V7X_VMEM_BYTES - 8 * 1024 * 1024

HEAD_DIM = 128
N_BRANCH = 4
IDX_HEADS = 8
IDX_DIM = 64
IDX_TOPK = 256
A_BLOCK = 128
DSA_KEY_RANGE_VARIANTS = 4
B_GROUPS = ((128, 1), (512, 4), (2048, 16))
B_BLOCK = 128
C_CHUNK = 128
C_GROUPS = 8
D_CONV_WIDTH = 3
REL_BUCKETS = 32
REL_MAX_DIST = 2048
TOP_K = 2
N_MOD = 6
EPS = 1e-6
NEG_INF = float("-inf")


def _params(semantics, vmem_bytes):
    limit = int(min(VMEM_CAP, max(vmem_bytes * 5 // 4 + (4 << 20), 16 << 20)))
    return pltpu.CompilerParams(dimension_semantics=semantics, vmem_limit_bytes=limit)


def _pick_tile(n, candidates):
    for c in candidates:
        if n % c == 0:
            return c
    raise ValueError(f"no tile in {candidates} divides {n}")


def _cast_rows(src_fn, dst_ref, n_rows, chunk):
    def body(r, carry):
        rows = pl.ds(pl.multiple_of(r * chunk, chunk), chunk)
        dst_ref[rows, :] = src_fn(rows).astype(BF16)
        return carry
    lax.fori_loop(0, n_rows // chunk, body, 0)


def _mm_body(chunk, lhs_ref, w_ref, o_ref, wb_ref):
    @pl.when(pl.program_id(1) == 0)
    def _():
        _cast_rows(lambda rows: w_ref[rows, :], wb_ref, w_ref.shape[0], chunk)

    o_ref[...] = jnp.dot(lhs_ref[...], wb_ref[...],
                         preferred_element_type=F32).astype(o_ref.dtype)


def _matmul(lhs, w, layer, *, tn=512, tm=512, out_dtype=F32, name="mm"):
    m, k = lhs.shape
    n_out = w.shape[2]
    tn = min(tn, n_out)
    tm = min(tm, m)
    assert n_out % tn == 0 and m % tm == 0 and tn % LANE == 0
    chunk = _pick_tile(k, (256, 128, 64, 8))
    osz = jnp.dtype(out_dtype).itemsize
    vmem = 2 * tm * k * 2 + 2 * k * tn * 4 + k * tn * 2 + 2 * tm * tn * osz + tm * tn * 4 \
        + chunk * tn * 8
    return pl.pallas_call(
        functools.partial(_mm_body, chunk),
        out_shape=jax.ShapeDtypeStruct((m, n_out), out_dtype),
        grid=(n_out // tn, m // tm),
        in_specs=[pl.BlockSpec((tm, k), lambda j, i: (i, 0)),
                  pl.BlockSpec((None, k, tn), lambda j, i: (layer, 0, j))],
        out_specs=pl.BlockSpec((tm, tn), lambda j, i: (i, j)),
        scratch_shapes=[pltpu.VMEM((k, tn), BF16)],
        compiler_params=_params(("arbitrary", "arbitrary"), vmem),
        name=name,
    )(lhs, w)


def _mm_t_body(chunk, lhs_ref, w_ref, o_ref, wb_ref):
    @pl.when(pl.program_id(1) == 0)
    def _():
        _cast_rows(lambda rows: w_ref[rows, :], wb_ref, w_ref.shape[0], chunk)

    acc = lax.dot_general(lhs_ref[...], wb_ref[...], (((1,), (1,)), ((), ())),
                          preferred_element_type=F32)
    o_ref[...] = acc.astype(o_ref.dtype)


def _matmul_t(lhs, w_t, layer, *, row0, n_out, tn=512, tm=512, out_dtype=F32, name="mm_t"):
    m, k = lhs.shape
    tn = min(tn, n_out)
    tm = min(tm, m)
    assert n_out % tn == 0 and m % tm == 0 and row0 % 8 == 0 and tn % LANE == 0
    chunk = _pick_tile(tn, (256, 128))
    osz = jnp.dtype(out_dtype).itemsize
    vmem = 2 * tm * k * 2 + 2 * k * tn * 4 + k * tn * 2 + 2 * tm * tn * osz + tm * tn * 4 \
        + 4 * chunk * k * 4
    return pl.pallas_call(
        functools.partial(_mm_t_body, chunk),
        out_shape=jax.ShapeDtypeStruct((m, n_out), out_dtype),
        grid=(n_out // tn, m // tm),
        in_specs=[pl.BlockSpec((tm, k), lambda j, i: (i, 0)),
                  pl.BlockSpec((None, pl.Element(tn), pl.Element(k)),
                               lambda j, i: (layer, pl.multiple_of(row0 + j * tn, 8), 0))],
        out_specs=pl.BlockSpec((tm, tn), lambda j, i: (i, j)),
        scratch_shapes=[pltpu.VMEM((tn, k), BF16)],
        compiler_params=_params(("arbitrary", "arbitrary"), vmem),
        name=name,
    )(lhs, w_t)


def _ada_body(c_ref, w_ref, b_ref, o_ref):
    c = c_ref[...]
    a = (c * jax.nn.sigmoid(c)).astype(BF16)
    o_ref[...] = jnp.dot(a, w_ref[...].astype(BF16), preferred_element_type=F32) + b_ref[...]


def _ada(c_pad, w_ada, b_ada, tn=512):
    m, k = c_pad.shape
    n = w_ada.shape[1]
    tn = _pick_tile(n, (tn, 256, 128))
    vmem = 2 * k * tn * 4 + k * tn * 2 + 4 * m * k * 4
    return pl.pallas_call(
        _ada_body,
        out_shape=jax.ShapeDtypeStruct((m, n), F32),
        grid=(n // tn,),
        in_specs=[pl.BlockSpec((m, k), lambda j: (0, 0)),
                  pl.BlockSpec((k, tn), lambda j: (0, j)),
                  pl.BlockSpec((1, tn), lambda j: (0, j))],
        out_specs=pl.BlockSpec((m, tn), lambda j: (0, j)),
        compiler_params=_params(("arbitrary",), vmem),
        name="ada",
    )(c_pad, w_ada, b_ada.reshape(1, n))


def _rms(x, g):
    return x * lax.rsqrt(jnp.mean(x * x, axis=-1, keepdims=True) + EPS) * g


def _route(h, router_ref, n_experts):
    rows = h.shape[0]
    lane = lax.broadcasted_iota(jnp.int32, (rows, LANE), 1)
    logits = jnp.full((rows, LANE), NEG_INF, F32)
    for e in range(n_experts):
        le = jnp.sum(h * router_ref[e:e + 1, :], axis=-1, keepdims=True)
        logits = jnp.where(lane == e, le, logits)
    m1 = jnp.max(logits, axis=-1, keepdims=True)
    i1 = jnp.min(jnp.where(logits == m1, lane, LANE), axis=-1, keepdims=True)
    rest = jnp.where(lane == i1, NEG_INF, logits)
    m2 = jnp.max(rest, axis=-1, keepdims=True)
    i2 = jnp.min(jnp.where(rest == m2, lane, LANE), axis=-1, keepdims=True)
    e2 = jnp.exp(m2 - m1)
    den = 1.0 + e2
    member = jnp.where(lane == i1, 1.0, jnp.where(lane == i2, 1.0, 0.0))
    extra = jnp.where(lane == n_experts, i1.astype(F32),
                      jnp.where(lane == n_experts + 1, i2.astype(F32),
                                jnp.where(lane == n_experts + 2, 1.0 / den, e2 / den)))
    return jnp.where(lane < n_experts, member, jnp.where(lane < n_experts + 4, extra, 0.0))


def _pack_bf16_pairs(h):
    half = h.shape[1] // 2
    lo = pltpu.bitcast(h[:, :half].astype(BF16).astype(F32), jnp.uint32)
    hi = pltpu.bitcast(h[:, half:].astype(BF16).astype(F32), jnp.uint32)
    return (hi & jnp.uint32(0xFFFF0000)) | (lo >> 16)


def _unpack_bf16_pairs(u):
    lo = pltpu.bitcast(u << 16, F32).astype(BF16)
    hi = pltpu.bitcast(u & jnp.uint32(0xFFFF0000), F32).astype(BF16)
    return lo, hi


def _pre_body(x_ref, mod_ref, g_ref, h_ref):
    h = _rms(x_ref[...], g_ref[...]) * (1.0 + mod_ref[1:2, :]) + mod_ref[0:1, :]
    h_ref[...] = h.astype(h_ref.dtype)


def _prenorm(x3, mod, g, tr=256):
    b, s, d = x3.shape
    tr = min(tr, s)
    vmem = 2 * tr * d * 4 + 2 * tr * d * 2 + 4 * tr * d * 4
    return pl.pallas_call(
        _pre_body,
        out_shape=jax.ShapeDtypeStruct((b, s, d), BF16),
        grid=(b, s // tr),
        in_specs=[pl.BlockSpec((None, tr, d), lambda bi, i: (bi, i, 0)),
                  pl.BlockSpec((None, N_MOD, d), lambda bi, i: (bi, 0, 0)),
                  pl.BlockSpec((1, d), lambda bi, i: (0, 0))],
        out_specs=pl.BlockSpec((None, tr, d), lambda bi, i: (bi, i, 0)),
        compiler_params=_params(("arbitrary", "arbitrary"), vmem),
        name="prenorm",
    )(x3, mod, g.reshape(1, d))


def _post_body(gate_row, nxt, n_experts, *refs):
    x_ref, y_ref, mod_ref, gpost_ref = refs[:4]
    pos = 4
    if nxt is not None:
        modn_ref, gpre_ref = refs[pos:pos + 2]
        pos += 2
    if n_experts:
        router_ref = refs[pos]
        pos += 1
    xo_ref = refs[pos]
    pos += 1
    xn = x_ref[...] + mod_ref[gate_row:gate_row + 1, :] * _rms(y_ref[...], gpost_ref[...])
    xo_ref[...] = xn
    if nxt is not None:
        shift_row, scale_row = nxt
        h = _rms(xn, gpre_ref[...]) * (1.0 + modn_ref[scale_row:scale_row + 1, :]) \
            + modn_ref[shift_row:shift_row + 1, :]
        if not n_experts:
            refs[pos][...] = h.astype(BF16)
            return
        hp_ref, route_ref, rank_ref, counts_ref, carry_ref = refs[pos:pos + 5]
        hp_ref[...] = _pack_bf16_pairs(h)
        route = _route(h, router_ref, n_experts)
        route_ref[...] = route

        @pl.when((pl.program_id(0) == 0) & (pl.program_id(1) == 0))
        def _():
            carry_ref[...] = jnp.zeros_like(carry_ref)

        rows = route.shape[0]
        lane = lax.broadcasted_iota(jnp.int32, route.shape, 1)
        member = jnp.where(lane < n_experts, route, 0.0)
        ri = lax.broadcasted_iota(jnp.int32, (rows, rows), 0)
        ci = lax.broadcasted_iota(jnp.int32, (rows, rows), 1)
        earlier = jnp.where(ci < ri, 1.0, 0.0).astype(BF16)
        within = jnp.dot(earlier, member.astype(BF16), preferred_element_type=F32)
        rank_ref[...] = within + carry_ref[0:1, :]
        carry_ref[...] = carry_ref[...] + jnp.sum(member, axis=0, keepdims=True)
        counts_ref[...] = carry_ref[...]


def _postnorm(x3, y3, mod, g_post, gate_row, *, nxt=None, mod_next=None, g_pre=None,
              router_t=None, tr=256):
    b, s, d = x3.shape
    tr = min(tr, s)
    n_experts = 0 if router_t is None else router_t.shape[0]
    row = lambda bi, i: (bi, i, 0)
    mod_spec = pl.BlockSpec((None, N_MOD, d), lambda bi, i: (bi, 0, 0))
    in_specs = [pl.BlockSpec((None, tr, d), row), pl.BlockSpec((None, tr, d), row), mod_spec,
                pl.BlockSpec((1, d), lambda bi, i: (0, 0))]
    args = [x3, y3, mod, g_post.reshape(1, d)]
    out_shape = [jax.ShapeDtypeStruct((b, s, d), F32)]
    out_specs = [pl.BlockSpec((None, tr, d), row)]
    scratch = []
    if nxt is not None:
        in_specs += [mod_spec, pl.BlockSpec((1, d), lambda bi, i: (0, 0))]
        args += [mod_next, g_pre.reshape(1, d)]
    if nxt is not None and not n_experts:
        out_shape.append(jax.ShapeDtypeStruct((b, s, d), BF16))
        out_specs.append(pl.BlockSpec((None, tr, d), row))
    if n_experts:
        assert nxt is not None and n_experts + 4 <= LANE
        in_specs.append(pl.BlockSpec((n_experts, d), lambda bi, i: (0, 0)))
        args.append(router_t)
        out_shape += [jax.ShapeDtypeStruct((b, s, d // 2), jnp.uint32),
                      jax.ShapeDtypeStruct((b, s, LANE), F32),
                      jax.ShapeDtypeStruct((b, s, LANE), F32),
                      jax.ShapeDtypeStruct((8, LANE), F32)]
        out_specs += [pl.BlockSpec((None, tr, d // 2), row), pl.BlockSpec((None, tr, LANE), row),
                      pl.BlockSpec((None, tr, LANE), row),
                      pl.BlockSpec((8, LANE), lambda bi, i: (0, 0))]
        scratch.append(pltpu.VMEM((8, LANE), F32))
    vmem = 2 * 3 * tr * d * 4 + 2 * tr * d * 2 + 6 * tr * d * 4
    return pl.pallas_call(
        functools.partial(_post_body, gate_row, nxt, n_experts),
        out_shape=out_shape,
        grid=(b, s // tr),
        in_specs=in_specs,
        out_specs=out_specs,
        scratch_shapes=scratch,
        compiler_params=_params(("arbitrary", "arbitrary"), vmem),
        name="postnorm",
    )(*args)


def _lat_body(alat_ref, pik_ref, g_ref, lat_ref, latt_ref, ikt_ref):
    lat = _rms(alat_ref[...], g_ref[...])
    lat_ref[...] = lat.astype(BF16)
    latt_ref[...] = lat.T.astype(BF16)
    ikt_ref[...] = pik_ref[...].T.astype(BF16)


def _latnorm(pa3, pik3, g_lat, lat_col, n_lat, tr=512):
    b, s, _ = pa3.shape
    tr = min(tr, s)
    assert lat_col % n_lat == 0
    vmem = 2 * tr * (n_lat + LANE) * 4 + 4 * tr * (n_lat + LANE) * 2 + 4 * tr * n_lat * 4
    return pl.pallas_call(
        _lat_body,
        out_shape=[jax.ShapeDtypeStruct((b, s, n_lat), BF16),
                   jax.ShapeDtypeStruct((b, n_lat, s), BF16),
                   jax.ShapeDtypeStruct((b, LANE, s), BF16)],
        grid=(b, s // tr),
        in_specs=[pl.BlockSpec((None, tr, n_lat), lambda bi, i: (bi, i, lat_col // n_lat)),
                  pl.BlockSpec((None, tr, LANE), lambda bi, i: (bi, i, 0)),
                  pl.BlockSpec((1, n_lat), lambda bi, i: (0, 0))],
        out_specs=[pl.BlockSpec((None, tr, n_lat), lambda bi, i: (bi, i, 0)),
                   pl.BlockSpec((None, n_lat, tr), lambda bi, i: (bi, 0, i)),
                   pl.BlockSpec((None, LANE, tr), lambda bi, i: (bi, 0, i))],
        compiler_params=_params(("arbitrary", "arbitrary"), vmem),
        name="latnorm",
    )(pa3, pik3, g_lat.reshape(1, n_lat))


def _dsa_body(topk, n_variants, *refs):
    i = pl.program_id(1)
    n_blk = refs[4].shape[0] // A_BLOCK
    per = -(-n_blk // n_variants)
    for v in range(n_variants):
        blocks = min(n_blk, (v + 1) * per)

        @pl.when((i >= v * per) & (i < (v + 1) * per))
        def _(blocks=blocks):
            _dsa_compute(topk, blocks * A_BLOCK, i, *refs)


def _dsa_compute(topk, seq, i, q_ref, iq_ref, iw_ref, ikt_ref, lat_ref, latt_ref, wuk_ref, wuv_ref,
                 bias_ref, o_ref, madd_ref):
    n_blk = seq // A_BLOCK
    n_heads = wuk_ref.shape[0]
    int_min = jnp.int32(-2 ** 31)
    qpos = i * A_BLOCK + lax.broadcasted_iota(jnp.int32, (A_BLOCK, 1), 0)
    kpos = lax.broadcasted_iota(jnp.int32, (A_BLOCK, seq), 1)

    iq = iq_ref[...].astype(BF16)
    iw = iw_ref[...]
    ikt = ikt_ref[0:IDX_DIM, 0:seq]
    score = jnp.zeros((A_BLOCK, seq), F32)
    for h in range(IDX_HEADS):
        l = jnp.dot(iq[:, h * IDX_DIM:(h + 1) * IDX_DIM], ikt, preferred_element_type=F32)
        score = score + iw[:, IDX_DIM + h:IDX_DIM + h + 1] * jnp.maximum(l, 0.0)

    score = jnp.where(score == 0.0, 0.0, score)
    key = pltpu.bitcast(score, jnp.int32)
    key = jnp.where(key < 0, key ^ jnp.int32(0x7FFFFFFF), key)
    key = jnp.where(kpos <= qpos, key, int_min)
    k_eff = jnp.minimum(qpos + 1, topk).astype(F32)

    def count(mask):
        return jnp.sum(jnp.where(mask, 1.0, 0.0), axis=1, keepdims=True)

    t0 = jnp.where(count(key >= 0) >= k_eff, jnp.int32(0), int_min)

    def value_step(it, t):
        c = t | (jnp.int32(1) << (30 - it))
        return jnp.where(count(key >= c) >= k_eff, c, t)

    t = lax.fori_loop(0, 31, value_step, t0)

    need = k_eff - count(key > t)
    eq = key == t
    partial_tie = jnp.max(count(eq) - need) > 0.0

    @pl.when(jnp.logical_not(partial_tie))
    def _():
        madd_ref[:, 0:seq] = jnp.where(key >= t, 0.0, NEG_INF)

    @pl.when(partial_tie)
    def _():
        idx_bits = max(1, (seq - 1).bit_length())

        def index_step(it, p):
            c = p | (jnp.int32(1) << (idx_bits - 1 - it))
            below = jnp.sum(jnp.where(eq, jnp.where(kpos < c, 1.0, 0.0), 0.0),
                            axis=1, keepdims=True)
            return jnp.where(below < need, c, p)

        p = lax.fori_loop(0, idx_bits, index_step, jnp.zeros((A_BLOCK, 1), jnp.int32))
        tie_add = jnp.where(eq, jnp.where(kpos <= p, 0.0, NEG_INF), NEG_INF)
        madd_ref[:, 0:seq] = jnp.where(key > t, 0.0, tie_add)

    scale = HEAD_DIM ** -0.5
    for h in range(n_heads):
        cols = slice(h * HEAD_DIM, (h + 1) * HEAD_DIM)
        qa = jnp.dot(q_ref[:, cols].astype(BF16), wuk_ref[h],
                     preferred_element_type=F32).astype(BF16)
        lg = jnp.dot(qa, latt_ref[:, 0:seq], preferred_element_type=F32) * scale
        bias = jnp.concatenate([bias_ref[jnp.maximum(i - j, 0), h] for j in range(n_blk)], axis=1)
        lg = lg + bias + madd_ref[:, 0:seq]
        m = jnp.max(lg, axis=1, keepdims=True)
        e = jnp.exp(lg - m)
        den = jnp.sum(e, axis=1, keepdims=True)
        o_lat = jnp.dot(e.astype(BF16), lat_ref[0:seq, :], preferred_element_type=F32) / den
        o_ref[:, cols] = jnp.dot(o_lat.astype(BF16), wuv_ref[h],
                                 preferred_element_type=F32).astype(o_ref.dtype)


def _dsa(pa3, pik3, lat, latt, ikt, wuk_t, wuv_t, bias_a, *, q_width, iq_col, iq_width,
         n_variants):
    b, s, _ = pa3.shape
    n_lat = lat.shape[-1]
    n_heads = wuk_t.shape[0]
    n_blk = s // A_BLOCK
    topk = min(IDX_TOPK, s // 4)
    assert iq_col % iq_width == 0
    vmem = 2 * (A_BLOCK * (q_width + iq_width + LANE) * 4 + LANE * s * 2 + 2 * s * n_lat * 2
                + 2 * n_heads * HEAD_DIM * n_lat * 2 + n_heads * n_blk * A_BLOCK * A_BLOCK * 4
                + A_BLOCK * q_width * 2) + 10 * A_BLOCK * s * 4
    return pl.pallas_call(
        functools.partial(_dsa_body, topk, n_variants),
        out_shape=jax.ShapeDtypeStruct((b, s, q_width), BF16),
        grid=(b, n_blk),
        in_specs=[pl.BlockSpec((None, A_BLOCK, q_width), lambda bi, i: (bi, i, 0)),
                  pl.BlockSpec((None, A_BLOCK, iq_width), lambda bi, i: (bi, i, iq_col // iq_width)),
                  pl.BlockSpec((None, A_BLOCK, LANE), lambda bi, i: (bi, i, 0)),
                  pl.BlockSpec((None, LANE, s), lambda bi, i: (bi, 0, 0)),
                  pl.BlockSpec((None, s, n_lat), lambda bi, i: (bi, 0, 0)),
                  pl.BlockSpec((None, n_lat, s), lambda bi, i: (bi, 0, 0)),
                  pl.BlockSpec(wuk_t.shape, lambda bi, i: (0, 0, 0)),
                  pl.BlockSpec(wuv_t.shape, lambda bi, i: (0, 0, 0)),
                  pl.BlockSpec(bias_a.shape, lambda bi, i: (0, 0, 0, 0))],
        out_specs=pl.BlockSpec((None, A_BLOCK, q_width), lambda bi, i: (bi, i, 0)),
        scratch_shapes=[pltpu.VMEM((A_BLOCK, s), F32)],
        compiler_params=_params(("arbitrary", "arbitrary"), vmem),
        name="dsa",
    )(pa3, pa3, pik3, ikt, lat, latt, wuk_t, wuv_t, bias_a)


def _dil_body(span, dil, hb, has_prev, *refs):
    if has_prev:
        q_ref, kp_ref, kc_ref, vp_ref, vc_ref, bias_ref, o_ref, lse_ref = refs
    else:
        q_ref, kc_ref, vc_ref, bias_ref, o_ref, lse_ref = refs
    i = pl.program_id(1)
    hblk = pl.program_id(2)
    n_keys = 2 * B_BLOCK if has_prev else B_BLOCK
    key0 = 0 if has_prev else B_BLOCK
    qi = lax.broadcasted_iota(jnp.int32, (B_BLOCK, n_keys), 0)
    kj = lax.broadcasted_iota(jnp.int32, (B_BLOCK, n_keys), 1) + key0
    sub_dist = qi + B_BLOCK - kj
    in_band = jnp.where(sub_dist <= span,
                        jnp.where((i - 1) * B_BLOCK + kj >= 0, 0.0, NEG_INF), NEG_INF)
    madd = jnp.where(sub_dist >= 0, in_band, NEG_INF)
    lane = lax.broadcasted_iota(jnp.int32, (B_BLOCK, LANE), 1)
    scale = HEAD_DIM ** -0.5

    @pl.when(hblk == 0)
    def _():
        lse_ref[...] = jnp.zeros_like(lse_ref)

    for r in range(dil):
        rows = pl.ds(r, B_BLOCK, stride=dil) if dil > 1 else slice(None)
        lse_acc = lse_ref[rows, :]
        for h in range(hb):
            cols = slice(h * HEAD_DIM, (h + 1) * HEAD_DIM)
            q = q_ref[rows, cols].astype(BF16)
            if has_prev:
                k2 = jnp.concatenate([kp_ref[rows, cols], kc_ref[rows, cols]], axis=0).astype(BF16)
                v2 = jnp.concatenate([vp_ref[rows, cols], vc_ref[rows, cols]], axis=0).astype(BF16)
            else:
                k2 = kc_ref[rows, cols].astype(BF16)
                v2 = vc_ref[rows, cols].astype(BF16)
            lg = lax.dot_general(q, k2, (((1,), (1,)), ((), ())), preferred_element_type=F32)
            lg = lg * scale + bias_ref[h, :, key0:] + madd
            m = jnp.max(lg, axis=1, keepdims=True)
            e = jnp.exp(lg - m)
            den = jnp.sum(e, axis=1, keepdims=True)
            o_ref[rows, cols] = jnp.dot((e / den).astype(BF16), v2, preferred_element_type=F32)
            lse_acc = jnp.where(lane == hblk * hb + h, m + jnp.log(den), lse_acc)
        lse_ref[rows, :] = lse_acc


def _dilated(rest3, bias_b, g, n_groups, window, dil, width):
    b, s, w_all = rest3.shape
    n_heads = width // HEAD_DIM
    span = window // dil
    n_sub = s // dil
    assert span == B_BLOCK and n_sub % B_BLOCK == 0 and w_all % width == 0
    nb = n_sub // B_BLOCK
    rows = B_BLOCK * dil
    hb = n_heads if dil == 1 else 1
    assert HEAD_DIM == LANE
    n_hblk = n_heads // hb
    has_prev = nb > 1
    cur = lambda slot: (lambda bi, i, hk: (bi, i, (slot * n_groups + g) * n_hblk + hk))
    prev = lambda slot: (lambda bi, i, hk: (bi, jnp.maximum(i - 1, 0),
                                            (slot * n_groups + g) * n_hblk + hk))
    blk = (None, rows, hb * HEAD_DIM)
    in_specs = [pl.BlockSpec(blk, cur(0))]
    for slot in (1, 2):
        if has_prev:
            in_specs.append(pl.BlockSpec(blk, prev(slot)))
        in_specs.append(pl.BlockSpec(blk, cur(slot)))
    in_specs.append(pl.BlockSpec((None, hb, B_BLOCK, 2 * B_BLOCK), lambda bi, i, hk: (g, hk, 0, 0)))
    vmem = 2 * (len(in_specs) * rows * hb * HEAD_DIM + rows * LANE) * 4 \
        + 2 * hb * B_BLOCK * 2 * B_BLOCK * 4 + (4 << 20)
    o, lse = pl.pallas_call(
        functools.partial(_dil_body, span, dil, hb, has_prev),
        out_shape=[jax.ShapeDtypeStruct((b, s, width), F32),
                   jax.ShapeDtypeStruct((b, s, LANE), F32)],
        grid=(b, nb, n_hblk),
        in_specs=in_specs,
        out_specs=[pl.BlockSpec(blk, lambda bi, i, hk: (bi, i, hk)),
                   pl.BlockSpec((None, rows, LANE), lambda bi, i, hk: (bi, i, 0))],
        compiler_params=_params(("arbitrary", "arbitrary", "arbitrary"), vmem),
        name=f"dilated{g}",
    )(*([rest3] * (len(in_specs) - 1)), bias_b)
    return o.reshape(b * s, width), lse.reshape(b * s, LANE)


def _mix_body(n_groups, n_heads, *refs):
    o_refs = refs[:n_groups]
    l_refs = refs[n_groups:2 * n_groups]
    out_ref = refs[2 * n_groups]
    ls = [r[...] for r in l_refs]
    m = functools.reduce(jnp.maximum, ls)
    es = [jnp.exp(l - m) for l in ls]
    tot = functools.reduce(jnp.add, es)
    ws = [e / tot for e in es]
    for h in range(n_heads):
        cols = slice(h * HEAD_DIM, (h + 1) * HEAD_DIM)
        acc = ws[0][:, h:h + 1] * o_refs[0][:, cols]
        for gi in range(1, n_groups):
            acc = acc + ws[gi][:, h:h + 1] * o_refs[gi][:, cols]
        out_ref[:, cols] = acc.astype(out_ref.dtype)


def _mixture(outs, lses, tr=512):
    t, width = outs[0].shape
    tr = min(tr, t)
    n_groups = len(outs)
    vmem = 2 * n_groups * tr * (width + LANE) * 4 + 2 * tr * width * 2 + 4 * tr * width * 4
    return pl.pallas_call(
        functools.partial(_mix_body, n_groups, width // HEAD_DIM),
        out_shape=jax.ShapeDtypeStruct((t, width), BF16),
        grid=(t // tr,),
        in_specs=[pl.BlockSpec((tr, width), lambda i: (i, 0))] * n_groups
        + [pl.BlockSpec((tr, LANE), lambda i: (i, 0))] * n_groups,
        out_specs=pl.BlockSpec((tr, width), lambda i: (i, 0)),
        compiler_params=_params(("arbitrary",), vmem),
        name="mixture",
    )(*outs, *lses)


def _gmlp_body(u_ref, v_ref, g_ref, b_ref, ws_ref, bs_ref, o_ref):
    u = jax.nn.gelu(u_ref[...])
    v = jax.nn.gelu(v_ref[...])
    mu = jnp.mean(v, axis=-1, keepdims=True)
    var = jnp.mean(jnp.square(v - mu), axis=-1, keepdims=True)
    v = (v - mu) * lax.rsqrt(var + EPS) * g_ref[...] + b_ref[...]
    ti = lax.broadcasted_iota(jnp.int32, (C_CHUNK, C_CHUNK), 0)
    si = lax.broadcasted_iota(jnp.int32, (C_CHUNK, C_CHUNK), 1)
    causal = si <= ti
    gw = u.shape[1] // C_GROUPS
    for g in range(C_GROUPS):
        cols = slice(g * gw, (g + 1) * gw)
        w = jnp.where(causal, ws_ref[g], 0.0).astype(BF16)
        mixed = jnp.dot(w, v[:, cols].astype(BF16), preferred_element_type=F32) + bs_ref[:, g:g + 1]
        o_ref[:, cols] = (u[:, cols] * mixed).astype(o_ref.dtype)


def _gmlp(rest, u_col, width, ln_g, ln_b, w_s, b_s_t):
    t = rest.shape[0]
    assert u_col % width == 0 and (width // C_GROUPS) % LANE == 0
    vmem = 2 * 2 * C_CHUNK * width * 4 + 2 * C_CHUNK * width * 2 + 2 * C_GROUPS * C_CHUNK * C_CHUNK * 4 \
        + 8 * C_CHUNK * width * 4
    return pl.pallas_call(
        _gmlp_body,
        out_shape=jax.ShapeDtypeStruct((t, width), BF16),
        grid=(t // C_CHUNK,),
        in_specs=[pl.BlockSpec((C_CHUNK, width), lambda i: (i, u_col // width)),
                  pl.BlockSpec((C_CHUNK, width), lambda i: (i, u_col // width + 1)),
                  pl.BlockSpec((1, width), lambda i: (0, 0)),
                  pl.BlockSpec((1, width), lambda i: (0, 0)),
                  pl.BlockSpec(w_s.shape, lambda i: (0, 0, 0)),
                  pl.BlockSpec(b_s_t.shape, lambda i: (0, 0))],
        out_specs=pl.BlockSpec((C_CHUNK, width), lambda i: (i, 0)),
        compiler_params=_params(("arbitrary",), vmem),
        name="gmlp",
    )(rest, rest, ln_g.reshape(1, width), ln_b.reshape(1, width), w_s, b_s_t)


def _conv_body(h_ref, bg_ref, cg_ref, w_ref, o_ref, carry_ref):
    @pl.when(pl.program_id(1) == 0)
    def _():
        carry_ref[...] = jnp.zeros_like(carry_ref)

    ch = cg_ref[...] * h_ref[...]
    rows = ch.shape[0]
    row = lax.broadcasted_iota(jnp.int32, ch.shape, 0)
    pm1 = carry_ref[7:8, :]
    pm2 = carry_ref[6:7, :]
    s1 = jnp.where(row == 0, pm1, pltpu.roll(ch, 1, 0))
    s2 = jnp.where(row == 0, pm2, jnp.where(row == 1, pm1, pltpu.roll(ch, 2, 0)))
    z = w_ref[0:1, :] * s2 + w_ref[1:2, :] * s1 + w_ref[2:3, :] * ch
    o_ref[...] = (bg_ref[...] * z).astype(o_ref.dtype)
    carry_ref[...] = ch[rows - 8:, :]


def _short_conv(rest3, h_col, width, conv_w, tr=256):
    b, s, _ = rest3.shape
    tr = min(tr, s)
    assert h_col % width == 0 and conv_w.shape[0] == D_CONV_WIDTH
    cb = h_col // width
    vmem = 2 * 3 * tr * width * 4 + 2 * tr * width * 2 + 6 * tr * width * 4
    return pl.pallas_call(
        _conv_body,
        out_shape=jax.ShapeDtypeStruct((b, s, width), BF16),
        grid=(b, s // tr),
        in_specs=[pl.BlockSpec((None, tr, width), lambda bi, i: (bi, i, cb)),
                  pl.BlockSpec((None, tr, width), lambda bi, i: (bi, i, cb + 1)),
                  pl.BlockSpec((None, tr, width), lambda bi, i: (bi, i, cb + 2)),
                  pl.BlockSpec(conv_w.shape, lambda bi, i: (0, 0))],
        out_specs=pl.BlockSpec((None, tr, width), lambda bi, i: (bi, i, 0)),
        scratch_shapes=[pltpu.VMEM((8, width), F32)],
        compiler_params=_params(("arbitrary", "arbitrary"), vmem),
        name="short_conv",
    )(rest3, rest3, rest3, conv_w)


def _merge_body(chunk, *refs):
    br_refs = refs[:N_BRANCH]
    gate_refs = refs[N_BRANCH:2 * N_BRANCH]
    w_ref, o_ref, wb_ref = refs[2 * N_BRANCH:]

    @pl.when(pl.program_id(1) == 0)
    def _():
        for n in range(N_BRANCH):
            _cast_rows(lambda rows, n=n: w_ref[n, rows, :], wb_ref.at[n], w_ref.shape[1], chunk)

    acc = None
    for n in range(N_BRANCH):
        proj = jnp.dot(br_refs[n][...], wb_ref[n], preferred_element_type=F32)
        term = jax.nn.sigmoid(gate_refs[n][...].astype(F32)) * proj
        acc = term if acc is None else acc + term
    o_ref[...] = acc.astype(o_ref.dtype)


def _merge(branches, gates, w_branch, layer, tn=512, tm=1024):
    t, width = branches[0].shape
    d = w_branch.shape[-1]
    tn = min(tn, d)
    tm = min(tm, t)
    assert d % tn == 0 and t % tm == 0 and gates.shape == (t, N_BRANCH * d)
    chunk = _pick_tile(width, (256, 128, 8))
    gate_spec = lambda n: pl.BlockSpec((tm, tn), lambda j, i: (i, (n * d) // tn + j))
    vmem = 2 * N_BRANCH * (tm * width * 2 + tm * tn * 2 + width * tn * 4) + N_BRANCH * width * tn * 2 \
        + 2 * tm * tn * 2 + 4 * tm * tn * 4
    return pl.pallas_call(
        functools.partial(_merge_body, chunk),
        out_shape=jax.ShapeDtypeStruct((t, d), BF16),
        grid=(d // tn, t // tm),
        in_specs=[pl.BlockSpec((tm, width), lambda j, i: (i, 0))] * N_BRANCH
        + [gate_spec(n) for n in range(N_BRANCH)]
        + [pl.BlockSpec((None, N_BRANCH, width, tn), lambda j, i: (layer, 0, 0, j))],
        out_specs=pl.BlockSpec((tm, tn), lambda j, i: (i, j)),
        scratch_shapes=[pltpu.VMEM((N_BRANCH, width, tn), BF16)],
        compiler_params=_params(("arbitrary", "arbitrary"), vmem),
        name="merge",
    )(*branches, *([gates] * N_BRANCH), w_branch)


def _up_body(chunk, h_ref, wg_ref, wu_ref, o_ref, wgb_ref, wub_ref):
    @pl.when(pl.program_id(1) == 0)
    def _():
        _cast_rows(lambda rows: wg_ref[rows, :], wgb_ref, wg_ref.shape[0], chunk)
        _cast_rows(lambda rows: wu_ref[rows, :], wub_ref, wu_ref.shape[0], chunk)

    h = h_ref[...]
    g = jnp.dot(h, wgb_ref[...], preferred_element_type=F32)
    u = jnp.dot(h, wub_ref[...], preferred_element_type=F32)
    o_ref[...] = (g * jax.nn.sigmoid(g) * u).astype(o_ref.dtype)


def _swiglu_up(h, wg, wu, tm=1024):
    t, k = h.shape
    n_e, _, f = wg.shape
    tn = _pick_tile(f, (256, 128))
    tm = min(tm, t)
    nj = f // tn
    chunk = _pick_tile(k, (256, 128, 8))
    w_spec = pl.BlockSpec((None, k, tn), lambda j, i: (j // nj, 0, j % nj))
    vmem = 2 * tm * k * 2 + 4 * k * tn * 4 + 2 * k * tn * 2 + 2 * tm * tn * 2 + 4 * tm * tn * 4
    return pl.pallas_call(
        functools.partial(_up_body, chunk),
        out_shape=jax.ShapeDtypeStruct((t, n_e * f), BF16),
        grid=(n_e * nj, t // tm),
        in_specs=[pl.BlockSpec((tm, k), lambda j, i: (i, 0)), w_spec, w_spec],
        out_specs=pl.BlockSpec((tm, tn), lambda j, i: (i, j)),
        scratch_shapes=[pltpu.VMEM((k, tn), BF16), pltpu.VMEM((k, tn), BF16)],
        compiler_params=_params(("arbitrary", "arbitrary"), vmem),
        name="swiglu_up",
    )(h, wg, wu)


def _round_body(w_ref, o_ref):
    o_ref[...] = w_ref[...].astype(o_ref.dtype)


def _round_bf16(w, layer, tr=512):
    _, k, n = w.shape
    tr = _pick_tile(k, (tr, 256, 128, 8))
    return pl.pallas_call(
        _round_body,
        out_shape=jax.ShapeDtypeStruct((k, n), BF16),
        grid=(k // tr,),
        in_specs=[pl.BlockSpec((None, tr, n), lambda i: (layer, i, 0))],
        out_specs=pl.BlockSpec((tr, n), lambda i: (i, 0)),
        compiler_params=_params(("arbitrary",), 2 * tr * n * 6),
        name="round_bf16",
    )(w)


def _down_body(a_ref, w_ref, o_ref):
    o_ref[...] = jnp.dot(a_ref[...], w_ref[...], preferred_element_type=F32)


def _swiglu_down(a, wd, layer, tn=512, tm=512):
    t, k = a.shape
    n = wd.shape[2]
    tn = min(tn, n)
    tm = min(tm, t)
    assert n % tn == 0 and t % tm == 0
    wb = _round_bf16(wd, layer)
    vmem = 2 * tm * k * 2 + 2 * k * tn * 2 + 2 * tm * tn * 4 + tm * tn * 4
    return pl.pallas_call(
        _down_body,
        out_shape=jax.ShapeDtypeStruct((t, n), F32),
        grid=(n // tn, t // tm),
        in_specs=[pl.BlockSpec((tm, k), lambda j, i: (i, 0)),
                  pl.BlockSpec((k, tn), lambda j, i: (0, j))],
        out_specs=pl.BlockSpec((tm, tn), lambda j, i: (i, j)),
        compiler_params=_params(("arbitrary", "arbitrary"), vmem),
        name="swiglu_down",
    )(a, wb)


MOE_TILE = 256


def _dispatch_body(n_tok, pos1_ref, pos2_ref, h_ref, o_ref, src_ref, buf_ref, sem):
    r = pl.program_id(0)

    def row_copy(tile, slot, k):
        return pltpu.make_async_copy(h_ref.at[pl.ds(src_ref[tile * MOE_TILE + k], 1), :],
                                     buf_ref.at[slot, pl.ds(k, 1), :], sem.at[slot])

    def start_tile(tile, slot):
        lax.fori_loop(0, MOE_TILE, lambda k, c: (row_copy(tile, slot, k).start(), c)[1], 0,
                      unroll=8)

    @pl.when(r == 0)
    def _():
        def clear(p, c):
            src_ref[p] = 0
            return c

        def invert(t, c):
            src_ref[pos1_ref[t]] = t
            src_ref[pos2_ref[t]] = t
            return c

        lax.fori_loop(0, src_ref.shape[0], clear, 0)
        lax.fori_loop(0, n_tok, invert, 0)
        start_tile(0, 0)

    @pl.when(r + 1 < pl.num_programs(0))
    def _():
        start_tile(r + 1, (r + 1) % 2)

    slot = r % 2
    pltpu.make_async_copy(h_ref.at[pl.ds(0, MOE_TILE), :], buf_ref.at[slot], sem.at[slot]).wait()
    o_ref[...] = buf_ref[slot]


def _moe_dispatch(hp, pos1, pos2, n_rows):
    t, w = hp.shape
    assert n_rows % MOE_TILE == 0
    return pl.pallas_call(
        functools.partial(_dispatch_body, t),
        out_shape=jax.ShapeDtypeStruct((n_rows, w), hp.dtype),
        grid_spec=pltpu.PrefetchScalarGridSpec(
            num_scalar_prefetch=2,
            grid=(n_rows // MOE_TILE,),
            in_specs=[pl.BlockSpec(memory_space=pl.ANY)],
            out_specs=pl.BlockSpec((MOE_TILE, w), lambda r, p1, p2: (r, 0)),
            scratch_shapes=[pltpu.SMEM((n_rows,), jnp.int32),
                            pltpu.VMEM((2, MOE_TILE, w), hp.dtype),
                            pltpu.SemaphoreType.DMA((2,))]),
        compiler_params=_params(("arbitrary",), 6 * MOE_TILE * w * 4),
        name="moe_dispatch",
    )(pos1, pos2, hp)


def _group_body(swiglu, chunk, te_ref, nused_ref, *refs):
    if swiglu:
        x_ref, wg_ref, wu_ref, o_ref, wgb_ref, wub_ref = refs
        pairs = ((wg_ref, wgb_ref), (wu_ref, wub_ref))
    else:
        x_ref, w_ref, o_ref, wb_ref = refs
        pairs = ((w_ref, wb_ref),)
    r = pl.program_id(1)
    expert = te_ref[r]
    prev = te_ref[jnp.maximum(r - 1, 0)]

    @pl.when(jnp.logical_or(r == 0, expert != prev))
    def _():
        for src_ref, dst_ref in pairs:
            _cast_rows(lambda rows, s=src_ref: s[rows, :], dst_ref, src_ref.shape[0], chunk)

    @pl.when(r < nused_ref[0])
    def _():
        if swiglu:
            lo, hi = _unpack_bf16_pairs(x_ref[...])
            half = lo.shape[1]

            def proj(wb):
                return jnp.dot(lo, wb[0:half, :], preferred_element_type=F32) \
                    + jnp.dot(hi, wb[half:, :], preferred_element_type=F32)

            g = proj(wgb_ref)
            o_ref[...] = (g * jax.nn.sigmoid(g) * proj(wub_ref)).astype(o_ref.dtype)
        else:
            o_ref[...] = jnp.dot(x_ref[...], wb_ref[...],
                                 preferred_element_type=F32).astype(o_ref.dtype)

    @pl.when(r >= nused_ref[0])
    def _():
        o_ref[...] = jnp.zeros_like(o_ref)


def _grouped_matmul(xs, weights, layer, tile_expert, n_used, out_dtype, tn, name):
    p, kx = xs.shape
    swiglu = len(weights) == 2
    _, _, k, n = weights[0].shape
    assert k == (2 * kx if swiglu else kx) and n % tn == 0 and p % MOE_TILE == 0
    chunk = _pick_tile(k, (256, 128, 8))
    w_spec = pl.BlockSpec((None, None, k, tn), lambda j, r, te, nu: (layer, te[r], 0, j))
    osz = jnp.dtype(out_dtype).itemsize
    vmem = 2 * MOE_TILE * kx * xs.dtype.itemsize + len(weights) * (2 * k * tn * 4 + k * tn * 2) \
        + 2 * MOE_TILE * tn * osz + 6 * MOE_TILE * tn * 4 + 2 * MOE_TILE * k * 2
    return pl.pallas_call(
        functools.partial(_group_body, swiglu, chunk),
        out_shape=jax.ShapeDtypeStruct((p, n), out_dtype),
        grid_spec=pltpu.PrefetchScalarGridSpec(
            num_scalar_prefetch=2,
            grid=(n // tn, p // MOE_TILE),
            in_specs=[pl.BlockSpec((MOE_TILE, kx), lambda j, r, te, nu: (r, 0))]
            + [w_spec] * len(weights),
            out_specs=pl.BlockSpec((MOE_TILE, tn), lambda j, r, te, nu: (r, j)),
            scratch_shapes=[pltpu.VMEM((k, tn), BF16)] * len(weights)),
        compiler_params=_params(("arbitrary", "arbitrary"), vmem),
        name=name,
    )(tile_expert, n_used, xs, *weights)


def _combine_body(tc, n_experts, pos1_ref, pos2_ref, route_ref, ye_ref, o_ref, buf_ref, sem):
    i = pl.program_id(0)

    def row_copies(tile, slot, k):
        t = tile * tc + k
        return (pltpu.make_async_copy(ye_ref.at[pl.ds(pos1_ref[t], 1), :],
                                      buf_ref.at[slot, 0, pl.ds(k, 1), :], sem.at[slot]),
                pltpu.make_async_copy(ye_ref.at[pl.ds(pos2_ref[t], 1), :],
                                      buf_ref.at[slot, 1, pl.ds(k, 1), :], sem.at[slot]))

    def start_tile(tile, slot):
        def body(k, c):
            for cp in row_copies(tile, slot, k):
                cp.start()
            return c
        lax.fori_loop(0, tc, body, 0, unroll=8)

    def wait_tile(slot):
        for half in range(2):
            pltpu.make_async_copy(ye_ref.at[pl.ds(0, tc), :], buf_ref.at[slot, half],
                                  sem.at[slot]).wait()

    @pl.when(i == 0)
    def _():
        start_tile(0, 0)

    @pl.when(i + 1 < pl.num_programs(0))
    def _():
        start_tile(i + 1, (i + 1) % 2)

    slot = i % 2
    wait_tile(slot)
    g1 = route_ref[:, n_experts + 2:n_experts + 3]
    g2 = route_ref[:, n_experts + 3:n_experts + 4]
    o_ref[...] = g1 * buf_ref[slot, 0] + g2 * buf_ref[slot, 1]


def _moe_combine(ye, route, pos1, pos2, n_experts, tc=128):
    t = route.shape[0]
    d = ye.shape[1]
    tc = min(tc, t)
    assert t % tc == 0
    return pl.pallas_call(
        functools.partial(_combine_body, tc, n_experts),
        out_shape=jax.ShapeDtypeStruct((t, d), F32),
        grid_spec=pltpu.PrefetchScalarGridSpec(
            num_scalar_prefetch=2,
            grid=(t // tc,),
            in_specs=[pl.BlockSpec((tc, LANE), lambda i, p1, p2: (i, 0)),
                      pl.BlockSpec(memory_space=pl.ANY)],
            out_specs=pl.BlockSpec((tc, d), lambda i, p1, p2: (i, 0)),
            scratch_shapes=[pltpu.VMEM((2, 2, tc, d), F32), pltpu.SemaphoreType.DMA((2,))]),
        compiler_params=_params(("arbitrary",), 4 * tc * d * 4 + 4 * tc * d * 4),
        name="moe_combine",
    )(pos1, pos2, route, ye)


def _moe(hp, route, rank, counts, layer_idx, w_gate, w_up, w_down):
    t = hp.shape[0]
    n_e = w_gate.shape[1]
    n_rows = TOP_K * t + n_e * MOE_TILE
    n_tiles = n_rows // MOE_TILE
    i1 = route[:, n_e].astype(jnp.int32)
    i2 = route[:, n_e + 1].astype(jnp.int32)
    cnt = counts[0, :n_e].astype(jnp.int32)
    tiles = (cnt + MOE_TILE - 1) // MOE_TILE
    tile_end = jnp.cumsum(tiles)
    row0 = (tile_end - tiles) * MOE_TILE
    slot = row0[None, :] + rank[:, :n_e].astype(jnp.int32)
    experts = jnp.arange(n_e, dtype=jnp.int32)[None, :]
    pos1 = jnp.sum(jnp.where(experts == i1[:, None], slot, 0), axis=1).astype(jnp.int32)
    pos2 = jnp.sum(jnp.where(experts == i2[:, None], slot, 0), axis=1).astype(jnp.int32)
    n_used = tile_end[-1:].astype(jnp.int32)
    tile_ids = jnp.minimum(jnp.arange(n_tiles, dtype=jnp.int32), n_used[0] - 1)
    tile_expert = jnp.sum(tile_ids[:, None] >= tile_end[None, :], axis=1).astype(jnp.int32)

    xs = _moe_dispatch(hp, pos1, pos2, n_rows)
    f = w_gate.shape[-1]
    a = _grouped_matmul(xs, (w_gate, w_up), layer_idx, tile_expert, n_used, BF16,
                        _pick_tile(f, (512, 256, 128)), "moe_up")
    ye = _grouped_matmul(a, (w_down,), layer_idx, tile_expert, n_used, F32,
                         _pick_tile(w_down.shape[-1], (1024, 512, 256, 128)), "moe_down")
    return _moe_combine(ye, route, pos1, pos2, n_e)


def _rel_bucket(dist):
    max_exact = REL_BUCKETS // 2
    d = jnp.maximum(dist, 0)
    df = jnp.maximum(d, max_exact).astype(F32)
    large = max_exact + (jnp.log(df / max_exact) / math.log(REL_MAX_DIST / max_exact)
                         * (REL_BUCKETS - max_exact)).astype(jnp.int32)
    large = jnp.minimum(large, REL_BUCKETS - 1)
    return jnp.where(d < max_exact, d, large)


def _bias_body(n_heads, head0, head_stride, bucket_ref, tab_ref, o_ref):
    bucket = bucket_ref[...]
    base = head0 + head_stride * pl.program_id(0)
    for h in range(n_heads):
        acc = jnp.zeros(bucket.shape, F32)
        for b in range(REL_BUCKETS):
            acc = jnp.where(bucket == b, tab_ref[b, base + h], acc)
        o_ref[h] = acc


def _bias_tiles(rel_bias, buckets, n_heads, head0, head_stride):
    n_tiles, q, k = buckets.shape
    return pl.pallas_call(
        functools.partial(_bias_body, n_heads, head0, head_stride),
        out_shape=jax.ShapeDtypeStruct((n_tiles, n_heads, q, k), F32),
        grid=(n_tiles,),
        in_specs=[pl.BlockSpec((None, q, k), lambda t: (t, 0, 0)),
                  pl.BlockSpec(memory_space=pltpu.SMEM)],
        out_specs=pl.BlockSpec((None, n_heads, q, k), lambda t: (t, 0, 0, 0)),
        compiler_params=_params(("arbitrary",), 4 * (n_heads + 1) * q * k * 4),
        name="bias_tiles",
    )(buckets, rel_bias)


def _bias_tiles_a(rel_bias, n_heads, seq):
    n_blk = seq // A_BLOCK
    qi = jnp.arange(A_BLOCK)[:, None]
    kj = jnp.arange(A_BLOCK)[None, :]
    dist = jnp.arange(n_blk)[:, None, None] * A_BLOCK + (qi - kj)[None]
    return _bias_tiles(rel_bias, _rel_bucket(dist), n_heads, 0, 0)


def _bias_tiles_b(rel_bias, head0, n_heads):
    qi = jnp.arange(B_BLOCK)[:, None]
    kj = jnp.arange(2 * B_BLOCK)[None, :]
    sub_dist = qi + B_BLOCK - kj
    buckets = jnp.stack([_rel_bucket(sub_dist * dil) for _, dil in B_GROUPS])
    return _bias_tiles(rel_bias, buckets, n_heads, head0, n_heads)


def _hybrid_mixer(h, bsz, seq, layer, w_in_t, lat_g, w_uk, w_uv, c_ln_g, c_ln_b, c_w_s, c_b_s,
                  d_conv, w_branch, w_out, bias_a, bias_b):
    t, d = h.shape
    width = d // N_BRANCH
    n_lat = w_uk.shape[0]
    a_heads = width // HEAD_DIM
    n_groups = len(B_GROUPS)
    iq_width = IDX_HEADS * IDX_DIM
    lat_col = width
    iq_col = lat_col + n_lat
    ik_col = iq_col + iq_width
    front = ik_col
    shift = IDX_DIM + IDX_HEADS
    qkv_w = 3 * n_groups * width
    rest_w = qkv_w + 2 * width + 3 * width + N_BRANCH * d
    assert front % 512 == 0 and w_in_t.shape[1] == front + shift + rest_w and shift <= LANE

    pa = _matmul_t(h, w_in_t, layer, row0=0, n_out=front, tm=1024, name="proj_front")
    pik = _matmul_t(h, w_in_t, layer, row0=front, n_out=LANE, name="proj_index_key")
    gate_col = qkv_w + 5 * width
    rest = _matmul_t(h, w_in_t, layer, row0=front + shift, n_out=gate_col, tm=1024,
                     name="proj_rest")
    gates = _matmul_t(h, w_in_t, layer, row0=front + shift + gate_col, n_out=N_BRANCH * d,
                      tm=1024, out_dtype=BF16, name="proj_gates")

    pa3 = pa.reshape(bsz, seq, front)
    pik3 = pik.reshape(bsz, seq, LANE)
    rest3 = rest.reshape(bsz, seq, gate_col)

    lat, latt, ikt = _latnorm(pa3, pik3, lat_g, lat_col, n_lat)
    wuk_t = jnp.transpose(w_uk, (1, 2, 0)).astype(BF16)
    wuv_t = jnp.transpose(w_uv, (1, 0, 2)).astype(BF16)
    o_a = _dsa(pa3, pik3, lat, latt, ikt, wuk_t, wuv_t, bias_a,
               q_width=width, iq_col=iq_col, iq_width=iq_width,
               n_variants=DSA_KEY_RANGE_VARIANTS).reshape(t, width)

    outs, lses = [], []
    for g, (window, dil) in enumerate(B_GROUPS):
        o, l = _dilated(rest3, bias_b, g, n_groups, window, dil, width)
        outs.append(o)
        lses.append(l)
    o_b = _mixture(outs, lses)

    o_c = _gmlp(rest, qkv_w, width, c_ln_g, c_ln_b, c_w_s, c_b_s.T)
    o_d = _short_conv(rest3, qkv_w + 2 * width, width, d_conv).reshape(t, width)

    mixed = _merge([o_a, o_b, o_c, o_d], gates, w_branch, layer)
    return _matmul(mixed, w_out, layer, tm=1024, name="proj_out")


def kernel(x, c, w_ada, b_ada, ada_table, rel_bias, norm_pre_mix, norm_post_mix, norm_pre_ffn,
           norm_post_ffn, w_in, a_lat_norm, a_w_uk, a_w_uv, c_ln_g, c_ln_b, c_w_s, c_b_s, d_conv,
           w_branch, w_out, ffn_w_gate, ffn_w_up, ffn_w_down, moe_router, moe_w_gate, moe_w_up,
           moe_w_down):
    bsz, seq, d = x.shape
    depth = w_in.shape[0]
    t = bsz * seq

    c_pad = jnp.pad(c, ((0, 16 - bsz % 16 if bsz % 16 else 0), (0, 0)))
    mod_shared = _ada(c_pad, w_ada, b_ada)[:bsz].reshape(bsz, N_MOD, d)
    mods = [mod_shared + ada_table[layer] for layer in range(depth)]

    w_in_t = jnp.transpose(w_in, (0, 2, 1))
    width = d // N_BRANCH
    bias_a = _bias_tiles_a(rel_bias, width // HEAD_DIM, seq)
    bias_b = _bias_tiles_b(rel_bias, width // HEAD_DIM, width // HEAD_DIM)

    h = _prenorm(x, mods[0], norm_pre_mix[0])
    for layer in range(depth):
        mod = mods[layer]
        y = _hybrid_mixer(h.reshape(t, d), bsz, seq, layer, w_in_t, a_lat_norm[layer],
                          a_w_uk[layer], a_w_uv[layer], c_ln_g[layer], c_ln_b[layer],
                          c_w_s[layer], c_b_s[layer], d_conv[layer], w_branch, w_out,
                          bias_a, bias_b)
        j = layer // 2
        dense = layer % 2 == 0
        res = _postnorm(x, y.reshape(bsz, seq, d), mod, norm_post_mix[layer], 2, nxt=(3, 4),
                        mod_next=mod, g_pre=norm_pre_ffn[layer],
                        router_t=None if dense else moe_router[j].T)
        if dense:
            x, h = res
            a = _swiglu_up(h.reshape(t, d), ffn_w_gate[j][None], ffn_w_up[j][None])
            y = _swiglu_down(a, ffn_w_down, j)
        else:
            x, hp, route, rank, counts = res
            y = _moe(hp.reshape(t, d // 2), route.reshape(t, LANE), rank.reshape(t, LANE), counts,
                     j, moe_w_gate, moe_w_up, moe_w_down)
        y3 = y.reshape(bsz, seq, d)
        if layer + 1 < depth:
            x, h = _postnorm(x, y3, mod, norm_post_ffn[layer], 5, nxt=(0, 1),
                             mod_next=mods[layer + 1], g_pre=norm_pre_mix[layer + 1])
        else:
            (x,) = _postnorm(x, y3, mod, norm_post_ffn[layer], 5)
    return x
```

```python
import functools
import math

import jax
import jax.numpy as jnp
from jax import lax
from jax.experimental import pallas as pl
from jax.experimental.pallas import tpu as pltpu

F32 = jnp.float32
BF16 = jnp.bfloat16

LANE = 128
V7X_VMEM_BYTES = 64 * 1024 * 1024
VMEM_CAP = V7X_VMEM_BYTES - 8 * 1024 * 1024

HEAD_DIM = 128
N_BRANCH = 4
IDX_HEADS = 8
IDX_DIM = 64
IDX_TOPK = 256
A_BLOCK = 128
DSA_KEY_RANGE_VARIANTS = 4
B_GROUPS = ((128, 1), (512, 4), (2048, 16))
B_BLOCK = 128
C_CHUNK = 128
C_GROUPS = 8
D_CONV_WIDTH = 3
REL_BUCKETS = 32
REL_MAX_DIST = 2048
TOP_K = 2
N_MOD = 6
EPS = 1e-6
NEG_INF = float("-inf")


def _params(semantics, vmem_bytes):
    limit = int(min(VMEM_CAP, max(vmem_bytes * 5 // 4 + (4 << 20), 16 << 20)))
    return pltpu.CompilerParams(dimension_semantics=semantics, vmem_limit_bytes=limit)


def _pick_tile(n, candidates):
    for c in candidates:
        if n % c == 0:
            return c
    raise ValueError(f"no tile in {candidates} divides {n}")


def _cast_rows(src_fn, dst_ref, n_rows, chunk):
    def body(r, carry):
        rows = pl.ds(pl.multiple_of(r * chunk, chunk), chunk)
        dst_ref[rows, :] = src_fn(rows).astype(BF16)
        return carry
    lax.fori_loop(0, n_rows // chunk, body, 0)


def _mm_body(chunk, lhs_ref, w_ref, o_ref, wb_ref):
    @pl.when(pl.program_id(1) == 0)
    def _():
        _cast_rows(lambda rows: w_ref[rows, :], wb_ref, w_ref.shape[0], chunk)

    o_ref[...] = jnp.dot(lhs_ref[...], wb_ref[...],
                         preferred_element_type=F32).astype(o_ref.dtype)


def _matmul(lhs, w, layer, *, tn=512, tm=512, out_dtype=F32, name="mm"):
    m, k = lhs.shape
    n_out = w.shape[2]
    tn = min(tn, n_out)
    tm = min(tm, m)
    assert n_out % tn == 0 and m % tm == 0 and tn % LANE == 0
    chunk = _pick_tile(k, (256, 128, 64, 8))
    osz = jnp.dtype(out_dtype).itemsize
    vmem = 2 * tm * k * 2 + 2 * k * tn * 4 + k * tn * 2 + 2 * tm * tn * osz + tm * tn * 4 \
        + chunk * tn * 8
    return pl.pallas_call(
        functools.partial(_mm_body, chunk),
        out_shape=jax.ShapeDtypeStruct((m, n_out), out_dtype),
        grid=(n_out // tn, m // tm),
        in_specs=[pl.BlockSpec((tm, k), lambda j, i: (i, 0)),
                  pl.BlockSpec((None, k, tn), lambda j, i: (layer, 0, j))],
        out_specs=pl.BlockSpec((tm, tn), lambda j, i: (i, j)),
        scratch_shapes=[pltpu.VMEM((k, tn), BF16)],
        compiler_params=_params(("arbitrary", "arbitrary"), vmem),
        name=name,
    )(lhs, w)


def _mm_t_body(chunk, lhs_ref, w_ref, o_ref, wb_ref):
    @pl.when(pl.program_id(1) == 0)
    def _():
        _cast_rows(lambda rows: w_ref[rows, :], wb_ref, w_ref.shape[0], chunk)

    acc = lax.dot_general(lhs_ref[...], wb_ref[...], (((1,), (1,)), ((), ())),
                          preferred_element_type=F32)
    o_ref[...] = acc.astype(o_ref.dtype)


def _matmul_t(lhs, w_t, layer, *, row0, n_out, tn=512, tm=512, out_dtype=F32, name="mm_t"):
    m, k = lhs.shape
    tn = min(tn, n_out)
    tm = min(tm, m)
    assert n_out % tn == 0 and m % tm == 0 and row0 % 8 == 0 and tn % LANE == 0
    chunk = _pick_tile(tn, (256, 128))
    osz = jnp.dtype(out_dtype).itemsize
    vmem = 2 * tm * k * 2 + 2 * k * tn * 4 + k * tn * 2 + 2 * tm * tn * osz + tm * tn * 4 \
        + 4 * chunk * k * 4
    return pl.pallas_call(
        functools.partial(_mm_t_body, chunk),
        out_shape=jax.ShapeDtypeStruct((m, n_out), out_dtype),
        grid=(n_out // tn, m // tm),
        in_specs=[pl.BlockSpec((tm, k), lambda j, i: (i, 0)),
                  pl.BlockSpec((None, pl.Element(tn), pl.Element(k)),
                               lambda j, i: (layer, pl.multiple_of(row0 + j * tn, 8), 0))],
        out_specs=pl.BlockSpec((tm, tn), lambda j, i: (i, j)),
        scratch_shapes=[pltpu.VMEM((tn, k), BF16)],
        compiler_params=_params(("arbitrary", "arbitrary"), vmem),
        name=name,
    )(lhs, w_t)


def _ada_body(c_ref, w_ref, b_ref, o_ref):
    c = c_ref[...]
    a = (c * jax.nn.sigmoid(c)).astype(BF16)
    o_ref[...] = jnp.dot(a, w_ref[...].astype(BF16), preferred_element_type=F32) + b_ref[...]


def _ada(c_pad, w_ada, b_ada, tn=512):
    m, k = c_pad.shape
    n = w_ada.shape[1]
    tn = _pick_tile(n, (tn, 256, 128))
    vmem = 2 * k * tn * 4 + k * tn * 2 + 4 * m * k * 4
    return pl.pallas_call(
        _ada_body,
        out_shape=jax.ShapeDtypeStruct((m, n), F32),
        grid=(n // tn,),
        in_specs=[pl.BlockSpec((m, k), lambda j: (0, 0)),
                  pl.BlockSpec((k, tn), lambda j: (0, j)),
                  pl.BlockSpec((1, tn), lambda j: (0, j))],
        out_specs=pl.BlockSpec((m, tn), lambda j: (0, j)),
        compiler_params=_params(("arbitrary",), vmem),
        name="ada",
    )(c_pad, w_ada, b_ada.reshape(1, n))


def _rms(x, g):
    return x * lax.rsqrt(jnp.mean(x * x, axis=-1, keepdims=True) + EPS) * g


def _route(h, router_ref, n_experts):
    rows = h.shape[0]
    lane = lax.broadcasted_iota(jnp.int32, (rows, LANE), 1)
    logits = jnp.full((rows, LANE), NEG_INF, F32)
    for e in range(n_experts):
        le = jnp.sum(h * router_ref[e:e + 1, :], axis=-1, keepdims=True)
        logits = jnp.where(lane == e, le, logits)
    m1 = jnp.max(logits, axis=-1, keepdims=True)
    i1 = jnp.min(jnp.where(logits == m1, lane, LANE), axis=-1, keepdims=True)
    rest = jnp.where(lane == i1, NEG_INF, logits)
    m2 = jnp.max(rest, axis=-1, keepdims=True)
    i2 = jnp.min(jnp.where(rest == m2, lane, LANE), axis=-1, keepdims=True)
    e2 = jnp.exp(m2 - m1)
    den = 1.0 + e2
    member = jnp.where(lane == i1, 1.0, jnp.where(lane == i2, 1.0, 0.0))
    extra = jnp.where(lane == n_experts, i1.astype(F32),
                      jnp.where(lane == n_experts + 1, i2.astype(F32),
                                jnp.where(lane == n_experts + 2, 1.0 / den, e2 / den)))
    return jnp.where(lane < n_experts, member, jnp.where(lane < n_experts + 4, extra, 0.0))


def _pack_bf16_pairs(h):
    half = h.shape[1] // 2
    lo = pltpu.bitcast(h[:, :half].astype(BF16).astype(F32), jnp.uint32)
    hi = pltpu.bitcast(h[:, half:].astype(BF16).astype(F32), jnp.uint32)
    return (hi & jnp.uint32(0xFFFF0000)) | (lo >> 16)


def _unpack_bf16_pairs(u):
    lo = pltpu.bitcast(u << 16, F32).astype(BF16)
    hi = pltpu.bitcast(u & jnp.uint32(0xFFFF0000), F32).astype(BF16)
    return lo, hi


def _pre_body(x_ref, mod_ref, g_ref, h_ref):
    h = _rms(x_ref[...], g_ref[...]) * (1.0 + mod_ref[1:2, :]) + mod_ref[0:1, :]
    h_ref[...] = h.astype(h_ref.dtype)


def _prenorm(x3, mod, g, tr=256):
    b, s, d = x3.shape
    tr = min(tr, s)
    vmem = 2 * tr * d * 4 + 2 * tr * d * 2 + 4 * tr * d * 4
    return pl.pallas_call(
        _pre_body,
        out_shape=jax.ShapeDtypeStruct((b, s, d), BF16),
        grid=(b, s // tr),
        in_specs=[pl.BlockSpec((None, tr, d), lambda bi, i: (bi, i, 0)),
                  pl.BlockSpec((None, N_MOD, d), lambda bi, i: (bi, 0, 0)),
                  pl.BlockSpec((1, d), lambda bi, i: (0, 0))],
        out_specs=pl.BlockSpec((None, tr, d), lambda bi, i: (bi, i, 0)),
        compiler_params=_params(("arbitrary", "arbitrary"), vmem),
        name="prenorm",
    )(x3, mod, g.reshape(1, d))


def _post_body(gate_row, nxt, n_experts, *refs):
    x_ref, y_ref, mod_ref, gpost_ref = refs[:4]
    pos = 4
    if nxt is not None:
        modn_ref, gpre_ref = refs[pos:pos + 2]
        pos += 2
    if n_experts:
        router_ref = refs[pos]
        pos += 1
    xo_ref = refs[pos]
    pos += 1
    xn = x_ref[...] + mod_ref[gate_row:gate_row + 1, :] * _rms(y_ref[...], gpost_ref[...])
    xo_ref[...] = xn
    if nxt is not None:
        shift_row, scale_row = nxt
        h = _rms(xn, gpre_ref[...]) * (1.0 + modn_ref[scale_row:scale_row + 1, :]) \
            + modn_ref[shift_row:shift_row + 1, :]
        if not n_experts:
            refs[pos][...] = h.astype(BF16)
            return
        hp_ref, route_ref, rank_ref, counts_ref, carry_ref = refs[pos:pos + 5]
        hp_ref[...] = _pack_bf16_pairs(h)
        route = _route(h, router_ref, n_experts)
        route_ref[...] = route

        @pl.when((pl.program_id(0) == 0) & (pl.program_id(1) == 0))
        def _():
            carry_ref[...] = jnp.zeros_like(carry_ref)

        rows = route.shape[0]
        lane = lax.broadcasted_iota(jnp.int32, route.shape, 1)
        member = jnp.where(lane < n_experts, route, 0.0)
        ri = lax.broadcasted_iota(jnp.int32, (rows, rows), 0)
        ci = lax.broadcasted_iota(jnp.int32, (rows, rows), 1)
        earlier = jnp.where(ci < ri, 1.0, 0.0).astype(BF16)
        within = jnp.dot(earlier, member.astype(BF16), preferred_element_type=F32)
        rank_ref[...] = within + carry_ref[0:1, :]
        carry_ref[...] = carry_ref[...] + jnp.sum(member, axis=0, keepdims=True)
        counts_ref[...] = carry_ref[...]


def _postnorm(x3, y3, mod, g_post, gate_row, *, nxt=None, mod_next=None, g_pre=None,
              router_t=None, tr=256):
    b, s, d = x3.shape
    tr = min(tr, s)
    n_experts = 0 if router_t is None else router_t.shape[0]
    row = lambda bi, i: (bi, i, 0)
    mod_spec = pl.BlockSpec((None, N_MOD, d), lambda bi, i: (bi, 0, 0))
    in_specs = [pl.BlockSpec((None, tr, d), row), pl.BlockSpec((None, tr, d), row), mod_spec,
                pl.BlockSpec((1, d), lambda bi, i: (0, 0))]
    args = [x3, y3, mod, g_post.reshape(1, d)]
    out_shape = [jax.ShapeDtypeStruct((b, s, d), F32)]
    out_specs = [pl.BlockSpec((None, tr, d), row)]
    scratch = []
    if nxt is not None:
        in_specs += [mod_spec, pl.BlockSpec((1, d), lambda bi, i: (0, 0))]
        args += [mod_next, g_pre.reshape(1, d)]
    if nxt is not None and not n_experts:
        out_shape.append(jax.ShapeDtypeStruct((b, s, d), BF16))
        out_specs.append(pl.BlockSpec((None, tr, d), row))
    if n_experts:
        assert nxt is not None and n_experts + 4 <= LANE
        in_specs.append(pl.BlockSpec((n_experts, d), lambda bi, i: (0, 0)))
        args.append(router_t)
        out_shape += [jax.ShapeDtypeStruct((b, s, d // 2), jnp.uint32),
                      jax.ShapeDtypeStruct((b, s, LANE), F32),
                      jax.ShapeDtypeStruct((b, s, LANE), F32),
                      jax.ShapeDtypeStruct((8, LANE), F32)]
        out_specs += [pl.BlockSpec((None, tr, d // 2), row), pl.BlockSpec((None, tr, LANE), row),
                      pl.BlockSpec((None, tr, LANE), row),
                      pl.BlockSpec((8, LANE), lambda bi, i: (0, 0))]
        scratch.append(pltpu.VMEM((8, LANE), F32))
    vmem = 2 * 3 * tr * d * 4 + 2 * tr * d * 2 + 6 * tr * d * 4
    return pl.pallas_call(
        functools.partial(_post_body, gate_row, nxt, n_experts),
        out_shape=out_shape,
        grid=(b, s // tr),
        in_specs=in_specs,
        out_specs=out_specs,
        scratch_shapes=scratch,
        compiler_params=_params(("arbitrary", "arbitrary"), vmem),
        name="postnorm",
    )(*args)


def _lat_body(alat_ref, pik_ref, g_ref, lat_ref, latt_ref, ikt_ref):
    lat = _rms(alat_ref[...], g_ref[...])
    lat_ref[...] = lat.astype(BF16)
    latt_ref[...] = lat.T.astype(BF16)
    ikt_ref[...] = pik_ref[...].T.astype(BF16)


def _latnorm(pa3, pik3, g_lat, lat_col, n_lat, tr=512):
    b, s, _ = pa3.shape
    tr = min(tr, s)
    assert lat_col % n_lat == 0
    vmem = 2 * tr * (n_lat + LANE) * 4 + 4 * tr * (n_lat + LANE) * 2 + 4 * tr * n_lat * 4
    return pl.pallas_call(
        _lat_body,
        out_shape=[jax.ShapeDtypeStruct((b, s, n_lat), BF16),
                   jax.ShapeDtypeStruct((b, n_lat, s), BF16),
                   jax.ShapeDtypeStruct((b, LANE, s), BF16)],
        grid=(b, s // tr),
        in_specs=[pl.BlockSpec((None, tr, n_lat), lambda bi, i: (bi, i, lat_col // n_lat)),
                  pl.BlockSpec((None, tr, LANE), lambda bi, i: (bi, i, 0)),
                  pl.BlockSpec((1, n_lat), lambda bi, i: (0, 0))],
        out_specs=[pl.BlockSpec((None, tr, n_lat), lambda bi, i: (bi, i, 0)),
                   pl.BlockSpec((None, n_lat, tr), lambda bi, i: (bi, 0, i)),
                   pl.BlockSpec((None, LANE, tr), lambda bi, i: (bi, 0, i))],
        compiler_params=_params(("arbitrary", "arbitrary"), vmem),
        name="latnorm",
    )(pa3, pik3, g_lat.reshape(1, n_lat))


def _dsa_body(topk, n_variants, *refs):
    i = pl.program_id(1)
    n_blk = refs[4].shape[0] // A_BLOCK
    per = -(-n_blk // n_variants)
    for v in range(n_variants):
        blocks = min(n_blk, (v + 1) * per)

        @pl.when((i >= v * per) & (i < (v + 1) * per))
        def _(blocks=blocks):
            _dsa_compute(topk, blocks * A_BLOCK, i, *refs)


def _dsa_compute(topk, seq, i, q_ref, iq_ref, iw_ref, ikt_ref, lat_ref, latt_ref, wuk_ref, wuv_ref,
                 bias_ref, o_ref, madd_ref):
    n_blk = seq // A_BLOCK
    n_heads = wuk_ref.shape[0]
    int_min = jnp.int32(-2 ** 31)
    qpos = i * A_BLOCK + lax.broadcasted_iota(jnp.int32, (A_BLOCK, 1), 0)
    kpos = lax.broadcasted_iota(jnp.int32, (A_BLOCK, seq), 1)

    iq = iq_ref[...].astype(BF16)
    iw = iw_ref[...]
    ikt = ikt_ref[0:IDX_DIM, 0:seq]
    score = jnp.zeros((A_BLOCK, seq), F32)
    for h in range(IDX_HEADS):
        l = jnp.dot(iq[:, h * IDX_DIM:(h + 1) * IDX_DIM], ikt, preferred_element_type=F32)
        score = score + iw[:, IDX_DIM + h:IDX_DIM + h + 1] * jnp.maximum(l, 0.0)

    score = jnp.where(score == 0.0, 0.0, score)
    key = pltpu.bitcast(score, jnp.int32)
    key = jnp.where(key < 0, key ^ jnp.int32(0x7FFFFFFF), key)
    key = jnp.where(kpos <= qpos, key, int_min)
    k_eff = jnp.minimum(qpos + 1, topk).astype(F32)

    def count(mask):
        return jnp.sum(jnp.where(mask, 1.0, 0.0), axis=1, keepdims=True)

    t0 = jnp.where(count(key >= 0) >= k_eff, jnp.int32(0), int_min)

    def value_step(it, t):
        c = t | (jnp.int32(1) << (30 - it))
        return jnp.where(count(key >= c) >= k_eff, c, t)

    t = lax.fori_loop(0, 31, value_step, t0)

    need = k_eff - count(key > t)
    eq = key == t
    partial_tie = jnp.max(count(eq) - need) > 0.0

    @pl.when(jnp.logical_not(partial_tie))
    def _():
        madd_ref[:, 0:seq] = jnp.where(key >= t, 0.0, NEG_INF)

    @pl.when(partial_tie)
    def _():
        idx_bits = max(1, (seq - 1).bit_length())

        def index_step(it, p):
            c = p | (jnp.int32(1) << (idx_bits - 1 - it))
            below = jnp.sum(jnp.where(eq, jnp.where(kpos < c, 1.0, 0.0), 0.0),
                            axis=1, keepdims=True)
            return jnp.where(below < need, c, p)

        p = lax.fori_loop(0, idx_bits, index_step, jnp.zeros((A_BLOCK, 1), jnp.int32))
        tie_add = jnp.where(eq, jnp.where(kpos <= p, 0.0, NEG_INF), NEG_INF)
        madd_ref[:, 0:seq] = jnp.where(key > t, 0.0, tie_add)

    scale = HEAD_DIM ** -0.5
    for h in range(n_heads):
        cols = slice(h * HEAD_DIM, (h + 1) * HEAD_DIM)
        qa = jnp.dot(q_ref[:, cols].astype(BF16), wuk_ref[h],
                     preferred_element_type=F32).astype(BF16)
        lg = jnp.dot(qa, latt_ref[:, 0:seq], preferred_element_type=F32) * scale
        bias = jnp.concatenate([bias_ref[jnp.maximum(i - j, 0), h] for j in range(n_blk)], axis=1)
        lg = lg + bias + madd_ref[:, 0:seq]
        m = jnp.max(lg, axis=1, keepdims=True)
        e = jnp.exp(lg - m)
        den = jnp.sum(e, axis=1, keepdims=True)
        o_lat = jnp.dot(e.astype(BF16), lat_ref[0:seq, :], preferred_element_type=F32) / den
        o_ref[:, cols] = jnp.dot(o_lat.astype(BF16), wuv_ref[h],
                                 preferred_element_type=F32).astype(o_ref.dtype)


def _dsa(pa3, pik3, lat, latt, ikt, wuk_t, wuv_t, bias_a, *, q_width, iq_col, iq_width,
         n_variants):
    b, s, _ = pa3.shape
    n_lat = lat.shape[-1]
    n_heads = wuk_t.shape[0]
    n_blk = s // A_BLOCK
    topk = min(IDX_TOPK, s // 4)
    assert iq_col % iq_width == 0
    vmem = 2 * (A_BLOCK * (q_width + iq_width + LANE) * 4 + LANE * s * 2 + 2 * s * n_lat * 2
                + 2 * n_heads * HEAD_DIM * n_lat * 2 + n_heads * n_blk * A_BLOCK * A_BLOCK * 4
                + A_BLOCK * q_width * 2) + 10 * A_BLOCK * s * 4
    return pl.pallas_call(
        functools.partial(_dsa_body, topk, n_variants),
        out_shape=jax.ShapeDtypeStruct((b, s, q_width), BF16),
        grid=(b, n_blk),
        in_specs=[pl.BlockSpec((None, A_BLOCK, q_width), lambda bi, i: (bi, i, 0)),
                  pl.BlockSpec((None, A_BLOCK, iq_width), lambda bi, i: (bi, i, iq_col // iq_width)),
                  pl.BlockSpec((None, A_BLOCK, LANE), lambda bi, i: (bi, i, 0)),
                  pl.BlockSpec((None, LANE, s), lambda bi, i: (bi, 0, 0)),
                  pl.BlockSpec((None, s, n_lat), lambda bi, i: (bi, 0, 0)),
                  pl.BlockSpec((None, n_lat, s), lambda bi, i: (bi, 0, 0)),
                  pl.BlockSpec(wuk_t.shape, lambda bi, i: (0, 0, 0)),
                  pl.BlockSpec(wuv_t.shape, lambda bi, i: (0, 0, 0)),
                  pl.BlockSpec(bias_a.shape, lambda bi, i: (0, 0, 0, 0))],
        out_specs=pl.BlockSpec((None, A_BLOCK, q_width), lambda bi, i: (bi, i, 0)),
        scratch_shapes=[pltpu.VMEM((A_BLOCK, s), F32)],
        compiler_params=_params(("arbitrary", "arbitrary"), vmem),
        name="dsa",
    )(pa3, pa3, pik3, ikt, lat, latt, wuk_t, wuv_t, bias_a)


def _dil_body(span, dil, hb, has_prev, *refs):
    if has_prev:
        q_ref, kp_ref, kc_ref, vp_ref, vc_ref, bias_ref, o_ref, lse_ref = refs
    else:
        q_ref, kc_ref, vc_ref, bias_ref, o_ref, lse_ref = refs
    i = pl.program_id(1)
    hblk = pl.program_id(2)
    n_keys = 2 * B_BLOCK if has_prev else B_BLOCK
    key0 = 0 if has_prev else B_BLOCK
    qi = lax.broadcasted_iota(jnp.int32, (B_BLOCK, n_keys), 0)
    kj = lax.broadcasted_iota(jnp.int32, (B_BLOCK, n_keys), 1) + key0
    sub_dist = qi + B_BLOCK - kj
    in_band = jnp.where(sub_dist <= span,
                        jnp.where((i - 1) * B_BLOCK + kj >= 0, 0.0, NEG_INF), NEG_INF)
    madd = jnp.where(sub_dist >= 0, in_band, NEG_INF)
    lane = lax.broadcasted_iota(jnp.int32, (B_BLOCK, LANE), 1)
    scale = HEAD_DIM ** -0.5

    @pl.when(hblk == 0)
    def _():
        lse_ref[...] = jnp.zeros_like(lse_ref)

    for r in range(dil):
        rows = pl.ds(r, B_BLOCK, stride=dil) if dil > 1 else slice(None)
        lse_acc = lse_ref[rows, :]
        for h in range(hb):
            cols = slice(h * HEAD_DIM, (h + 1) * HEAD_DIM)
            q = q_ref[rows, cols].astype(BF16)
            if has_prev:
                k2 = jnp.concatenate([kp_ref[rows, cols], kc_ref[rows, cols]], axis=0).astype(BF16)
                v2 = jnp.concatenate([vp_ref[rows, cols], vc_ref[rows, cols]], axis=0).astype(BF16)
            else:
                k2 = kc_ref[rows, cols].astype(BF16)
                v2 = vc_ref[rows, cols].astype(BF16)
            lg = lax.dot_general(q, k2, (((1,), (1,)), ((), ())), preferred_element_type=F32)
            lg = lg * scale + bias_ref[h, :, key0:] + madd
            m = jnp.max(lg, axis=1, keepdims=True)
            e = jnp.exp(lg - m)
            den = jnp.sum(e, axis=1, keepdims=True)
            o_ref[rows, cols] = jnp.dot((e / den).astype(BF16), v2, preferred_element_type=F32)
            lse_acc = jnp.where(lane == hblk * hb + h, m + jnp.log(den), lse_acc)
        lse_ref[rows, :] = lse_acc


def _dilated(rest3, bias_b, g, n_groups, window, dil, width):
    b, s, w_all = rest3.shape
    n_heads = width // HEAD_DIM
    span = window // dil
    n_sub = s // dil
    assert span == B_BLOCK and n_sub % B_BLOCK == 0 and w_all % width == 0
    nb = n_sub // B_BLOCK
    rows = B_BLOCK * dil
    hb = n_heads if dil == 1 else 1
    assert HEAD_DIM == LANE
    n_hblk = n_heads // hb
    has_prev = nb > 1
    cur = lambda slot: (lambda bi, i, hk: (bi, i, (slot * n_groups + g) * n_hblk + hk))
    prev = lambda slot: (lambda bi, i, hk: (bi, jnp.maximum(i - 1, 0),
                                            (slot * n_groups + g) * n_hblk + hk))
    blk = (None, rows, hb * HEAD_DIM)
    in_specs = [pl.BlockSpec(blk, cur(0))]
    for slot in (1, 2):
        if has_prev:
            in_specs.append(pl.BlockSpec(blk, prev(slot)))
        in_specs.append(pl.BlockSpec(blk, cur(slot)))
    in_specs.append(pl.BlockSpec((None, hb, B_BLOCK, 2 * B_BLOCK), lambda bi, i, hk: (g, hk, 0, 0)))
    vmem = 2 * (len(in_specs) * rows * hb * HEAD_DIM + rows * LANE) * 4 \
        + 2 * hb * B_BLOCK * 2 * B_BLOCK * 4 + (4 << 20)
    o, lse = pl.pallas_call(
        functools.partial(_dil_body, span, dil, hb, has_prev),
        out_shape=[jax.ShapeDtypeStruct((b, s, width), F32),
                   jax.ShapeDtypeStruct((b, s, LANE), F32)],
        grid=(b, nb, n_hblk),
        in_specs=in_specs,
        out_specs=[pl.BlockSpec(blk, lambda bi, i, hk: (bi, i, hk)),
                   pl.BlockSpec((None, rows, LANE), lambda bi, i, hk: (bi, i, 0))],
        compiler_params=_params(("arbitrary", "arbitrary", "arbitrary"), vmem),
        name=f"dilated{g}",
    )(*([rest3] * (len(in_specs) - 1)), bias_b)
    return o.reshape(b * s, width), lse.reshape(b * s, LANE)


def _mix_body(n_groups, n_heads, *refs):
    o_refs = refs[:n_groups]
    l_refs = refs[n_groups:2 * n_groups]
    out_ref = refs[2 * n_groups]
    ls = [r[...] for r in l_refs]
    m = functools.reduce(jnp.maximum, ls)
    es = [jnp.exp(l - m) for l in ls]
    tot = functools.reduce(jnp.add, es)
    ws = [e / tot for e in es]
    for h in range(n_heads):
        cols = slice(h * HEAD_DIM, (h + 1) * HEAD_DIM)
        acc = ws[0][:, h:h + 1] * o_refs[0][:, cols]
        for gi in range(1, n_groups):
            acc = acc + ws[gi][:, h:h + 1] * o_refs[gi][:, cols]
        out_ref[:, cols] = acc.astype(out_ref.dtype)


def _mixture(outs, lses, tr=512):
    t, width = outs[0].shape
    tr = min(tr, t)
    n_groups = len(outs)
    vmem = 2 * n_groups * tr * (width + LANE) * 4 + 2 * tr * width * 2 + 4 * tr * width * 4
    return pl.pallas_call(
        functools.partial(_mix_body, n_groups, width // HEAD_DIM),
        out_shape=jax.ShapeDtypeStruct((t, width), BF16),
        grid=(t // tr,),
        in_specs=[pl.BlockSpec((tr, width), lambda i: (i, 0))] * n_groups
        + [pl.BlockSpec((tr, LANE), lambda i: (i, 0))] * n_groups,
        out_specs=pl.BlockSpec((tr, width), lambda i: (i, 0)),
        compiler_params=_params(("arbitrary",), vmem),
        name="mixture",
    )(*outs, *lses)


def _gmlp_body(u_ref, v_ref, g_ref, b_ref, ws_ref, bs_ref, o_ref):
    u = jax.nn.gelu(u_ref[...])
    v = jax.nn.gelu(v_ref[...])
    mu = jnp.mean(v, axis=-1, keepdims=True)
    var = jnp.mean(jnp.square(v - mu), axis=-1, keepdims=True)
    v = (v - mu) * lax.rsqrt(var + EPS) * g_ref[...] + b_ref[...]
    ti = lax.broadcasted_iota(jnp.int32, (C_CHUNK, C_CHUNK), 0)
    si = lax.broadcasted_iota(jnp.int32, (C_CHUNK, C_CHUNK), 1)
    causal = si <= ti
    gw = u.shape[1] // C_GROUPS
    for g in range(C_GROUPS):
        cols = slice(g * gw, (g + 1) * gw)
        w = jnp.where(causal, ws_ref[g], 0.0).astype(BF16)
        mixed = jnp.dot(w, v[:, cols].astype(BF16), preferred_element_type=F32) + bs_ref[:, g:g + 1]
        o_ref[:, cols] = (u[:, cols] * mixed).astype(o_ref.dtype)


def _gmlp(rest, u_col, width, ln_g, ln_b, w_s, b_s_t):
    t = rest.shape[0]
    assert u_col % width == 0 and (width // C_GROUPS) % LANE == 0
    vmem = 2 * 2 * C_CHUNK * width * 4 + 2 * C_CHUNK * width * 2 + 2 * C_GROUPS * C_CHUNK * C_CHUNK * 4 \
        + 8 * C_CHUNK * width * 4
    return pl.pallas_call(
        _gmlp_body,
        out_shape=jax.ShapeDtypeStruct((t, width), BF16),
        grid=(t // C_CHUNK,),
        in_specs=[pl.BlockSpec((C_CHUNK, width), lambda i: (i, u_col // width)),
                  pl.BlockSpec((C_CHUNK, width), lambda i: (i, u_col // width + 1)),
                  pl.BlockSpec((1, width), lambda i: (0, 0)),
                  pl.BlockSpec((1, width), lambda i: (0, 0)),
                  pl.BlockSpec(w_s.shape, lambda i: (0, 0, 0)),
                  pl.BlockSpec(b_s_t.shape, lambda i: (0, 0))],
        out_specs=pl.BlockSpec((C_CHUNK, width), lambda i: (i, 0)),
        compiler_params=_params(("arbitrary",), vmem),
        name="gmlp",
    )(rest, rest, ln_g.reshape(1, width), ln_b.reshape(1, width), w_s, b_s_t)


def _conv_body(h_ref, bg_ref, cg_ref, w_ref, o_ref, carry_ref):
    @pl.when(pl.program_id(1) == 0)
    def _():
        carry_ref[...] = jnp.zeros_like(carry_ref)

    ch = cg_ref[...] * h_ref[...]
    rows = ch.shape[0]
    row = lax.broadcasted_iota(jnp.int32, ch.shape, 0)
    pm1 = carry_ref[7:8, :]
    pm2 = carry_ref[6:7, :]
    s1 = jnp.where(row == 0, pm1, pltpu.roll(ch, 1, 0))
    s2 = jnp.where(row == 0, pm2, jnp.where(row == 1, pm1, pltpu.roll(ch, 2, 0)))
    z = w_ref[0:1, :] * s2 + w_ref[1:2, :] * s1 + w_ref[2:3, :] * ch
    o_ref[...] = (bg_ref[...] * z).astype(o_ref.dtype)
    carry_ref[...] = ch[rows - 8:, :]


def _short_conv(rest3, h_col, width, conv_w, tr=256):
    b, s, _ = rest3.shape
    tr = min(tr, s)
    assert h_col % width == 0 and conv_w.shape[0] == D_CONV_WIDTH
    cb = h_col // width
    vmem = 2 * 3 * tr * width * 4 + 2 * tr * width * 2 + 6 * tr * width * 4
    return pl.pallas_call(
        _conv_body,
        out_shape=jax.ShapeDtypeStruct((b, s, width), BF16),
        grid=(b, s // tr),
        in_specs=[pl.BlockSpec((None, tr, width), lambda bi, i: (bi, i, cb)),
                  pl.BlockSpec((None, tr, width), lambda bi, i: (bi, i, cb + 1)),
                  pl.BlockSpec((None, tr, width), lambda bi, i: (bi, i, cb + 2)),
                  pl.BlockSpec(conv_w.shape, lambda bi, i: (0, 0))],
        out_specs=pl.BlockSpec((None, tr, width), lambda bi, i: (bi, i, 0)),
        scratch_shapes=[pltpu.VMEM((8, width), F32)],
        compiler_params=_params(("arbitrary", "arbitrary"), vmem),
        name="short_conv",
    )(rest3, rest3, rest3, conv_w)


def _merge_body(chunk, *refs):
    br_refs = refs[:N_BRANCH]
    gate_refs = refs[N_BRANCH:2 * N_BRANCH]
    w_ref, o_ref, wb_ref = refs[2 * N_BRANCH:]

    @pl.when(pl.program_id(1) == 0)
    def _():
        for n in range(N_BRANCH):
            _cast_rows(lambda rows, n=n: w_ref[n, rows, :], wb_ref.at[n], w_ref.shape[1], chunk)

    acc = None
    for n in range(N_BRANCH):
        proj = jnp.dot(br_refs[n][...], wb_ref[n], preferred_element_type=F32)
        term = jax.nn.sigmoid(gate_refs[n][...].astype(F32)) * proj
        acc = term if acc is None else acc + term
    o_ref[...] = acc.astype(o_ref.dtype)


def _merge(branches, gates, w_branch, layer, tn=512, tm=1024):
    t, width = branches[0].shape
    d = w_branch.shape[-1]
    tn = min(tn, d)
    tm = min(tm, t)
    assert d % tn == 0 and t % tm == 0 and gates.shape == (t, N_BRANCH * d)
    chunk = _pick_tile(width, (256, 128, 8))
    gate_spec = lambda n: pl.BlockSpec((tm, tn), lambda j, i: (i, (n * d) // tn + j))
    vmem = 2 * N_BRANCH * (tm * width * 2 + tm * tn * 2 + width * tn * 4) + N_BRANCH * width * tn * 2 \
        + 2 * tm * tn * 2 + 4 * tm * tn * 4
    return pl.pallas_call(
        functools.partial(_merge_body, chunk),
        out_shape=jax.ShapeDtypeStruct((t, d), BF16),
        grid=(d // tn, t // tm),
        in_specs=[pl.BlockSpec((tm, width), lambda j, i: (i, 0))] * N_BRANCH
        + [gate_spec(n) for n in range(N_BRANCH)]
        + [pl.BlockSpec((None, N_BRANCH, width, tn), lambda j, i: (layer, 0, 0, j))],
        out_specs=pl.BlockSpec((tm, tn), lambda j, i: (i, j)),
        scratch_shapes=[pltpu.VMEM((N_BRANCH, width, tn), BF16)],
        compiler_params=_params(("arbitrary", "arbitrary"), vmem),
        name="merge",
    )(*branches, *([gates] * N_BRANCH), w_branch)


def _up_body(chunk, h_ref, wg_ref, wu_ref, o_ref, wgb_ref, wub_ref):
    @pl.when(pl.program_id(1) == 0)
    def _():
        _cast_rows(lambda rows: wg_ref[rows, :], wgb_ref, wg_ref.shape[0], chunk)
        _cast_rows(lambda rows: wu_ref[rows, :], wub_ref, wu_ref.shape[0], chunk)

    h = h_ref[...]
    g = jnp.dot(h, wgb_ref[...], preferred_element_type=F32)
    u = jnp.dot(h, wub_ref[...], preferred_element_type=F32)
    o_ref[...] = (g * jax.nn.sigmoid(g) * u).astype(o_ref.dtype)


def _swiglu_up(h, wg, wu, tm=1024):
    t, k = h.shape
    n_e, _, f = wg.shape
    tn = _pick_tile(f, (256, 128))
    tm = min(tm, t)
    nj = f // tn
    chunk = _pick_tile(k, (256, 128, 8))
    w_spec = pl.BlockSpec((None, k, tn), lambda j, i: (j // nj, 0, j % nj))
    vmem = 2 * tm * k * 2 + 4 * k * tn * 4 + 2 * k * tn * 2 + 2 * tm * tn * 2 + 4 * tm * tn * 4
    return pl.pallas_call(
        functools.partial(_up_body, chunk),
        out_shape=jax.ShapeDtypeStruct((t, n_e * f), BF16),
        grid=(n_e * nj, t // tm),
        in_specs=[pl.BlockSpec((tm, k), lambda j, i: (i, 0)), w_spec, w_spec],
        out_specs=pl.BlockSpec((tm, tn), lambda j, i: (i, j)),
        scratch_shapes=[pltpu.VMEM((k, tn), BF16), pltpu.VMEM((k, tn), BF16)],
        compiler_params=_params(("arbitrary", "arbitrary"), vmem),
        name="swiglu_up",
    )(h, wg, wu)


def _round_body(w_ref, o_ref):
    o_ref[...] = w_ref[...].astype(o_ref.dtype)


def _round_bf16(w, layer, tr=512):
    _, k, n = w.shape
    tr = _pick_tile(k, (tr, 256, 128, 8))
    return pl.pallas_call(
        _round_body,
        out_shape=jax.ShapeDtypeStruct((k, n), BF16),
        grid=(k // tr,),
        in_specs=[pl.BlockSpec((None, tr, n), lambda i: (layer, i, 0))],
        out_specs=pl.BlockSpec((tr, n), lambda i: (i, 0)),
        compiler_params=_params(("arbitrary",), 2 * tr * n * 6),
        name="round_bf16",
    )(w)


def _down_body(a_ref, w_ref, o_ref):
    o_ref[...] = jnp.dot(a_ref[...], w_ref[...], preferred_element_type=F32)


def _swiglu_down(a, wd, layer, tn=512, tm=512):
    t, k = a.shape
    n = wd.shape[2]
    tn = min(tn, n)
    tm = min(tm, t)
    assert n % tn == 0 and t % tm == 0
    wb = _round_bf16(wd, layer)
    vmem = 2 * tm * k * 2 + 2 * k * tn * 2 + 2 * tm * tn * 4 + tm * tn * 4
    return pl.pallas_call(
        _down_body,
        out_shape=jax.ShapeDtypeStruct((t, n), F32),
        grid=(n // tn, t // tm),
        in_specs=[pl.BlockSpec((tm, k), lambda j, i: (i, 0)),
                  pl.BlockSpec((k, tn), lambda j, i: (0, j))],
        out_specs=pl.BlockSpec((tm, tn), lambda j, i: (i, j)),
        compiler_params=_params(("arbitrary", "arbitrary"), vmem),
        name="swiglu_down",
    )(a, wb)


MOE_TILE = 256


def _dispatch_body(n_tok, pos1_ref, pos2_ref, h_ref, o_ref, src_ref, buf_ref, sem):
    r = pl.program_id(0)

    def row_copy(tile, slot, k):
        return pltpu.make_async_copy(h_ref.at[pl.ds(src_ref[tile * MOE_TILE + k], 1), :],
                                     buf_ref.at[slot, pl.ds(k, 1), :], sem.at[slot])

    def start_tile(tile, slot):
        lax.fori_loop(0, MOE_TILE, lambda k, c: (row_copy(tile, slot, k).start(), c)[1], 0,
                      unroll=8)

    @pl.when(r == 0)
    def _():
        def clear(p, c):
            src_ref[p] = 0
            return c

        def invert(t, c):
            src_ref[pos1_ref[t]] = t
            src_ref[pos2_ref[t]] = t
            return c

        lax.fori_loop(0, src_ref.shape[0], clear, 0, unroll=8)
        lax.fori_loop(0, n_tok, invert, 0, unroll=8)
        start_tile(0, 0)

    @pl.when(r + 1 < pl.num_programs(0))
    def _():
        start_tile(r + 1, (r + 1) % 2)

    slot = r % 2
    pltpu.make_async_copy(h_ref.at[pl.ds(0, MOE_TILE), :], buf_ref.at[slot], sem.at[slot]).wait()
    o_ref[...] = buf_ref[slot]


def _moe_dispatch(hp, pos1, pos2, n_rows):
    t, w = hp.shape
    assert n_rows % MOE_TILE == 0
    return pl.pallas_call(
        functools.partial(_dispatch_body, t),
        out_shape=jax.ShapeDtypeStruct((n_rows, w), hp.dtype),
        grid_spec=pltpu.PrefetchScalarGridSpec(
            num_scalar_prefetch=2,
            grid=(n_rows // MOE_TILE,),
            in_specs=[pl.BlockSpec(memory_space=pl.ANY)],
            out_specs=pl.BlockSpec((MOE_TILE, w), lambda r, p1, p2: (r, 0)),
            scratch_shapes=[pltpu.SMEM((n_rows,), jnp.int32),
                            pltpu.VMEM((2, MOE_TILE, w), hp.dtype),
                            pltpu.SemaphoreType.DMA((2,))]),
        compiler_params=_params(("arbitrary",), 6 * MOE_TILE * w * 4),
        name="moe_dispatch",
    )(pos1, pos2, hp)


def _group_body(swiglu, chunk, te_ref, nused_ref, *refs):
    if swiglu:
        x_ref, wg_ref, wu_ref, o_ref, wgb_ref, wub_ref = refs
        pairs = ((wg_ref, wgb_ref), (wu_ref, wub_ref))
    else:
        x_ref, w_ref, o_ref, wb_ref = refs
        pairs = ((w_ref, wb_ref),)
    r = pl.program_id(1)
    expert = te_ref[r]
    prev = te_ref[jnp.maximum(r - 1, 0)]

    @pl.when(jnp.logical_or(r == 0, expert != prev))
    def _():
        for src_ref, dst_ref in pairs:
            _cast_rows(lambda rows, s=src_ref: s[rows, :], dst_ref, src_ref.shape[0], chunk)

    @pl.when(r < nused_ref[0])
    def _():
        if swiglu:
            lo, hi = _unpack_bf16_pairs(x_ref[...])
            half = lo.shape[1]

            def proj(wb):
                return jnp.dot(lo, wb[0:half, :], preferred_element_type=F32) \
                    + jnp.dot(hi, wb[half:, :], preferred_element_type=F32)

            g = proj(wgb_ref)
            o_ref[...] = (g * jax.nn.sigmoid(g) * proj(wub_ref)).astype(o_ref.dtype)
        else:
            o_ref[...] = jnp.dot(x_ref[...], wb_ref[...],
                                 preferred_element_type=F32).astype(o_ref.dtype)

    @pl.when(r >= nused_ref[0])
    def _():
        o_ref[...] = jnp.zeros_like(o_ref)


def _grouped_matmul(xs, weights, layer, tile_expert, n_used, out_dtype, tn, name):
    p, kx = xs.shape
    swiglu = len(weights) == 2
    _, _, k, n = weights[0].shape
    assert k == (2 * kx if swiglu else kx) and n % tn == 0 and p % MOE_TILE == 0
    chunk = _pick_tile(k, (256, 128, 8))
    w_spec = pl.BlockSpec((None, None, k, tn), lambda j, r, te, nu: (layer, te[r], 0, j))
    osz = jnp.dtype(out_dtype).itemsize
    vmem = 2 * MOE_TILE * kx * xs.dtype.itemsize + len(weights) * (2 * k * tn * 4 + k * tn * 2) \
        + 2 * MOE_TILE * tn * osz + 6 * MOE_TILE * tn * 4 + 2 * MOE_TILE * k * 2
    return pl.pallas_call(
        functools.partial(_group_body, swiglu, chunk),
        out_shape=jax.ShapeDtypeStruct((p, n), out_dtype),
        grid_spec=pltpu.PrefetchScalarGridSpec(
            num_scalar_prefetch=2,
            grid=(n // tn, p // MOE_TILE),
            in_specs=[pl.BlockSpec((MOE_TILE, kx), lambda j, r, te, nu: (r, 0))]
            + [w_spec] * len(weights),
            out_specs=pl.BlockSpec((MOE_TILE, tn), lambda j, r, te, nu: (r, j)),
            scratch_shapes=[pltpu.VMEM((k, tn), BF16)] * len(weights)),
        compiler_params=_params(("arbitrary", "arbitrary"), vmem),
        name=name,
    )(tile_expert, n_used, xs, *weights)


def _combine_body(tc, n_experts, pos1_ref, pos2_ref, route_ref, ye_ref, o_ref, buf_ref, sem):
    i = pl.program_id(0)

    def row_copies(tile, slot, k):
        t = tile * tc + k
        return (pltpu.make_async_copy(ye_ref.at[pl.ds(pos1_ref[t], 1), :],
                                      buf_ref.at[slot, 0, pl.ds(k, 1), :], sem.at[slot]),
                pltpu.make_async_copy(ye_ref.at[pl.ds(pos2_ref[t], 1), :],
                                      buf_ref.at[slot, 1, pl.ds(k, 1), :], sem.at[slot]))

    def start_tile(tile, slot):
        def body(k, c):
            for cp in row_copies(tile, slot, k):
                cp.start()
            return c
        lax.fori_loop(0, tc, body, 0, unroll=8)

    def wait_tile(slot):
        for half in range(2):
            pltpu.make_async_copy(ye_ref.at[pl.ds(0, tc), :], buf_ref.at[slot, half],
                                  sem.at[slot]).wait()

    @pl.when(i == 0)
    def _():
        start_tile(0, 0)

    @pl.when(i + 1 < pl.num_programs(0))
    def _():
        start_tile(i + 1, (i + 1) % 2)

    slot = i % 2
    wait_tile(slot)
    g1 = route_ref[:, n_experts + 2:n_experts + 3]
    g2 = route_ref[:, n_experts + 3:n_experts + 4]
    o_ref[...] = g1 * buf_ref[slot, 0] + g2 * buf_ref[slot, 1]


def _moe_combine(ye, route, pos1, pos2, n_experts, tc=128):
    t = route.shape[0]
    d = ye.shape[1]
    tc = min(tc, t)
    assert t % tc == 0
    return pl.pallas_call(
        functools.partial(_combine_body, tc, n_experts),
        out_shape=jax.ShapeDtypeStruct((t, d), F32),
        grid_spec=pltpu.PrefetchScalarGridSpec(
            num_scalar_prefetch=2,
            grid=(t // tc,),
            in_specs=[pl.BlockSpec((tc, LANE), lambda i, p1, p2: (i, 0)),
                      pl.BlockSpec(memory_space=pl.ANY)],
            out_specs=pl.BlockSpec((tc, d), lambda i, p1, p2: (i, 0)),
            scratch_shapes=[pltpu.VMEM((2, 2, tc, d), F32), pltpu.SemaphoreType.DMA((2,))]),
        compiler_params=_params(("arbitrary",), 4 * tc * d * 4 + 4 * tc * d * 4),
        name="moe_combine",
    )(pos1, pos2, route, ye)


def _moe(hp, route, rank, counts, layer_idx, w_gate, w_up, w_down):
    t = hp.shape[0]
    n_e = w_gate.shape[1]
    n_rows = TOP_K * t + n_e * MOE_TILE
    n_tiles = n_rows // MOE_TILE
    i1 = route[:, n_e].astype(jnp.int32)
    i2 = route[:, n_e + 1].astype(jnp.int32)
    cnt = counts[0, :n_e].astype(jnp.int32)
    tiles = (cnt + MOE_TILE - 1) // MOE_TILE
    tile_end = jnp.cumsum(tiles)
    row0 = (tile_end - tiles) * MOE_TILE
    slot = row0[None, :] + rank[:, :n_e].astype(jnp.int32)
    experts = jnp.arange(n_e, dtype=jnp.int32)[None, :]
    pos1 = jnp.sum(jnp.where(experts == i1[:, None], slot, 0), axis=1).astype(jnp.int32)
    pos2 = jnp.sum(jnp.where(experts == i2[:, None], slot, 0), axis=1).astype(jnp.int32)
    n_used = tile_end[-1:].astype(jnp.int32)
    tile_ids = jnp.minimum(jnp.arange(n_tiles, dtype=jnp.int32), n_used[0] - 1)
    tile_expert = jnp.sum(tile_ids[:, None] >= tile_end[None, :], axis=1).astype(jnp.int32)

    xs = _moe_dispatch(hp, pos1, pos2, n_rows)
    f = w_gate.shape[-1]
    a = _grouped_matmul(xs, (w_gate, w_up), layer_idx, tile_expert, n_used, BF16,
                        _pick_tile(f, (512, 256, 128)), "moe_up")
    ye = _grouped_matmul(a, (w_down,), layer_idx, tile_expert, n_used, F32,
                         _pick_tile(w_down.shape[-1], (1024, 512, 256, 128)), "moe_down")
    return _moe_combine(ye, route, pos1, pos2, n_e)


def _rel_bucket(dist):
    max_exact = REL_BUCKETS // 2
    d = jnp.maximum(dist, 0)
    df = jnp.maximum(d, max_exact).astype(F32)
    large = max_exact + (jnp.log(df / max_exact) / math.log(REL_MAX_DIST / max_exact)
                         * (REL_BUCKETS - max_exact)).astype(jnp.int32)
    large = jnp.minimum(large, REL_BUCKETS - 1)
    return jnp.where(d < max_exact, d, large)


def _bias_body(n_heads, head0, head_stride, bucket_ref, tab_ref, o_ref):
    bucket = bucket_ref[...]
    base = head0 + head_stride * pl.program_id(0)
    for h in range(n_heads):
        acc = jnp.zeros(bucket.shape, F32)
        for b in range(REL_BUCKETS):
            acc = jnp.where(bucket == b, tab_ref[b, base + h], acc)
        o_ref[h] = acc


def _bias_tiles(rel_bias, buckets, n_heads, head0, head_stride):
    n_tiles, q, k = buckets.shape
    return pl.pallas_call(
        functools.partial(_bias_body, n_heads, head0, head_stride),
        out_shape=jax.ShapeDtypeStruct((n_tiles, n_heads, q, k), F32),
        grid=(n_tiles,),
        in_specs=[pl.BlockSpec((None, q, k), lambda t: (t, 0, 0)),
                  pl.BlockSpec(memory_space=pltpu.SMEM)],
        out_specs=pl.BlockSpec((None, n_heads, q, k), lambda t: (t, 0, 0, 0)),
        compiler_params=_params(("arbitrary",), 4 * (n_heads + 1) * q * k * 4),
        name="bias_tiles",
    )(buckets, rel_bias)


def _bias_tiles_a(rel_bias, n_heads, seq):
    n_blk = seq // A_BLOCK
    qi = jnp.arange(A_BLOCK)[:, None]
    kj = jnp.arange(A_BLOCK)[None, :]
    dist = jnp.arange(n_blk)[:, None, None] * A_BLOCK + (qi - kj)[None]
    return _bias_tiles(rel_bias, _rel_bucket(dist), n_heads, 0, 0)


def _bias_tiles_b(rel_bias, head0, n_heads):
    qi = jnp.arange(B_BLOCK)[:, None]
    kj = jnp.arange(2 * B_BLOCK)[None, :]
    sub_dist = qi + B_BLOCK - kj
    buckets = jnp.stack([_rel_bucket(sub_dist * dil) for _, dil in B_GROUPS])
    return _bias_tiles(rel_bias, buckets, n_heads, head0, n_heads)


def _hybrid_mixer(h, bsz, seq, layer, w_in_t, lat_g, w_uk, w_uv, c_ln_g, c_ln_b, c_w_s, c_b_s,
                  d_conv, w_branch, w_out, bias_a, bias_b):
    t, d = h.shape
    width = d // N_BRANCH
    n_lat = w_uk.shape[0]
    a_heads = width // HEAD_DIM
    n_groups = len(B_GROUPS)
    iq_width = IDX_HEADS * IDX_DIM
    lat_col = width
    iq_col = lat_col + n_lat
    ik_col = iq_col + iq_width
    front = ik_col
    shift = IDX_DIM + IDX_HEADS
    qkv_w = 3 * n_groups * width
    rest_w = qkv_w + 2 * width + 3 * width + N_BRANCH * d
    assert front % 512 == 0 and w_in_t.shape[1] == front + shift + rest_w and shift <= LANE

    pa = _matmul_t(h, w_in_t, layer, row0=0, n_out=front, tm=1024, name="proj_front")
    pik = _matmul_t(h, w_in_t, layer, row0=front, n_out=LANE, name="proj_index_key")
    gate_col = qkv_w + 5 * width
    rest = _matmul_t(h, w_in_t, layer, row0=front + shift, n_out=gate_col, tm=1024,
                     name="proj_rest")
    gates = _matmul_t(h, w_in_t, layer, row0=front + shift + gate_col, n_out=N_BRANCH * d,
                      tm=1024, out_dtype=BF16, name="proj_gates")

    pa3 = pa.reshape(bsz, seq, front)
    pik3 = pik.reshape(bsz, seq, LANE)
    rest3 = rest.reshape(bsz, seq, gate_col)

    lat, latt, ikt = _latnorm(pa3, pik3, lat_g, lat_col, n_lat)
    wuk_t = jnp.transpose(w_uk, (1, 2, 0)).astype(BF16)
    wuv_t = jnp.transpose(w_uv, (1, 0, 2)).astype(BF16)
    o_a = _dsa(pa3, pik3, lat, latt, ikt, wuk_t, wuv_t, bias_a,
               q_width=width, iq_col=iq_col, iq_width=iq_width,
               n_variants=DSA_KEY_RANGE_VARIANTS).reshape(t, width)

    outs, lses = [], []
    for g, (window, dil) in enumerate(B_GROUPS):
        o, l = _dilated(rest3, bias_b, g, n_groups, window, dil, width)
        outs.append(o)
        lses.append(l)
    o_b = _mixture(outs, lses)

    o_c = _gmlp(rest, qkv_w, width, c_ln_g, c_ln_b, c_w_s, c_b_s.T)
    o_d = _short_conv(rest3, qkv_w + 2 * width, width, d_conv).reshape(t, width)

    mixed = _merge([o_a, o_b, o_c, o_d], gates, w_branch, layer)
    return _matmul(mixed, w_out, layer, tm=1024, name="proj_out")


def kernel(x, c, w_ada, b_ada, ada_table, rel_bias, norm_pre_mix, norm_post_mix, norm_pre_ffn,
           norm_post_ffn, w_in, a_lat_norm, a_w_uk, a_w_uv, c_ln_g, c_ln_b, c_w_s, c_b_s, d_conv,
           w_branch, w_out, ffn_w_gate, ffn_w_up, ffn_w_down, moe_router, moe_w_gate, moe_w_up,
           moe_w_down):
    bsz, seq, d = x.shape
    depth = w_in.shape[0]
    t = bsz * seq

    c_pad = jnp.pad(c, ((0, 16 - bsz % 16 if bsz % 16 else 0), (0, 0)))
    mod_shared = _ada(c_pad, w_ada, b_ada)[:bsz].reshape(bsz, N_MOD, d)
    mods = [mod_shared + ada_table[layer] for layer in range(depth)]

    w_in_t = jnp.transpose(w_in, (0, 2, 1))
    width = d // N_BRANCH
    bias_a = _bias_tiles_a(rel_bias, width // HEAD_DIM, seq)
    bias_b = _bias_tiles_b(rel_bias, width // HEAD_DIM, width // HEAD_DIM)

    h = _prenorm(x, mods[0], norm_pre_mix[0])
    for layer in range(depth):
        mod = mods[layer]
        y = _hybrid_mixer(h.reshape(t, d), bsz, seq, layer, w_in_t, a_lat_norm[layer],
                          a_w_uk[layer], a_w_uv[layer], c_ln_g[layer], c_ln_b[layer],
                          c_w_s[layer], c_b_s[layer], d_conv[layer], w_branch, w_out,
                          bias_a, bias_b)
        j = layer // 2
        dense = layer % 2 == 0
        res = _postnorm(x, y.reshape(bsz, seq, d), mod, norm_post_mix[layer], 2, nxt=(3, 4),
                        mod_next=mod, g_pre=norm_pre_ffn[layer],
                        router_t=None if dense else moe_router[j].T)
        if dense:
            x, h = res
            a = _swiglu_up(h.reshape(t, d), ffn_w_gate[j][None], ffn_w_up[j][None])
            y = _swiglu_down(a, ffn_w_down, j)
        else:
            x, hp, route, rank, counts = res
            y = _moe(hp.reshape(t, d // 2), route.reshape(t, LANE), rank.reshape(t, LANE), counts,
                     j, moe_w_gate, moe_w_up, moe_w_down)
        y3 = y.reshape(bsz, seq, d)
        if layer + 1 < depth:
            x, h = _postnorm(x, y3, mod, norm_post_ffn[layer], 5, nxt=(0, 1),
                             mod_next=mods[layer + 1], g_pre=norm_pre_mix[layer + 1])
        else:
            (x,) = _postnorm(x, y3, mod, norm_post_ffn[layer], 5)
    return x
```

```python
import functools
import math

import jax
import jax.numpy as jnp
from jax import lax
from jax.experimental import pallas as pl
from jax.experimental.pallas import tpu as pltpu

F32 = jnp.float32
BF16 = jnp.bfloat16

LANE = 128
V7X_VMEM_BYTES = 64 * 1024 * 1024
VMEM_CAP = V7X_VMEM_BYTES - 8 * 1024 * 1024

HEAD_DIM = 128
N_BRANCH = 4
IDX_HEADS = 8
IDX_DIM = 64
IDX_TOPK = 256
A_BLOCK = 128
DSA_KEY_RANGE_VARIANTS = 4
B_GROUPS = ((128, 1), (512, 4), (2048, 16))
B_BLOCK = 128
C_CHUNK = 128
C_GROUPS = 8
D_CONV_WIDTH = 3
REL_BUCKETS = 32
REL_MAX_DIST = 2048
TOP_K = 2
N_MOD = 6
EPS = 1e-6
NEG_INF = float("-inf")


def _params(semantics, vmem_bytes):
    limit = int(min(VMEM_CAP, max(vmem_bytes * 5 // 4 + (4 << 20), 16 << 20)))
    return pltpu.CompilerParams(dimension_semantics=semantics, vmem_limit_bytes=limit)


def _pick_tile(n, candidates):
    for c in candidates:
        if n % c == 0:
            return c
    raise ValueError(f"no tile in {candidates} divides {n}")


def _cast_rows(src_fn, dst_ref, n_rows, chunk):
    def body(r, carry):
        rows = pl.ds(pl.multiple_of(r * chunk, chunk), chunk)
        dst_ref[rows, :] = src_fn(rows).astype(BF16)
        return carry
    lax.fori_loop(0, n_rows // chunk, body, 0)


def _mm_body(chunk, lhs_ref, w_ref, o_ref, wb_ref):
    @pl.when(pl.program_id(1) == 0)
    def _():
        _cast_rows(lambda rows: w_ref[rows, :], wb_ref, w_ref.shape[0], chunk)

    o_ref[...] = jnp.dot(lhs_ref[...], wb_ref[...],
                         preferred_element_type=F32).astype(o_ref.dtype)


def _matmul(lhs, w, layer, *, tn=512, tm=512, out_dtype=F32, name="mm"):
    m, k = lhs.shape
    n_out = w.shape[2]
    tn = min(tn, n_out)
    tm = min(tm, m)
    assert n_out % tn == 0 and m % tm == 0 and tn % LANE == 0
    chunk = _pick_tile(k, (256, 128, 64, 8))
    osz = jnp.dtype(out_dtype).itemsize
    vmem = 2 * tm * k * 2 + 2 * k * tn * 4 + k * tn * 2 + 2 * tm * tn * osz + tm * tn * 4 \
        + chunk * tn * 8
    return pl.pallas_call(
        functools.partial(_mm_body, chunk),
        out_shape=jax.ShapeDtypeStruct((m, n_out), out_dtype),
        grid=(n_out // tn, m // tm),
        in_specs=[pl.BlockSpec((tm, k), lambda j, i: (i, 0)),
                  pl.BlockSpec((None, k, tn), lambda j, i: (layer, 0, j))],
        out_specs=pl.BlockSpec((tm, tn), lambda j, i: (i, j)),
        scratch_shapes=[pltpu.VMEM((k, tn), BF16)],
        compiler_params=_params(("arbitrary", "arbitrary"), vmem),
        name=name,
    )(lhs, w)


def _mm_t_body(chunk, lhs_ref, w_ref, o_ref, wb_ref):
    @pl.when(pl.program_id(1) == 0)
    def _():
        _cast_rows(lambda rows: w_ref[rows, :], wb_ref, w_ref.shape[0], chunk)

    acc = lax.dot_general(lhs_ref[...], wb_ref[...], (((1,), (1,)), ((), ())),
                          preferred_element_type=F32)
    o_ref[...] = acc.astype(o_ref.dtype)


def _matmul_t(lhs, w_t, layer, *, row0, n_out, tn=512, tm=512, out_dtype=F32, name="mm_t"):
    m, k = lhs.shape
    tn = min(tn, n_out)
    tm = min(tm, m)
    assert n_out % tn == 0 and m % tm == 0 and row0 % 8 == 0 and tn % LANE == 0
    chunk = _pick_tile(tn, (256, 128))
    osz = jnp.dtype(out_dtype).itemsize
    vmem = 2 * tm * k * 2 + 2 * k * tn * 4 + k * tn * 2 + 2 * tm * tn * osz + tm * tn * 4 \
        + 4 * chunk * k * 4
    return pl.pallas_call(
        functools.partial(_mm_t_body, chunk),
        out_shape=jax.ShapeDtypeStruct((m, n_out), out_dtype),
        grid=(n_out // tn, m // tm),
        in_specs=[pl.BlockSpec((tm, k), lambda j, i: (i, 0)),
                  pl.BlockSpec((None, pl.Element(tn), pl.Element(k)),
                               lambda j, i: (layer, pl.multiple_of(row0 + j * tn, 8), 0))],
        out_specs=pl.BlockSpec((tm, tn), lambda j, i: (i, j)),
        scratch_shapes=[pltpu.VMEM((tn, k), BF16)],
        compiler_params=_params(("arbitrary", "arbitrary"), vmem),
        name=name,
    )(lhs, w_t)


def _ada_body(c_ref, w_ref, b_ref, o_ref):
    c = c_ref[...]
    a = (c * jax.nn.sigmoid(c)).astype(BF16)
    o_ref[...] = jnp.dot(a, w_ref[...].astype(BF16), preferred_element_type=F32) + b_ref[...]


def _ada(c_pad, w_ada, b_ada, tn=512):
    m, k = c_pad.shape
    n = w_ada.shape[1]
    tn = _pick_tile(n, (tn, 256, 128))
    vmem = 2 * k * tn * 4 + k * tn * 2 + 4 * m * k * 4
    return pl.pallas_call(
        _ada_body,
        out_shape=jax.ShapeDtypeStruct((m, n), F32),
        grid=(n // tn,),
        in_specs=[pl.BlockSpec((m, k), lambda j: (0, 0)),
                  pl.BlockSpec((k, tn), lambda j: (0, j)),
                  pl.BlockSpec((1, tn), lambda j: (0, j))],
        out_specs=pl.BlockSpec((m, tn), lambda j: (0, j)),
        compiler_params=_params(("arbitrary",), vmem),
        name="ada",
    )(c_pad, w_ada, b_ada.reshape(1, n))


def _rms(x, g):
    return x * lax.rsqrt(jnp.mean(x * x, axis=-1, keepdims=True) + EPS) * g


def _route(h, router_ref, n_experts):
    rows = h.shape[0]
    lane = lax.broadcasted_iota(jnp.int32, (rows, LANE), 1)
    logits = jnp.full((rows, LANE), NEG_INF, F32)
    for e in range(n_experts):
        le = jnp.sum(h * router_ref[e:e + 1, :], axis=-1, keepdims=True)
        logits = jnp.where(lane == e, le, logits)
    m1 = jnp.max(logits, axis=-1, keepdims=True)
    i1 = jnp.min(jnp.where(logits == m1, lane, LANE), axis=-1, keepdims=True)
    rest = jnp.where(lane == i1, NEG_INF, logits)
    m2 = jnp.max(rest, axis=-1, keepdims=True)
    i2 = jnp.min(jnp.where(rest == m2, lane, LANE), axis=-1, keepdims=True)
    e2 = jnp.exp(m2 - m1)
    den = 1.0 + e2
    member = jnp.where(lane == i1, 1.0, jnp.where(lane == i2, 1.0, 0.0))
    extra = jnp.where(lane == n_experts, i1.astype(F32),
                      jnp.where(lane == n_experts + 1, i2.astype(F32),
                                jnp.where(lane == n_experts + 2, 1.0 / den, e2 / den)))
    return jnp.where(lane < n_experts, member, jnp.where(lane < n_experts + 4, extra, 0.0))


def _pack_bf16_pairs(h):
    half = h.shape[1] // 2
    lo = pltpu.bitcast(h[:, :half].astype(BF16).astype(F32), jnp.uint32)
    hi = pltpu.bitcast(h[:, half:].astype(BF16).astype(F32), jnp.uint32)
    return (hi & jnp.uint32(0xFFFF0000)) | (lo >> 16)


def _unpack_bf16_pairs(u):
    lo = pltpu.bitcast(u << 16, F32).astype(BF16)
    hi = pltpu.bitcast(u & jnp.uint32(0xFFFF0000), F32).astype(BF16)
    return lo, hi


def _pre_body(x_ref, mod_ref, g_ref, h_ref):
    h = _rms(x_ref[...], g_ref[...]) * (1.0 + mod_ref[1:2, :]) + mod_ref[0:1, :]
    h_ref[...] = h.astype(h_ref.dtype)


def _prenorm(x3, mod, g, tr=256):
    b, s, d = x3.shape
    tr = min(tr, s)
    vmem = 2 * tr * d * 4 + 2 * tr * d * 2 + 4 * tr * d * 4
    return pl.pallas_call(
        _pre_body,
        out_shape=jax.ShapeDtypeStruct((b, s, d), BF16),
        grid=(b, s // tr),
        in_specs=[pl.BlockSpec((None, tr, d), lambda bi, i: (bi, i, 0)),
                  pl.BlockSpec((None, N_MOD, d), lambda bi, i: (bi, 0, 0)),
                  pl.BlockSpec((1, d), lambda bi, i: (0, 0))],
        out_specs=pl.BlockSpec((None, tr, d), lambda bi, i: (bi, i, 0)),
        compiler_params=_params(("arbitrary", "arbitrary"), vmem),
        name="prenorm",
    )(x3, mod, g.reshape(1, d))


def _post_body(gate_row, nxt, n_experts, *refs):
    x_ref, y_ref, mod_ref, gpost_ref = refs[:4]
    pos = 4
    if nxt is not None:
        modn_ref, gpre_ref = refs[pos:pos + 2]
        pos += 2
    if n_experts:
        router_ref = refs[pos]
        pos += 1
    xo_ref = refs[pos]
    pos += 1
    xn = x_ref[...] + mod_ref[gate_row:gate_row + 1, :] * _rms(y_ref[...], gpost_ref[...])
    xo_ref[...] = xn
    if nxt is not None:
        shift_row, scale_row = nxt
        h = _rms(xn, gpre_ref[...]) * (1.0 + modn_ref[scale_row:scale_row + 1, :]) \
            + modn_ref[shift_row:shift_row + 1, :]
        if not n_experts:
            refs[pos][...] = h.astype(BF16)
            return
        hp_ref, route_ref, rank_ref, counts_ref, carry_ref = refs[pos:pos + 5]
        hp_ref[...] = _pack_bf16_pairs(h)
        route = _route(h, router_ref, n_experts)
        route_ref[...] = route

        @pl.when((pl.program_id(0) == 0) & (pl.program_id(1) == 0))
        def _():
            carry_ref[...] = jnp.zeros_like(carry_ref)

        rows = route.shape[0]
        lane = lax.broadcasted_iota(jnp.int32, route.shape, 1)
        member = jnp.where(lane < n_experts, route, 0.0)
        ri = lax.broadcasted_iota(jnp.int32, (rows, rows), 0)
        ci = lax.broadcasted_iota(jnp.int32, (rows, rows), 1)
        earlier = jnp.where(ci < ri, 1.0, 0.0).astype(BF16)
        within = jnp.dot(earlier, member.astype(BF16), preferred_element_type=F32)
        rank_ref[...] = within + carry_ref[0:1, :]
        carry_ref[...] = carry_ref[...] + jnp.sum(member, axis=0, keepdims=True)
        counts_ref[...] = carry_ref[...]


def _postnorm(x3, y3, mod, g_post, gate_row, *, nxt=None, mod_next=None, g_pre=None,
              router_t=None, tr=256):
    b, s, d = x3.shape
    tr = min(tr, s)
    n_experts = 0 if router_t is None else router_t.shape[0]
    row = lambda bi, i: (bi, i, 0)
    mod_spec = pl.BlockSpec((None, N_MOD, d), lambda bi, i: (bi, 0, 0))
    in_specs = [pl.BlockSpec((None, tr, d), row), pl.BlockSpec((None, tr, d), row), mod_spec,
                pl.BlockSpec((1, d), lambda bi, i: (0, 0))]
    args = [x3, y3, mod, g_post.reshape(1, d)]
    out_shape = [jax.ShapeDtypeStruct((b, s, d), F32)]
    out_specs = [pl.BlockSpec((None, tr, d), row)]
    scratch = []
    if nxt is not None:
        in_specs += [mod_spec, pl.BlockSpec((1, d), lambda bi, i: (0, 0))]
        args += [mod_next, g_pre.reshape(1, d)]
    if nxt is not None and not n_experts:
        out_shape.append(jax.ShapeDtypeStruct((b, s, d), BF16))
        out_specs.append(pl.BlockSpec((None, tr, d), row))
    if n_experts:
        assert nxt is not None and n_experts + 4 <= LANE
        in_specs.append(pl.BlockSpec((n_experts, d), lambda bi, i: (0, 0)))
        args.append(router_t)
        out_shape += [jax.ShapeDtypeStruct((b, s, d // 2), jnp.uint32),
                      jax.ShapeDtypeStruct((b, s, LANE), F32),
                      jax.ShapeDtypeStruct((b, s, LANE), F32),
                      jax.ShapeDtypeStruct((8, LANE), F32)]
        out_specs += [pl.BlockSpec((None, tr, d // 2), row), pl.BlockSpec((None, tr, LANE), row),
                      pl.BlockSpec((None, tr, LANE), row),
                      pl.BlockSpec((8, LANE), lambda bi, i: (0, 0))]
        scratch.append(pltpu.VMEM((8, LANE), F32))
    vmem = 2 * 3 * tr * d * 4 + 2 * tr * d * 2 + 6 * tr * d * 4
    return pl.pallas_call(
        functools.partial(_post_body, gate_row, nxt, n_experts),
        out_shape=out_shape,
        grid=(b, s // tr),
        in_specs=in_specs,
        out_specs=out_specs,
        scratch_shapes=scratch,
        compiler_params=_params(("arbitrary", "arbitrary"), vmem),
        name="postnorm",
    )(*args)


def _lat_body(alat_ref, pik_ref, g_ref, lat_ref, latt_ref, ikt_ref):
    lat = _rms(alat_ref[...], g_ref[...])
    lat_ref[...] = lat.astype(BF16)
    latt_ref[...] = lat.T.astype(BF16)
    ikt_ref[...] = pik_ref[...].T.astype(BF16)


def _latnorm(pa3, pik3, g_lat, lat_col, n_lat, tr=512):
    b, s, _ = pa3.shape
    tr = min(tr, s)
    assert lat_col % n_lat == 0
    vmem = 2 * tr * (n_lat + LANE) * 4 + 4 * tr * (n_lat + LANE) * 2 + 4 * tr * n_lat * 4
    return pl.pallas_call(
        _lat_body,
        out_shape=[jax.ShapeDtypeStruct((b, s, n_lat), BF16),
                   jax.ShapeDtypeStruct((b, n_lat, s), BF16),
                   jax.ShapeDtypeStruct((b, LANE, s), BF16)],
        grid=(b, s // tr),
        in_specs=[pl.BlockSpec((None, tr, n_lat), lambda bi, i: (bi, i, lat_col // n_lat)),
                  pl.BlockSpec((None, tr, LANE), lambda bi, i: (bi, i, 0)),
                  pl.BlockSpec((1, n_lat), lambda bi, i: (0, 0))],
        out_specs=[pl.BlockSpec((None, tr, n_lat), lambda bi, i: (bi, i, 0)),
                   pl.BlockSpec((None, n_lat, tr), lambda bi, i: (bi, 0, i)),
                   pl.BlockSpec((None, LANE, tr), lambda bi, i: (bi, 0, i))],
        compiler_params=_params(("arbitrary", "arbitrary"), vmem),
        name="latnorm",
    )(pa3, pik3, g_lat.reshape(1, n_lat))


def _dsa_body(topk, n_variants, *refs):
    i = pl.program_id(1)
    n_blk = refs[4].shape[0] // A_BLOCK
    per = -(-n_blk // n_variants)
    for v in range(n_variants):
        blocks = min(n_blk, (v + 1) * per)

        @pl.when((i >= v * per) & (i < (v + 1) * per))
        def _(blocks=blocks):
            _dsa_compute(topk, blocks * A_BLOCK, i, *refs)


def _dsa_compute(topk, seq, i, q_ref, iq_ref, iw_ref, ikt_ref, lat_ref, latt_ref, wuk_ref, wuv_ref,
                 bias_ref, o_ref, madd_ref):
    n_blk = seq // A_BLOCK
    n_heads = wuk_ref.shape[0]
    int_min = jnp.int32(-2 ** 31)
    qpos = i * A_BLOCK + lax.broadcasted_iota(jnp.int32, (A_BLOCK, 1), 0)
    kpos = lax.broadcasted_iota(jnp.int32, (A_BLOCK, seq), 1)

    iq = iq_ref[...].astype(BF16)
    iw = iw_ref[...]
    ikt = ikt_ref[0:IDX_DIM, 0:seq]
    score = jnp.zeros((A_BLOCK, seq), F32)
    for h in range(IDX_HEADS):
        l = jnp.dot(iq[:, h * IDX_DIM:(h + 1) * IDX_DIM], ikt, preferred_element_type=F32)
        score = score + iw[:, IDX_DIM + h:IDX_DIM + h + 1] * jnp.maximum(l, 0.0)

    score = jnp.where(score == 0.0, 0.0, score)
    key = pltpu.bitcast(score, jnp.int32)
    key = jnp.where(key < 0, key ^ jnp.int32(0x7FFFFFFF), key)
    key = jnp.where(kpos <= qpos, key, int_min)
    k_eff = jnp.minimum(qpos + 1, topk).astype(F32)

    def count(mask):
        return jnp.sum(jnp.where(mask, 1.0, 0.0), axis=1, keepdims=True)

    t0 = jnp.where(count(key >= 0) >= k_eff, jnp.int32(0), int_min)

    def value_step(it, t):
        c = t | (jnp.int32(1) << (30 - it))
        return jnp.where(count(key >= c) >= k_eff, c, t)

    t = lax.fori_loop(0, 31, value_step, t0)

    need = k_eff - count(key > t)
    eq = key == t
    partial_tie = jnp.max(count(eq) - need) > 0.0

    @pl.when(jnp.logical_not(partial_tie))
    def _():
        madd_ref[:, 0:seq] = jnp.where(key >= t, 0.0, NEG_INF)

    @pl.when(partial_tie)
    def _():
        idx_bits = max(1, (seq - 1).bit_length())

        def index_step(it, p):
            c = p | (jnp.int32(1) << (idx_bits - 1 - it))
            below = jnp.sum(jnp.where(eq, jnp.where(kpos < c, 1.0, 0.0), 0.0),
                            axis=1, keepdims=True)
            return jnp.where(below < need, c, p)

        p = lax.fori_loop(0, idx_bits, index_step, jnp.zeros((A_BLOCK, 1), jnp.int32))
        tie_add = jnp.where(eq, jnp.where(kpos <= p, 0.0, NEG_INF), NEG_INF)
        madd_ref[:, 0:seq] = jnp.where(key > t, 0.0, tie_add)

    scale = HEAD_DIM ** -0.5
    for h in range(n_heads):
        cols = slice(h * HEAD_DIM, (h + 1) * HEAD_DIM)
        qa = jnp.dot(q_ref[:, cols].astype(BF16), wuk_ref[h],
                     preferred_element_type=F32).astype(BF16)
        lg = jnp.dot(qa, latt_ref[:, 0:seq], preferred_element_type=F32) * scale
        bias = jnp.concatenate([bias_ref[jnp.maximum(i - j, 0), h] for j in range(n_blk)], axis=1)
        lg = lg + bias + madd_ref[:, 0:seq]
        m = jnp.max(lg, axis=1, keepdims=True)
        e = jnp.exp(lg - m)
        den = jnp.sum(e, axis=1, keepdims=True)
        o_lat = jnp.dot(e.astype(BF16), lat_ref[0:seq, :], preferred_element_type=F32) / den
        o_ref[:, cols] = jnp.dot(o_lat.astype(BF16), wuv_ref[h],
                                 preferred_element_type=F32).astype(o_ref.dtype)


def _dsa(pa3, pik3, lat, latt, ikt, wuk_t, wuv_t, bias_a, *, q_width, iq_col, iq_width,
         n_variants):
    b, s, _ = pa3.shape
    n_lat = lat.shape[-1]
    n_heads = wuk_t.shape[0]
    n_blk = s // A_BLOCK
    topk = min(IDX_TOPK, s // 4)
    assert iq_col % iq_width == 0
    vmem = 2 * (A_BLOCK * (q_width + iq_width + LANE) * 4 + LANE * s * 2 + 2 * s * n_lat * 2
                + 2 * n_heads * HEAD_DIM * n_lat * 2 + n_heads * n_blk * A_BLOCK * A_BLOCK * 4
                + A_BLOCK * q_width * 2) + 10 * A_BLOCK * s * 4
    return pl.pallas_call(
        functools.partial(_dsa_body, topk, n_variants),
        out_shape=jax.ShapeDtypeStruct((b, s, q_width), BF16),
        grid=(b, n_blk),
        in_specs=[pl.BlockSpec((None, A_BLOCK, q_width), lambda bi, i: (bi, i, 0)),
                  pl.BlockSpec((None, A_BLOCK, iq_width), lambda bi, i: (bi, i, iq_col // iq_width)),
                  pl.BlockSpec((None, A_BLOCK, LANE), lambda bi, i: (bi, i, 0)),
                  pl.BlockSpec((None, LANE, s), lambda bi, i: (bi, 0, 0)),
                  pl.BlockSpec((None, s, n_lat), lambda bi, i: (bi, 0, 0)),
                  pl.BlockSpec((None, n_lat, s), lambda bi, i: (bi, 0, 0)),
                  pl.BlockSpec(wuk_t.shape, lambda bi, i: (0, 0, 0)),
                  pl.BlockSpec(wuv_t.shape, lambda bi, i: (0, 0, 0)),
                  pl.BlockSpec(bias_a.shape, lambda bi, i: (0, 0, 0, 0))],
        out_specs=pl.BlockSpec((None, A_BLOCK, q_width), lambda bi, i: (bi, i, 0)),
        scratch_shapes=[pltpu.VMEM((A_BLOCK, s), F32)],
        compiler_params=_params(("arbitrary", "arbitrary"), vmem),
        name="dsa",
    )(pa3, pa3, pik3, ikt, lat, latt, wuk_t, wuv_t, bias_a)


def _dil_body(span, dil, hb, has_prev, *refs):
    if has_prev:
        q_ref, kp_ref, kc_ref, vp_ref, vc_ref, bias_ref, o_ref, lse_ref = refs
    else:
        q_ref, kc_ref, vc_ref, bias_ref, o_ref, lse_ref = refs
    i = pl.program_id(1)
    hblk = pl.program_id(2)
    n_keys = 2 * B_BLOCK if has_prev else B_BLOCK
    key0 = 0 if has_prev else B_BLOCK
    qi = lax.broadcasted_iota(jnp.int32, (B_BLOCK, n_keys), 0)
    kj = lax.broadcasted_iota(jnp.int32, (B_BLOCK, n_keys), 1) + key0
    sub_dist = qi + B_BLOCK - kj
    in_band = jnp.where(sub_dist <= span,
                        jnp.where((i - 1) * B_BLOCK + kj >= 0, 0.0, NEG_INF), NEG_INF)
    madd = jnp.where(sub_dist >= 0, in_band, NEG_INF)
    lane = lax.broadcasted_iota(jnp.int32, (B_BLOCK, LANE), 1)
    scale = HEAD_DIM ** -0.5

    @pl.when(hblk == 0)
    def _():
        lse_ref[...] = jnp.zeros_like(lse_ref)

    for r in range(dil):
        rows = pl.ds(r, B_BLOCK, stride=dil) if dil > 1 else slice(None)
        lse_acc = lse_ref[rows, :]
        for h in range(hb):
            cols = slice(h * HEAD_DIM, (h + 1) * HEAD_DIM)
            q = q_ref[rows, cols].astype(BF16)
            if has_prev:
                k2 = jnp.concatenate([kp_ref[rows, cols], kc_ref[rows, cols]], axis=0).astype(BF16)
                v2 = jnp.concatenate([vp_ref[rows, cols], vc_ref[rows, cols]], axis=0).astype(BF16)
            else:
                k2 = kc_ref[rows, cols].astype(BF16)
                v2 = vc_ref[rows, cols].astype(BF16)
            lg = lax.dot_general(q, k2, (((1,), (1,)), ((), ())), preferred_element_type=F32)
            lg = lg * scale + bias_ref[h, :, key0:] + madd
            m = jnp.max(lg, axis=1, keepdims=True)
            e = jnp.exp(lg - m)
            den = jnp.sum(e, axis=1, keepdims=True)
            o_ref[rows, cols] = jnp.dot((e / den).astype(BF16), v2, preferred_element_type=F32)
            lse_acc = jnp.where(lane == hblk * hb + h, m + jnp.log(den), lse_acc)
        lse_ref[rows, :] = lse_acc


def _dilated(rest3, bias_b, g, n_groups, window, dil, width):
    b, s, w_all = rest3.shape
    n_heads = width // HEAD_DIM
    span = window // dil
    n_sub = s // dil
    assert span == B_BLOCK and n_sub % B_BLOCK == 0 and w_all % width == 0
    nb = n_sub // B_BLOCK
    rows = B_BLOCK * dil
    hb = n_heads if dil == 1 else 1
    assert HEAD_DIM == LANE
    n_hblk = n_heads // hb
    has_prev = nb > 1
    cur = lambda slot: (lambda bi, i, hk: (bi, i, (slot * n_groups + g) * n_hblk + hk))
    prev = lambda slot: (lambda bi, i, hk: (bi, jnp.maximum(i - 1, 0),
                                            (slot * n_groups + g) * n_hblk + hk))
    blk = (None, rows, hb * HEAD_DIM)
    in_specs = [pl.BlockSpec(blk, cur(0))]
    for slot in (1, 2):
        if has_prev:
            in_specs.append(pl.BlockSpec(blk, prev(slot)))
        in_specs.append(pl.BlockSpec(blk, cur(slot)))
    in_specs.append(pl.BlockSpec((None, hb, B_BLOCK, 2 * B_BLOCK), lambda bi, i, hk: (g, hk, 0, 0)))
    vmem = 2 * (len(in_specs) * rows * hb * HEAD_DIM + rows * LANE) * 4 \
        + 2 * hb * B_BLOCK * 2 * B_BLOCK * 4 + (4 << 20)
    o, lse = pl.pallas_call(
        functools.partial(_dil_body, span, dil, hb, has_prev),
        out_shape=[jax.ShapeDtypeStruct((b, s, width), F32),
                   jax.ShapeDtypeStruct((b, s, LANE), F32)],
        grid=(b, nb, n_hblk),
        in_specs=in_specs,
        out_specs=[pl.BlockSpec(blk, lambda bi, i, hk: (bi, i, hk)),
                   pl.BlockSpec((None, rows, LANE), lambda bi, i, hk: (bi, i, 0))],
        compiler_params=_params(("arbitrary", "arbitrary", "arbitrary"), vmem),
        name=f"dilated{g}",
    )(*([rest3] * (len(in_specs) - 1)), bias_b)
    return o.reshape(b * s, width), lse.reshape(b * s, LANE)


def _mix_body(n_groups, n_heads, *refs):
    o_refs = refs[:n_groups]
    l_refs = refs[n_groups:2 * n_groups]
    out_ref = refs[2 * n_groups]
    ls = [r[...] for r in l_refs]
    m = functools.reduce(jnp.maximum, ls)
    es = [jnp.exp(l - m) for l in ls]
    tot = functools.reduce(jnp.add, es)
    ws = [e / tot for e in es]
    for h in range(n_heads):
        cols = slice(h * HEAD_DIM, (h + 1) * HEAD_DIM)
        acc = ws[0][:, h:h + 1] * o_refs[0][:, cols]
        for gi in range(1, n_groups):
            acc = acc + ws[gi][:, h:h + 1] * o_refs[gi][:, cols]
        out_ref[:, cols] = acc.astype(out_ref.dtype)


def _mixture(outs, lses, tr=512):
    t, width = outs[0].shape
    tr = min(tr, t)
    n_groups = len(outs)
    vmem = 2 * n_groups * tr * (width + LANE) * 4 + 2 * tr * width * 2 + 4 * tr * width * 4
    return pl.pallas_call(
        functools.partial(_mix_body, n_groups, width // HEAD_DIM),
        out_shape=jax.ShapeDtypeStruct((t, width), BF16),
        grid=(t // tr,),
        in_specs=[pl.BlockSpec((tr, width), lambda i: (i, 0))] * n_groups
        + [pl.BlockSpec((tr, LANE), lambda i: (i, 0))] * n_groups,
        out_specs=pl.BlockSpec((tr, width), lambda i: (i, 0)),
        compiler_params=_params(("arbitrary",), vmem),
        name="mixture",
    )(*outs, *lses)


def _gmlp_body(u_ref, v_ref, g_ref, b_ref, ws_ref, bs_ref, o_ref):
    u = jax.nn.gelu(u_ref[...])
    v = jax.nn.gelu(v_ref[...])
    mu = jnp.mean(v, axis=-1, keepdims=True)
    var = jnp.mean(jnp.square(v - mu), axis=-1, keepdims=True)
    v = (v - mu) * lax.rsqrt(var + EPS) * g_ref[...] + b_ref[...]
    ti = lax.broadcasted_iota(jnp.int32, (C_CHUNK, C_CHUNK), 0)
    si = lax.broadcasted_iota(jnp.int32, (C_CHUNK, C_CHUNK), 1)
    causal = si <= ti
    gw = u.shape[1] // C_GROUPS
    for g in range(C_GROUPS):
        cols = slice(g * gw, (g + 1) * gw)
        w = jnp.where(causal, ws_ref[g], 0.0).astype(BF16)
        mixed = jnp.dot(w, v[:, cols].astype(BF16), preferred_element_type=F32) + bs_ref[:, g:g + 1]
        o_ref[:, cols] = (u[:, cols] * mixed).astype(o_ref.dtype)


def _gmlp(rest, u_col, width, ln_g, ln_b, w_s, b_s_t):
    t = rest.shape[0]
    assert u_col % width == 0 and (width // C_GROUPS) % LANE == 0
    vmem = 2 * 2 * C_CHUNK * width * 4 + 2 * C_CHUNK * width * 2 + 2 * C_GROUPS * C_CHUNK * C_CHUNK * 4 \
        + 8 * C_CHUNK * width * 4
    return pl.pallas_call(
        _gmlp_body,
        out_shape=jax.ShapeDtypeStruct((t, width), BF16),
        grid=(t // C_CHUNK,),
        in_specs=[pl.BlockSpec((C_CHUNK, width), lambda i: (i, u_col // width)),
                  pl.BlockSpec((C_CHUNK, width), lambda i: (i, u_col // width + 1)),
                  pl.BlockSpec((1, width), lambda i: (0, 0)),
                  pl.BlockSpec((1, width), lambda i: (0, 0)),
                  pl.BlockSpec(w_s.shape, lambda i: (0, 0, 0)),
                  pl.BlockSpec(b_s_t.shape, lambda i: (0, 0))],
        out_specs=pl.BlockSpec((C_CHUNK, width), lambda i: (i, 0)),
        compiler_params=_params(("arbitrary",), vmem),
        name="gmlp",
    )(rest, rest, ln_g.reshape(1, width), ln_b.reshape(1, width), w_s, b_s_t)


def _conv_body(h_ref, bg_ref, cg_ref, w_ref, o_ref, carry_ref):
    @pl.when(pl.program_id(1) == 0)
    def _():
        carry_ref[...] = jnp.zeros_like(carry_ref)

    ch = cg_ref[...] * h_ref[...]
    rows = ch.shape[0]
    row = lax.broadcasted_iota(jnp.int32, ch.shape, 0)
    pm1 = carry_ref[7:8, :]
    pm2 = carry_ref[6:7, :]
    s1 = jnp.where(row == 0, pm1, pltpu.roll(ch, 1, 0))
    s2 = jnp.where(row == 0, pm2, jnp.where(row == 1, pm1, pltpu.roll(ch, 2, 0)))
    z = w_ref[0:1, :] * s2 + w_ref[1:2, :] * s1 + w_ref[2:3, :] * ch
    o_ref[...] = (bg_ref[...] * z).astype(o_ref.dtype)
    carry_ref[...] = ch[rows - 8:, :]


def _short_conv(rest3, h_col, width, conv_w, tr=256):
    b, s, _ = rest3.shape
    tr = min(tr, s)
    assert h_col % width == 0 and conv_w.shape[0] == D_CONV_WIDTH
    cb = h_col // width
    vmem = 2 * 3 * tr * width * 4 + 2 * tr * width * 2 + 6 * tr * width * 4
    return pl.pallas_call(
        _conv_body,
        out_shape=jax.ShapeDtypeStruct((b, s, width), BF16),
        grid=(b, s // tr),
        in_specs=[pl.BlockSpec((None, tr, width), lambda bi, i: (bi, i, cb)),
                  pl.BlockSpec((None, tr, width), lambda bi, i: (bi, i, cb + 1)),
                  pl.BlockSpec((None, tr, width), lambda bi, i: (bi, i, cb + 2)),
                  pl.BlockSpec(conv_w.shape, lambda bi, i: (0, 0))],
        out_specs=pl.BlockSpec((None, tr, width), lambda bi, i: (bi, i, 0)),
        scratch_shapes=[pltpu.VMEM((8, width), F32)],
        compiler_params=_params(("arbitrary", "arbitrary"), vmem),
        name="short_conv",
    )(rest3, rest3, rest3, conv_w)


def _merge_body(chunk, *refs):
    br_refs = refs[:N_BRANCH]
    gate_refs = refs[N_BRANCH:2 * N_BRANCH]
    w_ref, o_ref, wb_ref = refs[2 * N_BRANCH:]

    @pl.when(pl.program_id(1) == 0)
    def _():
        for n in range(N_BRANCH):
            _cast_rows(lambda rows, n=n: w_ref[n, rows, :], wb_ref.at[n], w_ref.shape[1], chunk)

    acc = None
    for n in range(N_BRANCH):
        proj = jnp.dot(br_refs[n][...], wb_ref[n], preferred_element_type=F32)
        term = jax.nn.sigmoid(gate_refs[n][...].astype(F32)) * proj
        acc = term if acc is None else acc + term
    o_ref[...] = acc.astype(o_ref.dtype)


def _merge(branches, gates, w_branch, layer, tn=512, tm=1024):
    t, width = branches[0].shape
    d = w_branch.shape[-1]
    tn = min(tn, d)
    tm = min(tm, t)
    assert d % tn == 0 and t % tm == 0 and gates.shape == (t, N_BRANCH * d)
    chunk = _pick_tile(width, (256, 128, 8))
    gate_spec = lambda n: pl.BlockSpec((tm, tn), lambda j, i: (i, (n * d) // tn + j))
    vmem = 2 * N_BRANCH * (tm * width * 2 + tm * tn * 2 + width * tn * 4) + N_BRANCH * width * tn * 2 \
        + 2 * tm * tn * 2 + 4 * tm * tn * 4
    return pl.pallas_call(
        functools.partial(_merge_body, chunk),
        out_shape=jax.ShapeDtypeStruct((t, d), BF16),
        grid=(d // tn, t // tm),
        in_specs=[pl.BlockSpec((tm, width), lambda j, i: (i, 0))] * N_BRANCH
        + [gate_spec(n) for n in range(N_BRANCH)]
        + [pl.BlockSpec((None, N_BRANCH, width, tn), lambda j, i: (layer, 0, 0, j))],
        out_specs=pl.BlockSpec((tm, tn), lambda j, i: (i, j)),
        scratch_shapes=[pltpu.VMEM((N_BRANCH, width, tn), BF16)],
        compiler_params=_params(("arbitrary", "arbitrary"), vmem),
        name="merge",
    )(*branches, *([gates] * N_BRANCH), w_branch)


def _up_body(chunk, h_ref, wg_ref, wu_ref, o_ref, wgb_ref, wub_ref):
    @pl.when(pl.program_id(1) == 0)
    def _():
        _cast_rows(lambda rows: wg_ref[rows, :], wgb_ref, wg_ref.shape[0], chunk)
        _cast_rows(lambda rows: wu_ref[rows, :], wub_ref, wu_ref.shape[0], chunk)

    h = h_ref[...]
    g = jnp.dot(h, wgb_ref[...], preferred_element_type=F32)
    u = jnp.dot(h, wub_ref[...], preferred_element_type=F32)
    o_ref[...] = (g * jax.nn.sigmoid(g) * u).astype(o_ref.dtype)


def _swiglu_up(h, wg, wu, tm=1024):
    t, k = h.shape
    n_e, _, f = wg.shape
    tn = _pick_tile(f, (256, 128))
    tm = min(tm, t)
    nj = f // tn
    chunk = _pick_tile(k, (256, 128, 8))
    w_spec = pl.BlockSpec((None, k, tn), lambda j, i: (j // nj, 0, j % nj))
    vmem = 2 * tm * k * 2 + 4 * k * tn * 4 + 2 * k * tn * 2 + 2 * tm * tn * 2 + 4 * tm * tn * 4
    return pl.pallas_call(
        functools.partial(_up_body, chunk),
        out_shape=jax.ShapeDtypeStruct((t, n_e * f), BF16),
        grid=(n_e * nj, t // tm),
        in_specs=[pl.BlockSpec((tm, k), lambda j, i: (i, 0)), w_spec, w_spec],
        out_specs=pl.BlockSpec((tm, tn), lambda j, i: (i, j)),
        scratch_shapes=[pltpu.VMEM((k, tn), BF16), pltpu.VMEM((k, tn), BF16)],
        compiler_params=_params(("arbitrary", "arbitrary"), vmem),
        name="swiglu_up",
    )(h, wg, wu)


def _round_body(w_ref, o_ref):
    o_ref[...] = w_ref[...].astype(o_ref.dtype)


def _round_bf16(w, layer, tr=512):
    _, k, n = w.shape
    tr = _pick_tile(k, (tr, 256, 128, 8))
    return pl.pallas_call(
        _round_body,
        out_shape=jax.ShapeDtypeStruct((k, n), BF16),
        grid=(k // tr,),
        in_specs=[pl.BlockSpec((None, tr, n), lambda i: (layer, i, 0))],
        out_specs=pl.BlockSpec((tr, n), lambda i: (i, 0)),
        compiler_params=_params(("arbitrary",), 2 * tr * n * 6),
        name="round_bf16",
    )(w)


def _down_body(a_ref, w_ref, o_ref):
    o_ref[...] = jnp.dot(a_ref[...], w_ref[...], preferred_element_type=F32)


def _swiglu_down(a, wd, layer, tn=512, tm=512):
    t, k = a.shape
    n = wd.shape[2]
    tn = min(tn, n)
    tm = min(tm, t)
    assert n % tn == 0 and t % tm == 0
    wb = _round_bf16(wd, layer)
    vmem = 2 * tm * k * 2 + 2 * k * tn * 2 + 2 * tm * tn * 4 + tm * tn * 4
    return pl.pallas_call(
        _down_body,
        out_shape=jax.ShapeDtypeStruct((t, n), F32),
        grid=(n // tn, t // tm),
        in_specs=[pl.BlockSpec((tm, k), lambda j, i: (i, 0)),
                  pl.BlockSpec((k, tn), lambda j, i: (0, j))],
        out_specs=pl.BlockSpec((tm, tn), lambda j, i: (i, j)),
        compiler_params=_params(("arbitrary", "arbitrary"), vmem),
        name="swiglu_down",
    )(a, wb)


MOE_TILE = 256


def _dispatch_body(n_tok, pos1_ref, pos2_ref, h_ref, o_ref, src_ref, buf_ref, sem):
    r = pl.program_id(0)

    def row_copy(tile, slot, k):
        return pltpu.make_async_copy(h_ref.at[pl.ds(src_ref[tile * MOE_TILE + k], 1), :],
                                     buf_ref.at[slot, pl.ds(k, 1), :], sem.at[slot])

    def start_tile(tile, slot):
        def pair(k2, c):
            row_copy(tile, slot, 2 * k2).start(priority=0)
            row_copy(tile, slot, 2 * k2 + 1).start(priority=1)
            return c
        lax.fori_loop(0, MOE_TILE // 2, pair, 0, unroll=4)

    @pl.when(r == 0)
    def _():
        def clear(p, c):
            src_ref[p] = 0
            return c

        def invert(t, c):
            src_ref[pos1_ref[t]] = t
            src_ref[pos2_ref[t]] = t
            return c

        lax.fori_loop(0, src_ref.shape[0], clear, 0, unroll=8)
        lax.fori_loop(0, n_tok, invert, 0, unroll=8)
        start_tile(0, 0)

    @pl.when(r + 1 < pl.num_programs(0))
    def _():
        start_tile(r + 1, (r + 1) % 2)

    slot = r % 2
    pltpu.make_async_copy(h_ref.at[pl.ds(0, MOE_TILE), :], buf_ref.at[slot], sem.at[slot]).wait()
    o_ref[...] = buf_ref[slot]


def _moe_dispatch(hp, pos1, pos2, n_rows):
    t, w = hp.shape
    assert n_rows % MOE_TILE == 0
    return pl.pallas_call(
        functools.partial(_dispatch_body, t),
        out_shape=jax.ShapeDtypeStruct((n_rows, w), hp.dtype),
        grid_spec=pltpu.PrefetchScalarGridSpec(
            num_scalar_prefetch=2,
            grid=(n_rows // MOE_TILE,),
            in_specs=[pl.BlockSpec(memory_space=pl.ANY)],
            out_specs=pl.BlockSpec((MOE_TILE, w), lambda r, p1, p2: (r, 0)),
            scratch_shapes=[pltpu.SMEM((n_rows,), jnp.int32),
                            pltpu.VMEM((2, MOE_TILE, w), hp.dtype),
                            pltpu.SemaphoreType.DMA((2,))]),
        compiler_params=_params(("arbitrary",), 6 * MOE_TILE * w * 4),
        name="moe_dispatch",
    )(pos1, pos2, hp)


def _group_body(swiglu, chunk, te_ref, nused_ref, *refs):
    if swiglu:
        x_ref, wg_ref, wu_ref, o_ref, wgb_ref, wub_ref = refs
        pairs = ((wg_ref, wgb_ref), (wu_ref, wub_ref))
    else:
        x_ref, w_ref, o_ref, wb_ref = refs
        pairs = ((w_ref, wb_ref),)
    r = pl.program_id(1)
    expert = te_ref[r]
    prev = te_ref[jnp.maximum(r - 1, 0)]

    @pl.when(jnp.logical_or(r == 0, expert != prev))
    def _():
        for src_ref, dst_ref in pairs:
            _cast_rows(lambda rows, s=src_ref: s[rows, :], dst_ref, src_ref.shape[0], chunk)

    @pl.when(r < nused_ref[0])
    def _():
        if swiglu:
            lo, hi = _unpack_bf16_pairs(x_ref[...])
            half = lo.shape[1]

            def proj(wb):
                return jnp.dot(lo, wb[0:half, :], preferred_element_type=F32) \
                    + jnp.dot(hi, wb[half:, :], preferred_element_type=F32)

            g = proj(wgb_ref)
            o_ref[...] = (g * jax.nn.sigmoid(g) * proj(wub_ref)).astype(o_ref.dtype)
        else:
            o_ref[...] = jnp.dot(x_ref[...], wb_ref[...],
                                 preferred_element_type=F32).astype(o_ref.dtype)

    @pl.when(r >= nused_ref[0])
    def _():
        o_ref[...] = jnp.zeros_like(o_ref)


def _grouped_matmul(xs, weights, layer, tile_expert, n_used, out_dtype, tn, name):
    p, kx = xs.shape
    swiglu = len(weights) == 2
    _, _, k, n = weights[0].shape
    assert k == (2 * kx if swiglu else kx) and n % tn == 0 and p % MOE_TILE == 0
    chunk = _pick_tile(k, (256, 128, 8))
    w_spec = pl.BlockSpec((None, None, k, tn), lambda j, r, te, nu: (layer, te[r], 0, j))
    osz = jnp.dtype(out_dtype).itemsize
    vmem = 2 * MOE_TILE * kx * xs.dtype.itemsize + len(weights) * (2 * k * tn * 4 + k * tn * 2) \
        + 2 * MOE_TILE * tn * osz + 6 * MOE_TILE * tn * 4 + 2 * MOE_TILE * k * 2
    return pl.pallas_call(
        functools.partial(_group_body, swiglu, chunk),
        out_shape=jax.ShapeDtypeStruct((p, n), out_dtype),
        grid_spec=pltpu.PrefetchScalarGridSpec(
            num_scalar_prefetch=2,
            grid=(n // tn, p // MOE_TILE),
            in_specs=[pl.BlockSpec((MOE_TILE, kx), lambda j, r, te, nu: (r, 0))]
            + [w_spec] * len(weights),
            out_specs=pl.BlockSpec((MOE_TILE, tn), lambda j, r, te, nu: (r, j)),
            scratch_shapes=[pltpu.VMEM((k, tn), BF16)] * len(weights)),
        compiler_params=_params(("arbitrary", "arbitrary"), vmem),
        name=name,
    )(tile_expert, n_used, xs, *weights)


def _combine_body(tc, n_experts, pos1_ref, pos2_ref, route_ref, ye_ref, o_ref, buf_ref, sem):
    i = pl.program_id(0)

    def row_copies(tile, slot, k):
        t = tile * tc + k
        return (pltpu.make_async_copy(ye_ref.at[pl.ds(pos1_ref[t], 1), :],
                                      buf_ref.at[slot, 0, pl.ds(k, 1), :], sem.at[slot]),
                pltpu.make_async_copy(ye_ref.at[pl.ds(pos2_ref[t], 1), :],
                                      buf_ref.at[slot, 1, pl.ds(k, 1), :], sem.at[slot]))

    def start_tile(tile, slot):
        def body(k, c):
            first, second = row_copies(tile, slot, k)
            first.start(priority=0)
            second.start(priority=1)
            return c
        lax.fori_loop(0, tc, body, 0, unroll=8)

    def wait_tile(slot):
        for half in range(2):
            pltpu.make_async_copy(ye_ref.at[pl.ds(0, tc), :], buf_ref.at[slot, half],
                                  sem.at[slot]).wait()

    @pl.when(i == 0)
    def _():
        start_tile(0, 0)

    @pl.when(i + 1 < pl.num_programs(0))
    def _():
        start_tile(i + 1, (i + 1) % 2)

    slot = i % 2
    wait_tile(slot)
    g1 = route_ref[:, n_experts + 2:n_experts + 3]
    g2 = route_ref[:, n_experts + 3:n_experts + 4]
    o_ref[...] = g1 * buf_ref[slot, 0] + g2 * buf_ref[slot, 1]


def _moe_combine(ye, route, pos1, pos2, n_experts, tc=128):
    t = route.shape[0]
    d = ye.shape[1]
    tc = min(tc, t)
    assert t % tc == 0
    return pl.pallas_call(
        functools.partial(_combine_body, tc, n_experts),
        out_shape=jax.ShapeDtypeStruct((t, d), F32),
        grid_spec=pltpu.PrefetchScalarGridSpec(
            num_scalar_prefetch=2,
            grid=(t // tc,),
            in_specs=[pl.BlockSpec((tc, LANE), lambda i, p1, p2: (i, 0)),
                      pl.BlockSpec(memory_space=pl.ANY)],
            out_specs=pl.BlockSpec((tc, d), lambda i, p1, p2: (i, 0)),
            scratch_shapes=[pltpu.VMEM((2, 2, tc, d), F32), pltpu.SemaphoreType.DMA((2,))]),
        compiler_params=_params(("arbitrary",), 4 * tc * d * 4 + 4 * tc * d * 4),
        name="moe_combine",
    )(pos1, pos2, route, ye)


def _moe(hp, route, rank, counts, layer_idx, w_gate, w_up, w_down):
    t = hp.shape[0]
    n_e = w_gate.shape[1]
    n_rows = TOP_K * t + n_e * MOE_TILE
    n_tiles = n_rows // MOE_TILE
    i1 = route[:, n_e].astype(jnp.int32)
    i2 = route[:, n_e + 1].astype(jnp.int32)
    cnt = counts[0, :n_e].astype(jnp.int32)
    tiles = (cnt + MOE_TILE - 1) // MOE_TILE
    tile_end = jnp.cumsum(tiles)
    row0 = (tile_end - tiles) * MOE_TILE
    slot = row0[None, :] + rank[:, :n_e].astype(jnp.int32)
    experts = jnp.arange(n_e, dtype=jnp.int32)[None, :]
    pos1 = jnp.sum(jnp.where(experts == i1[:, None], slot, 0), axis=1).astype(jnp.int32)
    pos2 = jnp.sum(jnp.where(experts == i2[:, None], slot, 0), axis=1).astype(jnp.int32)
    n_used = tile_end[-1:].astype(jnp.int32)
    tile_ids = jnp.minimum(jnp.arange(n_tiles, dtype=jnp.int32), n_used[0] - 1)
    tile_expert = jnp.sum(tile_ids[:, None] >= tile_end[None, :], axis=1).astype(jnp.int32)

    xs = _moe_dispatch(hp, pos1, pos2, n_rows)
    f = w_gate.shape[-1]
    a = _grouped_matmul(xs, (w_gate, w_up), layer_idx, tile_expert, n_used, BF16,
                        _pick_tile(f, (512, 256, 128)), "moe_up")
    ye = _grouped_matmul(a, (w_down,), layer_idx, tile_expert, n_used, F32,
                         _pick_tile(w_down.shape[-1], (1024, 512, 256, 128)), "moe_down")
    return _moe_combine(ye, route, pos1, pos2, n_e)


def _rel_bucket(dist):
    max_exact = REL_BUCKETS // 2
    d = jnp.maximum(dist, 0)
    df = jnp.maximum(d, max_exact).astype(F32)
    large = max_exact + (jnp.log(df / max_exact) / math.log(REL_MAX_DIST / max_exact)
                         * (REL_BUCKETS - max_exact)).astype(jnp.int32)
    large = jnp.minimum(large, REL_BUCKETS - 1)
    return jnp.where(d < max_exact, d, large)


def _bias_body(n_heads, head0, head_stride, bucket_ref, tab_ref, o_ref):
    bucket = bucket_ref[...]
    base = head0 + head_stride * pl.program_id(0)
    for h in range(n_heads):
        acc = jnp.zeros(bucket.shape, F32)
        for b in range(REL_BUCKETS):
            acc = jnp.where(bucket == b, tab_ref[b, base + h], acc)
        o_ref[h] = acc


def _bias_tiles(rel_bias, buckets, n_heads, head0, head_stride):
    n_tiles, q, k = buckets.shape
    return pl.pallas_call(
        functools.partial(_bias_body, n_heads, head0, head_stride),
        out_shape=jax.ShapeDtypeStruct((n_tiles, n_heads, q, k), F32),
        grid=(n_tiles,),
        in_specs=[pl.BlockSpec((None, q, k), lambda t: (t, 0, 0)),
                  pl.BlockSpec(memory_space=pltpu.SMEM)],
        out_specs=pl.BlockSpec((None, n_heads, q, k), lambda t: (t, 0, 0, 0)),
        compiler_params=_params(("arbitrary",), 4 * (n_heads + 1) * q * k * 4),
        name="bias_tiles",
    )(buckets, rel_bias)


def _bias_tiles_a(rel_bias, n_heads, seq):
    n_blk = seq // A_BLOCK
    qi = jnp.arange(A_BLOCK)[:, None]
    kj = jnp.arange(A_BLOCK)[None, :]
    dist = jnp.arange(n_blk)[:, None, None] * A_BLOCK + (qi - kj)[None]
    return _bias_tiles(rel_bias, _rel_bucket(dist), n_heads, 0, 0)


def _bias_tiles_b(rel_bias, head0, n_heads):
    qi = jnp.arange(B_BLOCK)[:, None]
    kj = jnp.arange(2 * B_BLOCK)[None, :]
    sub_dist = qi + B_BLOCK - kj
    buckets = jnp.stack([_rel_bucket(sub_dist * dil) for _, dil in B_GROUPS])
    return _bias_tiles(rel_bias, buckets, n_heads, head0, n_heads)


def _hybrid_mixer(h, bsz, seq, layer, w_in_t, lat_g, w_uk, w_uv, c_ln_g, c_ln_b, c_w_s, c_b_s,
                  d_conv, w_branch, w_out, bias_a, bias_b):
    t, d = h.shape
    width = d // N_BRANCH
    n_lat = w_uk.shape[0]
    a_heads = width // HEAD_DIM
    n_groups = len(B_GROUPS)
    iq_width = IDX_HEADS * IDX_DIM
    lat_col = width
    iq_col = lat_col + n_lat
    ik_col = iq_col + iq_width
    front = ik_col
    shift = IDX_DIM + IDX_HEADS
    qkv_w = 3 * n_groups * width
    rest_w = qkv_w + 2 * width + 3 * width + N_BRANCH * d
    assert front % 512 == 0 and w_in_t.shape[1] == front + shift + rest_w and shift <= LANE

    pa = _matmul_t(h, w_in_t, layer, row0=0, n_out=front, tm=1024, name="proj_front")
    pik = _matmul_t(h, w_in_t, layer, row0=front, n_out=LANE, name="proj_index_key")
    gate_col = qkv_w + 5 * width
    rest = _matmul_t(h, w_in_t, layer, row0=front + shift, n_out=gate_col, tm=1024,
                     name="proj_rest")
    gates = _matmul_t(h, w_in_t, layer, row0=front + shift + gate_col, n_out=N_BRANCH * d,
                      tm=1024, out_dtype=BF16, name="proj_gates")

    pa3 = pa.reshape(bsz, seq, front)
    pik3 = pik.reshape(bsz, seq, LANE)
    rest3 = rest.reshape(bsz, seq, gate_col)

    lat, latt, ikt = _latnorm(pa3, pik3, lat_g, lat_col, n_lat)
    wuk_t = jnp.transpose(w_uk, (1, 2, 0)).astype(BF16)
    wuv_t = jnp.transpose(w_uv, (1, 0, 2)).astype(BF16)
    o_a = _dsa(pa3, pik3, lat, latt, ikt, wuk_t, wuv_t, bias_a,
               q_width=width, iq_col=iq_col, iq_width=iq_width,
               n_variants=DSA_KEY_RANGE_VARIANTS).reshape(t, width)

    outs, lses = [], []
    for g, (window, dil) in enumerate(B_GROUPS):
        o, l = _dilated(rest3, bias_b, g, n_groups, window, dil, width)
        outs.append(o)
        lses.append(l)
    o_b = _mixture(outs, lses)

    o_c = _gmlp(rest, qkv_w, width, c_ln_g, c_ln_b, c_w_s, c_b_s.T)
    o_d = _short_conv(rest3, qkv_w + 2 * width, width, d_conv).reshape(t, width)

    mixed = _merge([o_a, o_b, o_c, o_d], gates, w_branch, layer)
    return _matmul(mixed, w_out, layer, tm=1024, name="proj_out")


def kernel(x, c, w_ada, b_ada, ada_table, rel_bias, norm_pre_mix, norm_post_mix, norm_pre_ffn,
           norm_post_ffn, w_in, a_lat_norm, a_w_uk, a_w_uv, c_ln_g, c_ln_b, c_w_s, c_b_s, d_conv,
           w_branch, w_out, ffn_w_gate, ffn_w_up, ffn_w_down, moe_router, moe_w_gate, moe_w_up,
           moe_w_down):
    bsz, seq, d = x.shape
    depth = w_in.shape[0]
    t = bsz * seq

    c_pad = jnp.pad(c, ((0, 16 - bsz % 16 if bsz % 16 else 0), (0, 0)))
    mod_shared = _ada(c_pad, w_ada, b_ada)[:bsz].reshape(bsz, N_MOD, d)
    mods = [mod_shared + ada_table[layer] for layer in range(depth)]

    w_in_t = jnp.transpose(w_in, (0, 2, 1))
    width = d // N_BRANCH
    bias_a = _bias_tiles_a(rel_bias, width // HEAD_DIM, seq)
    bias_b = _bias_tiles_b(rel_bias, width // HEAD_DIM, width // HEAD_DIM)

    h = _prenorm(x, mods[0], norm_pre_mix[0])
    for layer in range(depth):
        mod = mods[layer]
        y = _hybrid_mixer(h.reshape(t, d), bsz, seq, layer, w_in_t, a_lat_norm[layer],
                          a_w_uk[layer], a_w_uv[layer], c_ln_g[layer], c_ln_b[layer],
                          c_w_s[layer], c_b_s[layer], d_conv[layer], w_branch, w_out,
                          bias_a, bias_b)
        j = layer // 2
        dense = layer % 2 == 0
        res = _postnorm(x, y.reshape(bsz, seq, d), mod, norm_post_mix[layer], 2, nxt=(3, 4),
                        mod_next=mod, g_pre=norm_pre_ffn[layer],
                        router_t=None if dense else moe_router[j].T)
        if dense:
            x, h = res
            a = _swiglu_up(h.reshape(t, d), ffn_w_gate[j][None], ffn_w_up[j][None])
            y = _swiglu_down(a, ffn_w_down, j)
        else:
            x, hp, route, rank, counts = res
            y = _moe(hp.reshape(t, d // 2), route.reshape(t, LANE), rank.reshape(t, LANE), counts,
                     j, moe_w_gate, moe_w_up, moe_w_down)
        y3 = y.reshape(bsz, seq, d)
        if layer + 1 < depth:
            x, h = _postnorm(x, y3, mod, norm_post_ffn[layer], 5, nxt=(0, 1),
                             mod_next=mods[layer + 1], g_pre=norm_pre_mix[layer + 1])
        else:
            (x,) = _postnorm(x, y3, mod, norm_post_ffn[layer], 5)
    return x
```
